```python
import math
import jax
import jax.numpy as jnp
from jax import lax
import numpy as np

D_MODEL = 1024
BATCH = 8
SEQ = 4096
DEPTH = 2

GLA_HEADS = 4
GLA_DK = D_MODEL // 16
GLA_DV = D_MODEL // 8
GLA_RANK = 16
GLA_GATE_NORM = 16.0
ML_HEADS = 4
ML_DK = D_MODEL // 16
ML_DV = D_MODEL // 8
ML_CONV = 4
S5_GROUP = 16
S5_WIDTH = D_MODEL // 2
S5_GROUPS = S5_WIDTH // S5_GROUP
S5_STATE = 64
S5_DT_MIN = 0.001
S5_DT_MAX = 0.1
CHUNK = 64
FFN_HIDDEN = ((8 * D_MODEL + 767) // 768) * 256
N_BRANCH = 3
RMS_EPS = 1e-6

IN_SIZES = (
    GLA_HEADS * GLA_DK,
    GLA_HEADS * GLA_DK,
    GLA_HEADS * GLA_DV,
    GLA_RANK,
    GLA_HEADS * GLA_DV,
    ML_HEADS * ML_DK,
    ML_HEADS * ML_DK,
    ML_HEADS * ML_DV,
    ML_HEADS,
    ML_HEADS,
    ML_HEADS * ML_DV,
    S5_WIDTH,
    N_BRANCH * D_MODEL,
)
IN_DIM = sum(IN_SIZES)

kernel_name = 'hybrid_gla_mlstm_s5_block'


def rms_norm(x, w):
    xf = x.astype(jnp.float32)
    y = xf * lax.rsqrt(jnp.mean(xf * xf, axis=-1, keepdims=True) + RMS_EPS)
    return (y * w.astype(jnp.float32)).astype(x.dtype)


def _split(z, sizes):
    out, start = [], 0
    for s in sizes:
        out.append(z[..., start:start + s])
        start += s
    return out


def _heads(a, n):
    return a.reshape(a.shape[0], a.shape[1], n, -1)


def _to_chunks(a):
    b, t, h, d = a.shape
    return a.reshape(b, t // CHUNK, CHUNK, h, d).transpose(1, 0, 3, 2, 4)


def _from_chunks(a):
    n, b, h, l, d = a.shape
    return a.transpose(1, 0, 3, 2, 4).reshape(b, n * l, h, d)


def gla_attention(q, k, v, g):
    out_dtype = v.dtype
    f32 = jnp.float32
    bsz, _, heads, dk = q.shape
    dv = v.shape[-1]
    qc, kc, vc, gc = (_to_chunks(a.astype(f32)) for a in (q * dk ** -0.5, k, v, g))
    causal = jnp.tril(jnp.ones((CHUNK, CHUNK), dtype=bool))

    def step(state, inp):
        qi, ki, vi, gi = inp
        cum = jnp.cumsum(gi, axis=2)
        o_inter = jnp.einsum('bhld,bhdv->bhlv', qi * jnp.exp(cum), state)
        rel = cum[:, :, :, None, :] - cum[:, :, None, :, :]
        decay = jnp.exp(jnp.where(causal[:, :, None], rel, -jnp.inf))
        scores = jnp.einsum('bhid,bhjd,bhijd->bhij', qi, ki, decay)
        o = o_inter + jnp.einsum('bhij,bhjv->bhiv', scores, vi)
        last = cum[:, :, -1:, :]
        state = (jnp.exp(last[:, :, 0, :])[..., None] * state
                 + jnp.einsum('bhjd,bhjv->bhdv', ki * jnp.exp(last - cum), vi))
        return state, o

    s0 = jnp.zeros((bsz, heads, dk, dv), f32)
    _, o = lax.scan(step, s0, (qc, kc, vc, gc))
    return _from_chunks(o).astype(out_dtype)


def mlstm_memory(q, k, v, i_pre, f_pre):
    out_dtype = v.dtype
    f32 = jnp.float32
    bsz, _, heads, dk = q.shape
    dv = v.shape[-1]
    log_f = jax.nn.log_sigmoid(f_pre.astype(f32))
    qc, kc, vc = (_to_chunks(a.astype(f32)) for a in (q, k * dk ** -0.5, v))
    ic, fc = (_to_chunks(a.astype(f32)[..., None])[..., 0] for a in (i_pre, log_f))
    causal = jnp.tril(jnp.ones((CHUNK, CHUNK), dtype=bool))

    def step(carry, inp):
        mem, norm, m_prev = carry
        qi, ki, vi, ii, lfi = inp
        b = jnp.cumsum(lfi, axis=-1)
        a_inter = b + m_prev[..., None]
        d_intra = jnp.where(causal, b[..., :, None] - b[..., None, :] + ii[..., None, :], -jnp.inf)
        m_t = jnp.maximum(a_inter, jnp.max(d_intra, axis=-1))
        w_inter = jnp.exp(a_inter - m_t)
        scores = jnp.einsum('bhtd,bhsd->bhts', qi, ki) * jnp.exp(d_intra - m_t[..., None])
        num = (w_inter[..., None] * jnp.einsum('bhtd,bhdv->bhtv', qi, mem)
               + jnp.einsum('bhts,bhsv->bhtv', scores, vi))
        den = w_inter * jnp.einsum('bhtd,bhd->bht', qi, norm) + jnp.sum(scores, axis=-1)
        h = num / jnp.maximum(jnp.abs(den), jnp.exp(-m_t))[..., None]
        m_new = m_t[..., -1]
        carry_decay = jnp.exp(b[..., -1] + m_prev - m_new)
        w_key = jnp.exp(b[..., -1:] - b + ii - m_new[..., None])
        mem = carry_decay[..., None, None] * mem + jnp.einsum('bhs,bhsd,bhsv->bhdv', w_key, ki, vi)
        norm = carry_decay[..., None] * norm + jnp.einsum('bhs,bhsd->bhd', w_key, ki)
        return (mem, norm, m_new), h

    carry0 = (jnp.zeros((bsz, heads, dk, dv), f32),
              jnp.zeros((bsz, heads, dk), f32),
              jnp.zeros((bsz, heads), f32))
    _, h = lax.scan(step, carry0, (qc, kc, vc, ic, fc))
    return _from_chunks(h).astype(out_dtype)


def _complex_linear_combine(earlier, later):
    a1r, a1i, b1r, b1i = earlier
    a2r, a2i, b2r, b2i = later
    return (a2r * a1r - a2i * a1i,
            a2r * a1i + a2i * a1r,
            a2r * b1r - a2i * b1i + b2r,
            a2r * b1i + a2i * b1r + b2i)


def s5_ssm(u, lam_re, lam_im, log_step, b_re, b_im, c_re, c_im, d_skip):
    out_dtype = u.dtype
    f32 = jnp.float32
    bsz, t, _ = u.shape
    uf = u.astype(f32).reshape(bsz, t, S5_GROUPS, S5_GROUP)
    lam_re = jnp.minimum(lam_re.astype(f32), -1e-4)
    lam_im = lam_im.astype(f32)
    dt = jnp.exp(log_step.astype(f32))[:, None]
    mag = jnp.exp(lam_re * dt)
    abar_re = mag * jnp.cos(lam_im * dt)
    abar_im = mag * jnp.sin(lam_im * dt)
    inv = 1.0 / (lam_re * lam_re + lam_im * lam_im)
    z_re = ((abar_re - 1.0) * lam_re + abar_im * lam_im) * inv
    z_im = (abar_im * lam_re - (abar_re - 1.0) * lam_im) * inv
    b_re = b_re.astype(f32)
    b_im = b_im.astype(f32)
    bbar_re = z_re[..., None] * b_re - z_im[..., None] * b_im
    bbar_im = z_re[..., None] * b_im + z_im[..., None] * b_re
    bu_re = jnp.einsum('gpc,btgc->btgp', bbar_re, uf)
    bu_im = jnp.einsum('gpc,btgc->btgp', bbar_im, uf)
    a_re = jnp.broadcast_to(abar_re, (1, t) + abar_re.shape)
    a_im = jnp.broadcast_to(abar_im, (1, t) + abar_im.shape)
    _, _, x_re, x_im = lax.associative_scan(_complex_linear_combine, (a_re, a_im, bu_re, bu_im), axis=1)
    y = (jnp.einsum('gcp,btgp->btgc', c_re.astype(f32), x_re)
         - jnp.einsum('gcp,btgp->btgc', c_im.astype(f32), x_im))
    y = y.reshape(bsz, t, S5_WIDTH) + d_skip.astype(f32) * u.astype(f32)
    return y.astype(out_dtype)


def causal_dwconv(u, w, b):
    k = w.shape[0]
    y = lax.conv_general_dilated(u, w[:, None, :], window_strides=(1,), padding=[(k - 1, 0)],
                                 dimension_numbers=('NWC', 'WIO', 'NWC'),
                                 feature_group_count=u.shape[-1])
    return y + b


def hybrid_mixer(h, w_in, gla_a2, gla_a_b, gla_norm_w, ml_conv_w, ml_conv_b, ml_i_b, ml_f_b,
                 s5_lam_re, s5_lam_im, s5_log_step, s5_b_re, s5_b_im, s5_c_re, s5_c_im, s5_d,
                 s5_glu_w, s5_glu_b, proj_gla, proj_ml, proj_s5, branch_gate_b, w_out):
    bsz, t, _ = h.shape
    z = h @ w_in
    gq, gk, gv, ga, gr, mq, mk, mv, mi, mf, mo, su, zg = _split(z, IN_SIZES)

    g_log = jax.nn.log_sigmoid((ga @ gla_a2 + gla_a_b).astype(jnp.float32)) / GLA_GATE_NORM
    o_gla = gla_attention(_heads(gq, GLA_HEADS), _heads(gk, GLA_HEADS), _heads(gv, GLA_HEADS),
                          _heads(g_log, GLA_HEADS))
    o_gla = rms_norm(o_gla, gla_norm_w.reshape(GLA_HEADS, GLA_DV)).reshape(bsz, t, -1)
    y_gla = o_gla * jax.nn.silu(gr)

    qk = jax.nn.silu(causal_dwconv(jnp.concatenate([mq, mk], axis=-1), ml_conv_w, ml_conv_b))
    mq = qk[..., :ML_HEADS * ML_DK]
    mk = qk[..., ML_HEADS * ML_DK:]
    h_ml = mlstm_memory(_heads(mq, ML_HEADS), _heads(mk, ML_HEADS), _heads(mv, ML_HEADS),
                        mi + ml_i_b, mf + ml_f_b)
    y_ml = jax.nn.sigmoid(mo) * h_ml.reshape(bsz, t, -1)

    y_s5 = jax.nn.gelu(s5_ssm(su, s5_lam_re, s5_lam_im, s5_log_step, s5_b_re, s5_b_im,
                              s5_c_re, s5_c_im, s5_d))
    y_s5 = y_s5 * jax.nn.sigmoid(y_s5 @ s5_glu_w + s5_glu_b)

    gates = jax.nn.sigmoid(zg + branch_gate_b).reshape(bsz, t, N_BRANCH, D_MODEL)
    merged = (gates[:, :, 0] * (y_gla @ proj_gla)
              + gates[:, :, 1] * (y_ml @ proj_ml)
              + gates[:, :, 2] * (y_s5 @ proj_s5))
    return merged @ w_out


def swiglu_ffn(h, w_in, w_out):
    gate, up = jnp.split(h @ w_in, 2, axis=-1)
    return (jax.nn.silu(gate) * up) @ w_out


def setup_inputs(seed: int = 0) -> dict:
    key = jax.random.key(seed)
    keys = list(jax.random.split(key, 48))

    def normal(shape, scale):
        return scale * jax.random.normal(keys.pop(), shape, jnp.float32)

    def uniform(shape):
        return jax.random.uniform(keys.pop(), shape, jnp.float32)

    def gain(shape):
        return 1.0 + normal(shape, 0.05)

    L, D = DEPTH, D_MODEL
    glw = GLA_HEADS * GLA_DV
    mlw = ML_HEADS * ML_DV
    n_idx = jnp.arange(S5_STATE, dtype=jnp.float32)
    log_lo, log_hi = math.log(S5_DT_MIN), math.log(S5_DT_MAX)
    return {
        'x': normal((BATCH, SEQ, D), 1.0),
        'c': normal((BATCH, D), 1.0),
        'ada_w': normal((L, D, 6 * D), 0.1 * D ** -0.5),
        'ada_b': normal((L, 6 * D), 0.01),
        'pre1_w': gain((L, D)),
        'post1_w': gain((L, D)),
        'pre2_w': gain((L, D)),
        'post2_w': gain((L, D)),
        'w_in': normal((L, D, IN_DIM), D ** -0.5),
        'gla_a2': normal((L, GLA_RANK, GLA_HEADS * GLA_DK), GLA_RANK ** -0.5),
        'gla_a_b': normal((L, GLA_HEADS * GLA_DK), 0.1),
        'gla_norm_w': gain((L, glw)),
        'ml_conv_w': normal((L, ML_CONV, 2 * ML_HEADS * ML_DK), ML_CONV ** -0.5),
        'ml_conv_b': normal((L, 2 * ML_HEADS * ML_DK), 0.01),
        'ml_i_b': normal((L, ML_HEADS), 0.1),
        'ml_f_b': 3.0 + 3.0 * uniform((L, ML_HEADS)),
        's5_lam_re': -0.5 + normal((L, S5_GROUPS, S5_STATE), 0.01),
        's5_lam_im': math.pi * n_idx + normal((L, S5_GROUPS, S5_STATE), 0.01),
        's5_log_step': log_lo + (log_hi - log_lo) * uniform((L, S5_GROUPS)),
        's5_b_re': normal((L, S5_GROUPS, S5_STATE, S5_GROUP), (2 * S5_GROUP) ** -0.5),
        's5_b_im': normal((L, S5_GROUPS, S5_STATE, S5_GROUP), (2 * S5_GROUP) ** -0.5),
        's5_c_re': normal((L, S5_GROUPS, S5_GROUP, S5_STATE), S5_STATE ** -0.5),
        's5_c_im': normal((L, S5_GROUPS, S5_GROUP, S5_STATE), S5_STATE ** -0.5),
        's5_d': normal((L, S5_WIDTH), 1.0),
        's5_glu_w': normal((L, S5_WIDTH, S5_WIDTH), S5_WIDTH ** -0.5),
        's5_glu_b': normal((L, S5_WIDTH), 0.01),
        'proj_gla': normal((L, glw, D), glw ** -0.5),
        'proj_ml': normal((L, mlw, D), mlw ** -0.5),
        'proj_s5': normal((L, S5_WIDTH, D), S5_WIDTH ** -0.5),
        'branch_gate_b': normal((L, N_BRANCH * D), 0.01),
        'w_out': normal((L, D, D), D ** -0.5),
        'ffn_w_in': normal((L, D, 2 * FFN_HIDDEN), D ** -0.5),
        'ffn_w_out': normal((L, FFN_HIDDEN, D), FFN_HIDDEN ** -0.5),
    }


def reference(x, c, ada_w, ada_b, pre1_w, post1_w, pre2_w, post2_w, w_in, gla_a2, gla_a_b,
              gla_norm_w, ml_conv_w, ml_conv_b, ml_i_b, ml_f_b, s5_lam_re, s5_lam_im, s5_log_step,
              s5_b_re, s5_b_im, s5_c_re, s5_c_im, s5_d, s5_glu_w, s5_glu_b, proj_gla, proj_ml,
              proj_s5, branch_gate_b, w_out, ffn_w_in, ffn_w_out):
    c_act = jax.nn.silu(c)
    for l in range(DEPTH):
        mod = c_act @ ada_w[l] + ada_b[l]
        sh1, sc1, gt1, sh2, sc2, gt2 = [m[:, None, :] for m in jnp.split(mod, 6, axis=-1)]

        h = rms_norm(x, pre1_w[l]) * (1.0 + sc1) + sh1
        y = hybrid_mixer(h, w_in[l], gla_a2[l], gla_a_b[l], gla_norm_w[l], ml_conv_w[l], ml_conv_b[l],
                         ml_i_b[l], ml_f_b[l], s5_lam_re[l], s5_lam_im[l], s5_log_step[l],
                         s5_b_re[l], s5_b_im[l], s5_c_re[l], s5_c_im[l], s5_d[l], s5_glu_w[l],
                         s5_glu_b[l], proj_gla[l], proj_ml[l], proj_s5[l], branch_gate_b[l], w_out[l])
        x = x + (1.0 + gt1) * rms_norm(y, post1_w[l])

        h = rms_norm(x, pre2_w[l]) * (1.0 + sc2) + sh2
        y = swiglu_ffn(h, ffn_w_in[l], ffn_w_out[l])
        x = x + (1.0 + gt2) * rms_norm(y, post2_w[l])
    return x
```

```python
import functools
import math

import jax
import jax.numpy as jnp
from jax import lax
from jax.experimental import pallas as pl
from jax.experimental.pallas import tpu as pltpu

HEADS = 4
DK = 64
DV = 128
RANK = 16
GATE_NORM = 16.0
CONV = 4
S5_GROUP = 16
S5_STATE = 64
CHUNK = 64
S5_BLOCK = 16
RMS_EPS = 1e-6
LANES = 128
GATE_LANE = 16
VMEM_LIMIT = 56 * 1024 * 1024

HK = HEADS * DK
HV = HEADS * DV
F32 = jnp.float32
BF16 = jnp.bfloat16
HIGHEST = lax.Precision.HIGHEST
NT = (((1,), (1,)), ((), ()))
TN = (((0,), (0,)), ((), ()))


def _params(sem):
    return pltpu.CompilerParams(dimension_semantics=sem, vmem_limit_bytes=VMEM_LIMIT)


def _const_spec(shape):
    nd = len(shape)
    return pl.BlockSpec(shape, lambda *_: (0,) * nd, pipeline_mode=pl.Buffered(1))


def _sigmoid(x):
    return 1.0 / (1.0 + jnp.exp(-x))


def _silu(x):
    return x * _sigmoid(x)


def _log_sigmoid(x):
    return jnp.minimum(x, 0.0) - jnp.log1p(jnp.exp(-jnp.abs(x)))


def _gelu_tanh(x):
    c = math.sqrt(2.0 / math.pi)
    return 0.5 * x * (1.0 + jnp.tanh(c * (x + 0.044715 * (x * x * x))))


def _rms(x, w):
    ms = jnp.mean(x * x, axis=-1, keepdims=True)
    return x * lax.rsqrt(ms + RMS_EPS) * w


def _ada_kernel(c_ref, w_ref, b_ref, o_ref):
    ca = _silu(c_ref[...]).astype(BF16)
    o_ref[0] = jnp.dot(ca, w_ref[0].astype(BF16), preferred_element_type=F32) + b_ref[0]


def _ada_mod(c, ada_w, ada_b):
    nl, d, n = ada_w.shape
    b = c.shape[0]
    tn = 1024
    return pl.pallas_call(
        _ada_kernel,
        grid=(nl, n // tn),
        in_specs=[pl.BlockSpec((b, d), lambda l, j: (0, 0)),
                  pl.BlockSpec((1, d, tn), lambda l, j: (l, 0, j)),
                  pl.BlockSpec((1, 1, tn), lambda l, j: (l, 0, j))],
        out_specs=pl.BlockSpec((1, b, tn), lambda l, j: (l, 0, j)),
        out_shape=jax.ShapeDtypeStruct((nl, b, n), F32),
        compiler_params=_params(("arbitrary", "arbitrary")),
        name="ada_mod",
    )(c, ada_w, ada_b.reshape(nl, 1, n))


N_GLA = 2 * HK + 2 * HV
N_ML = 2 * HK + 2 * HV
N_SMALL = 2 * LANES


def _inproj_kernel(x_ref, mod_ref, nw_ref, w_ref, zg_ref, zm_ref, zs_ref, zsm_ref, zgate_ref, *, offs):
    x = x_ref[0]
    mod = mod_ref[0]
    h = _rms(x, nw_ref[...]) * (1.0 + mod[1:2]) + mod[0:1]
    hb = h.astype(BF16)
    for ref, (a, b) in zip((zg_ref, zm_ref, zs_ref, zsm_ref, zgate_ref), offs):
        ref[0] = jnp.dot(hb, w_ref[:, a:b], preferred_element_type=F32).astype(ref.dtype)


def _inproj(x, mod, nw, w, *, tm, s5w, ngate):
    b, t, d = x.shape
    sizes = (N_GLA, N_ML, s5w, N_SMALL, ngate)
    offs, a = [], 0
    for s in sizes:
        offs.append((a, a + s))
        a += s
    dts = (BF16, BF16, BF16, F32, BF16)
    tok = lambda n: pl.BlockSpec((1, tm, n), lambda i, j: (i, j, 0))
    return pl.pallas_call(
        functools.partial(_inproj_kernel, offs=tuple(offs)),
        grid=(b, t // tm),
        in_specs=[tok(d),
                  pl.BlockSpec((1, 6, d), lambda i, j: (i, 0, 0)),
                  pl.BlockSpec((1, d), lambda i, j: (0, 0)),
                  _const_spec(w.shape)],
        out_specs=[tok(s) for s in sizes],
        out_shape=[jax.ShapeDtypeStruct((b, t, s), dt) for s, dt in zip(sizes, dts)],
        compiler_params=_params(("parallel", "parallel")),
        name="in_proj",
    )(x, mod, nw, w)


def _block_diag_rows(a, width):
    lane_head = lax.broadcasted_iota(jnp.int32, a.shape, 1) // width
    zero = jnp.zeros_like(a)
    return jnp.concatenate([jnp.where(lane_head == h, a, zero) for h in range(HEADS)], axis=0)


def _head_mask(rows_per_head, cols_per_head):
    r = lax.broadcasted_iota(jnp.int32, (HEADS * rows_per_head, HEADS * cols_per_head), 0) // rows_per_head
    c = lax.broadcasted_iota(jnp.int32, (HEADS * rows_per_head, HEADS * cols_per_head), 1) // cols_per_head
    return r == c


def _lower_tri(n):
    r = lax.broadcasted_iota(jnp.int32, (n, n), 0)
    c = lax.broadcasted_iota(jnp.int32, (n, n), 1)
    return (c <= r).astype(F32)


def _gla_levels(q, k, g, c):
    n = q.shape[0]
    row = lax.broadcasted_iota(jnp.int32, q.shape, 0)
    zero = jnp.zeros_like(q)
    out = [(q, k, 1)]
    hs = 1
    while 2 * hs <= n:
        blk = 2 * hs
        pos = row % blk
        upper = pos >= hs
        if hs == 1:
            x = jnp.where(upper, g, 0.0)
        elif hs == 2:
            g_prev = pltpu.roll(g, 1, 0)
            g_next = pltpu.roll(g, n - 1, 0)
            x = jnp.where(pos == 2, g, jnp.where(pos == 3, g + g_prev, jnp.where(pos == 0, g_next, 0.0)))
        else:
            c3 = c.reshape(n // blk, blk, c.shape[1])
            anchor = jnp.broadcast_to(c3[:, hs - 1:hs, :], c3.shape).reshape(c.shape)
            x = -jnp.abs(c - anchor)
        e = jnp.exp(x)
        out.append((jnp.where(upper, q * e, zero), jnp.where(upper, zero, k * e), blk))
        hs = blk
    return out


def _gla_kernel(zg_ref, zs_ref, a2_ref, ab_ref, nw_ref, y_ref, st_ref, *, n_chunks):
    @pl.when(pl.program_id(1) == 0)
    def _():
        st_ref[...] = jnp.zeros_like(st_ref)

    L = CHUNK
    tri = _lower_tri(L)
    stmask = _head_mask(DV, DK)
    col = lax.broadcasted_iota(jnp.int32, (L, HK), 1) % DK
    rowi = lax.broadcasted_iota(jnp.int32, (L, HK), 0)
    for ci in range(n_chunks):
        r0 = ci * L
        zs = zs_ref[0, r0:r0 + L, 0:LANES]
        xg = jnp.dot(zs, a2_ref[...], precision=HIGHEST, preferred_element_type=F32) + ab_ref[...]
        g = _log_sigmoid(xg) * (1.0 / GATE_NORM)
        c = jnp.dot(tri, g, precision=HIGHEST, preferred_element_type=F32)
        q = zg_ref[0, r0:r0 + L, 0:HK].astype(F32) * (DK ** -0.5)
        k = zg_ref[0, r0:r0 + L, HK:2 * HK].astype(F32)
        v = zg_ref[0, r0:r0 + L, 2 * HK:2 * HK + HV]
        gr = zg_ref[0, r0:r0 + L, 2 * HK + HV:2 * HK + 2 * HV].astype(F32)

        scores = jnp.zeros((L, HK), F32)
        for ql, kl, blk in _gla_levels(q, k, g, c):
            s = lax.dot_general(ql.astype(BF16), _block_diag_rows(kl.astype(BF16), DK), NT,
                                preferred_element_type=F32)
            valid = (rowi == col) if blk == 1 else ((rowi // blk) == (col // blk))
            scores = scores + jnp.where(valid, s, 0.0)

        st = st_ref[...]
        o = jnp.dot(scores.astype(BF16), _block_diag_rows(v, DV), preferred_element_type=F32)
        o = o + lax.dot_general((q * jnp.exp(c)).astype(BF16), st.astype(BF16), NT,
                                preferred_element_type=F32)
        c_last = c[L - 1:L]
        kd = (k * jnp.exp(c_last - c)).astype(BF16)
        upd = lax.dot_general(v, kd, TN, preferred_element_type=F32)
        st_ref[...] = jnp.exp(c_last) * st + jnp.where(stmask, upd, 0.0)

        parts = []
        for h in range(HEADS):
            oh = o[:, h * DV:(h + 1) * DV]
            parts.append(oh * lax.rsqrt(jnp.mean(oh * oh, axis=-1, keepdims=True) + RMS_EPS))
        on = jnp.concatenate(parts, axis=1) * nw_ref[...]
        y_ref[0, r0:r0 + L, :] = (on * _silu(gr)).astype(y_ref.dtype)


def _gla(zg, zs, a2p, ab, nw, *, tt):
    b, t, _ = zg.shape
    return pl.pallas_call(
        functools.partial(_gla_kernel, n_chunks=tt // CHUNK),
        grid=(b, t // tt),
        in_specs=[pl.BlockSpec((1, tt, N_GLA), lambda i, j: (i, j, 0)),
                  pl.BlockSpec((1, tt, N_SMALL), lambda i, j: (i, j, 0)),
                  pl.BlockSpec((LANES, HK), lambda i, j: (0, 0)),
                  pl.BlockSpec((1, HK), lambda i, j: (0, 0)),
                  pl.BlockSpec((1, HV), lambda i, j: (0, 0))],
        out_specs=pl.BlockSpec((1, tt, HV), lambda i, j: (i, j, 0)),
        out_shape=jax.ShapeDtypeStruct((b, t, HV), BF16),
        scratch_shapes=[pltpu.VMEM((HV, HK), F32)],
        compiler_params=_params(("parallel", "arbitrary")),
        name="gla",
    )(zg, zs, a2p, ab, nw)


def _sel(n_out, per_head, lane0):
    r = lax.broadcasted_iota(jnp.int32, (LANES, n_out), 0)
    c = lax.broadcasted_iota(jnp.int32, (LANES, n_out), 1) // per_head
    return (r == c + lane0).astype(F32)


def _mlstm_kernel(zm_ref, zs_ref, cw_ref, cb_ref, ib_ref, fb_ref, y_ref,
                  mt_ref, nrm_ref, m_ref, tail_ref, *, n_chunks):
    tt = n_chunks * CHUNK

    @pl.when(pl.program_id(1) == 0)
    def _():
        mt_ref[...] = jnp.zeros_like(mt_ref)
        nrm_ref[...] = jnp.zeros_like(nrm_ref)
        m_ref[...] = jnp.zeros_like(m_ref)
        tail_ref[...] = jnp.zeros_like(tail_ref)

    qk_in = zm_ref[0, :, 0:2 * HK].astype(F32)
    ext = jnp.concatenate([tail_ref[...], qk_in], axis=0)
    acc = jnp.zeros((tt, 2 * HK), F32) + cb_ref[...]
    for j in range(CONV):
        off = 8 - (CONV - 1) + j
        acc = acc + cw_ref[j:j + 1, :] * ext[off:off + tt]
    tail_ref[...] = qk_in[tt - 8:tt]
    qk = _silu(acc)

    L = CHUNK
    tri = _lower_tri(L)
    lane = lax.broadcasted_iota(jnp.int32, (L, LANES), 1)
    gate_lanes = (lane >= GATE_LANE) & (lane < GATE_LANE + HEADS)
    rowg = lax.broadcasted_iota(jnp.int32, (L, LANES), 0)
    sel_v = _sel(HV, DV, GATE_LANE)
    sel_k = _sel(HK, DK, GATE_LANE)
    causal = (lax.broadcasted_iota(jnp.int32, (L, HK), 1) % DK) <= lax.broadcasted_iota(jnp.int32, (L, HK), 0)
    mtmask = _head_mask(DV, DK)
    ones_den = (lax.broadcasted_iota(jnp.int32, (HK, LANES), 0) // DK + GATE_LANE
                == lax.broadcasted_iota(jnp.int32, (HK, LANES), 1)).astype(BF16)
    den_row = lax.broadcasted_iota(jnp.int32, (LANES, HK), 0)
    den_head = lax.broadcasted_iota(jnp.int32, (LANES, HK), 1) // DK + GATE_LANE
    b_one_lane = lax.broadcasted_iota(jnp.int32, (L, LANES), 1)

    for ci in range(n_chunks):
        r0 = ci * L
        q = qk[r0:r0 + L, 0:HK]
        k = qk[r0:r0 + L, HK:2 * HK] * (DK ** -0.5)
        v = zm_ref[0, r0:r0 + L, 2 * HK:2 * HK + HV]
        og = zm_ref[0, r0:r0 + L, 2 * HK + HV:2 * HK + 2 * HV].astype(F32)
        ig = zs_ref[0, r0:r0 + L, 0:LANES] + ib_ref[...]
        lf = _log_sigmoid(zs_ref[0, r0:r0 + L, LANES:2 * LANES] + fb_ref[...])
        ig = jnp.where(gate_lanes, ig, 0.0)
        lf = jnp.where(gate_lanes, lf, 0.0)
        bcum = jnp.dot(tri, lf, precision=HIGHEST, preferred_element_type=F32)
        u = ig - bcum
        cmax = u
        sh = 1
        while sh < L:
            cmax = jnp.maximum(cmax, jnp.where(rowg >= sh, pltpu.roll(cmax, sh, 0), -jnp.inf))
            sh *= 2
        m_prev = m_ref[...]
        big_m = jnp.maximum(m_prev, cmax)
        m_t = bcum + big_m
        m_new = m_t[L - 1:L]
        b_last = bcum[L - 1:L]

        a_mat = jnp.where(gate_lanes, -big_m, jnp.where((lane >= GATE_LANE + 8) & (lane < GATE_LANE + 8 + HEADS), 1.0, 0.0))
        u8 = pltpu.roll(u, 8, 1)
        b_rows = []
        for h in range(HEADS):
            b_rows.append(jnp.where(b_one_lane == GATE_LANE + h, 1.0,
                                    jnp.where(b_one_lane == GATE_LANE + 8 + h, u8, 0.0)))
        b_mat = jnp.concatenate(b_rows, axis=0)
        dmat = lax.dot_general(a_mat, b_mat, NT, precision=HIGHEST, preferred_element_type=F32)
        pmat = jnp.where(causal, jnp.exp(jnp.where(causal, dmat, 0.0)), 0.0)

        qb = q.astype(BF16)
        qk_s = lax.dot_general(qb, _block_diag_rows(k.astype(BF16), DK), NT, preferred_element_type=F32)
        sm = (qk_s * pmat).astype(BF16)

        mt = mt_ref[...]
        nrm = nrm_ref[...]
        nrm_rows = jnp.where(den_row == den_head, jnp.broadcast_to(nrm, (LANES, HK)), 0.0)
        mem_ext = jnp.concatenate([mt, nrm_rows], axis=0).astype(BF16)
        inter = lax.dot_general(qb, mem_ext, NT, preferred_element_type=F32)
        v_ext = jnp.concatenate([_block_diag_rows(v, DV), ones_den], axis=1)
        intra = jnp.dot(sm, v_ext, preferred_element_type=F32)

        lw = jnp.where(gate_lanes, m_prev - big_m, 0.0)
        den = jnp.exp(lw) * inter[:, HV:] + intra[:, HV:]
        rden = 1.0 / jnp.maximum(jnp.abs(den), jnp.exp(-m_t))
        rden = jnp.where(gate_lanes, rden, 0.0)
        bc = jnp.dot(jnp.concatenate([lw, rden], axis=0), sel_v, precision=HIGHEST,
                     preferred_element_type=F32)
        hout = (jnp.exp(bc[0:L]) * inter[:, 0:HV] + intra[:, 0:HV]) * bc[L:2 * L]
        y_ref[0, r0:r0 + L, :] = (_sigmoid(og) * hout).astype(y_ref.dtype)

        wk = jnp.where(gate_lanes, b_last - m_new + u, 0.0)
        cd = jnp.where(gate_lanes[0:1], b_last + m_prev - m_new, 0.0)
        bk = jnp.dot(jnp.concatenate([wk, jnp.broadcast_to(cd, (8, LANES))], axis=0), sel_k,
                     precision=HIGHEST, preferred_element_type=F32)
        kw = k * jnp.exp(bk[0:L])
        decay = jnp.exp(bk[L:L + 1])
        upd = lax.dot_general(v, kw.astype(BF16), TN, preferred_element_type=F32)
        mt_ref[...] = decay * mt + jnp.where(mtmask, upd, 0.0)
        nrm_ref[...] = decay * nrm + jnp.sum(kw, axis=0, keepdims=True)
        m_ref[...] = m_new


def _mlstm(zm, zs, cw, cb, ibp, fbp, *, tt):
    b, t, _ = zm.shape
    return pl.pallas_call(
        functools.partial(_mlstm_kernel, n_chunks=tt // CHUNK),
        grid=(b, t // tt),
        in_specs=[pl.BlockSpec((1, tt, N_ML), lambda i, j: (i, j, 0)),
                  pl.BlockSpec((1, tt, N_SMALL), lambda i, j: (i, j, 0)),
                  pl.BlockSpec((CONV, 2 * HK), lambda i, j: (0, 0)),
                  pl.BlockSpec((1, 2 * HK), lambda i, j: (0, 0)),
                  pl.BlockSpec((1, LANES), lambda i, j: (0, 0)),
                  pl.BlockSpec((1, LANES), lambda i, j: (0, 0))],
        out_specs=pl.BlockSpec((1, tt, HV), lambda i, j: (i, j, 0)),
        out_shape=jax.ShapeDtypeStruct((b, t, HV), BF16),
        scratch_shapes=[pltpu.VMEM((HV, HK), F32), pltpu.VMEM((1, HK), F32),
                        pltpu.VMEM((1, LANES), F32), pltpu.VMEM((8, 2 * HK), F32)],
        compiler_params=_params(("parallel", "arbitrary")),
        name="mlstm",
    )(zm, zs, cw, cb, ibp, fbp)


def _s5_prep_kernel(lr_ref, li_ref, lrc_ref, lic_ref, dt_ref, btr_ref, bti_ref, ctr_ref, cti_ref,
                    t_ref, w_ref, v_ref, pa_ref, pb_ref, *, n_pow):
    P = S5_STATE
    nb = S5_BLOCK
    dt = dt_ref[0]
    lr = jnp.minimum(lr_ref[0], -1e-4)
    li = li_ref[0]
    lrc = jnp.minimum(lrc_ref[0], -1e-4)
    lic = lic_ref[0]
    dtr = jnp.exp(dt)

    def power(lr_, li_, n):
        mag = jnp.exp(lr_ * dtr * n)
        ang = li_ * dtr * n
        return mag * jnp.cos(ang), mag * jnp.sin(ang)

    ar1, ai1 = power(lr, li, 1.0)
    inv = 1.0 / (lr * lr + li * li)
    zr = ((ar1 - 1.0) * lr + ai1 * li) * inv
    zi = (ai1 * lr - (ar1 - 1.0) * li) * inv
    bbr = zr * btr_ref[0] - zi * bti_ref[0]
    bbi = zr * bti_ref[0] + zi * btr_ref[0]

    tl = (lax.broadcasted_iota(jnp.int32, (P, nb * S5_GROUP), 1) // S5_GROUP).astype(F32)
    ctr, cti = ctr_ref[0], cti_ref[0]

    def out_map(shift):
        ar, ai = power(lrc, lic, tl + shift)
        return jnp.concatenate([ctr * ar - cti * ai, -(ctr * ai + cti * ar)], axis=0)

    v_ref[0] = out_map(1.0).astype(v_ref.dtype)
    base = jnp.dot(jnp.concatenate([bbr, bbi], axis=1), out_map(0.0), precision=HIGHEST,
                   preferred_element_type=F32)
    lane = lax.broadcasted_iota(jnp.int32, base.shape, 1)
    rows = [base]
    for s in range(1, nb):
        rows.append(jnp.where(lane >= s * S5_GROUP, pltpu.roll(base, s * S5_GROUP, 1), 0.0))
    t_ref[0] = jnp.concatenate(rows, axis=0).astype(t_ref.dtype)

    se = (nb - 1 - lax.broadcasted_iota(jnp.int32, (nb * S5_GROUP, P), 0) // S5_GROUP).astype(F32)
    ar, ai = power(lr, li, se)
    bbr_t = jnp.concatenate([bbr] * nb, axis=0)
    bbi_t = jnp.concatenate([bbi] * nb, axis=0)
    w_ref[0] = jnp.concatenate([ar * bbr_t - ai * bbi_t, ar * bbi_t + ai * bbr_t], axis=1).astype(w_ref.dtype)

    pr, pi = power(lr, li, float(nb))
    pa, pb = [], []
    for _ in range(n_pow):
        pa.append(jnp.concatenate([pr, pr], axis=1))
        pb.append(jnp.concatenate([-pi, pi], axis=1))
        pr, pi = pr * pr - pi * pi, 2.0 * pr * pi
    pa_ref[0] = jnp.concatenate(pa, axis=0)
    pb_ref[0] = jnp.concatenate(pb, axis=0)


def _s5_prep(lam_re, lam_im, log_step, b_re, b_im, c_re, c_im, *, n_pow):
    g, p = lam_re.shape
    nbg = S5_BLOCK * S5_GROUP
    bt = lambda a: jnp.swapaxes(a, 1, 2)
    ct = lambda a: jnp.tile(jnp.swapaxes(a, 1, 2), (1, 1, S5_BLOCK))
    row = lambda a: a.reshape(g, 1, p)
    colv = lambda a: a.reshape(g, p, 1)
    spec = lambda *s: pl.BlockSpec((1,) + s, lambda i: (i,) + (0,) * len(s))
    return pl.pallas_call(
        functools.partial(_s5_prep_kernel, n_pow=n_pow),
        grid=(g,),
        in_specs=[spec(1, p), spec(1, p), spec(p, 1), spec(p, 1), spec(1, 1),
                  spec(S5_GROUP, p), spec(S5_GROUP, p), spec(p, nbg), spec(p, nbg)],
        out_specs=[spec(nbg, nbg), spec(nbg, 2 * p), spec(2 * p, nbg), spec(n_pow, 2 * p), spec(n_pow, 2 * p)],
        out_shape=[jax.ShapeDtypeStruct((g, nbg, nbg), BF16), jax.ShapeDtypeStruct((g, nbg, 2 * p), BF16),
                   jax.ShapeDtypeStruct((g, 2 * p, nbg), BF16), jax.ShapeDtypeStruct((g, n_pow, 2 * p), F32),
                   jax.ShapeDtypeStruct((g, n_pow, 2 * p), F32)],
        compiler_params=_params(("parallel",)),
        name="s5_prep",
    )(row(lam_re), row(lam_im), colv(lam_re), colv(lam_im), log_step.reshape(g, 1, 1),
      bt(b_re), bt(b_im), ct(c_re), ct(c_im))


def _s5_kernel(u_ref, t_ref, w_ref, v_ref, pa_ref, pb_ref, d_ref, y_ref, *, n_pow, seq_blocks):
    u = u_ref[0]
    rows = u.shape[0]
    z = jnp.dot(u, w_ref[0], preferred_element_type=F32)
    nloc = lax.broadcasted_iota(jnp.int32, z.shape, 0) % seq_blocks
    half = z.shape[1] // 2
    x = z
    for k in range(n_pow):
        sh = 1 << k
        if sh >= seq_blocks:
            break
        xs = pltpu.roll(x, sh, 0)
        prod = xs * pa_ref[0, k:k + 1, :] + pltpu.roll(xs, half, 1) * pb_ref[0, k:k + 1, :]
        x = x + jnp.where(nloc >= sh, prod, 0.0)
    x_in = jnp.where(nloc >= 1, pltpu.roll(x, 1, 0), 0.0)
    y = jnp.dot(u, t_ref[0], preferred_element_type=F32)
    y = y + jnp.dot(x_in.astype(BF16), v_ref[0], preferred_element_type=F32)
    y = y + d_ref[0] * u.astype(F32)
    y_ref[0] = y.astype(y_ref.dtype)


def _s5(u, tm, wm, vm, pa, pb, dtile, *, seq_blocks):
    g, rows, nbg = u.shape
    n_pow = pa.shape[1]
    p2 = wm.shape[2]
    spec = lambda *s: pl.BlockSpec((1,) + s, lambda i: (i,) + (0,) * len(s))
    return pl.pallas_call(
        functools.partial(_s5_kernel, n_pow=n_pow, seq_blocks=seq_blocks),
        grid=(g,),
        in_specs=[spec(rows, nbg), spec(nbg, nbg), spec(nbg, p2), spec(p2, nbg), spec(n_pow, p2),
                  spec(n_pow, p2), spec(1, nbg)],
        out_specs=spec(rows, nbg),
        out_shape=jax.ShapeDtypeStruct((g, rows, nbg), BF16),
        compiler_params=_params(("parallel",)),
        name="s5",
    )(u, tm, wm, vm, pa, pb, dtile)


def _merge_kernel(x_ref, mod_ref, yg_ref, ym_ref, ys_ref, zgate_ref, gluw_ref, glub_ref, pg_ref, pm_ref,
                  ps_ref, gb_ref, wo_ref, nw_ref, o_ref, *, d):
    mod = mod_ref[0]
    s = _gelu_tanh(ys_ref[0].astype(F32))
    s = s * _sigmoid(jnp.dot(s.astype(BF16), gluw_ref[...], preferred_element_type=F32) + glub_ref[...])
    merged = None
    for idx, (y, p) in enumerate(((yg_ref[0], pg_ref), (ym_ref[0], pm_ref), (s.astype(BF16), ps_ref))):
        gate = _sigmoid(zgate_ref[0, :, idx * d:(idx + 1) * d].astype(F32) + gb_ref[:, idx * d:(idx + 1) * d])
        term = gate * jnp.dot(y, p[...], preferred_element_type=F32)
        merged = term if merged is None else merged + term
    y = jnp.dot(merged.astype(BF16), wo_ref[...], preferred_element_type=F32)
    o_ref[0] = x_ref[0] + (1.0 + mod[2:3]) * _rms(y, nw_ref[...])


def _merge(x, mod, yg, ym, ys, zgate, gluw, glub, pg, pm, ps, gb, wo, nw, *, tm):
    b, t, d = x.shape
    tok = lambda n: pl.BlockSpec((1, tm, n), lambda i, j: (i, j, 0))
    small = lambda a: pl.BlockSpec(a.shape, lambda i, j: (0,) * a.ndim)
    return pl.pallas_call(
        functools.partial(_merge_kernel, d=d),
        grid=(b, t // tm),
        in_specs=[tok(d), pl.BlockSpec((1, 6, d), lambda i, j: (i, 0, 0)), tok(HV), tok(HV), tok(ys.shape[2]),
                  tok(zgate.shape[2]), _const_spec(gluw.shape), small(glub), _const_spec(pg.shape),
                  _const_spec(pm.shape), _const_spec(ps.shape), small(gb), _const_spec(wo.shape), small(nw)],
        out_specs=tok(d),
        out_shape=jax.ShapeDtypeStruct((b, t, d), F32),
        compiler_params=_params(("parallel", "parallel")),
        name="merge_out",
    )(x, mod, yg, ym, ys, zgate, gluw, glub, pg, pm, ps, gb, wo, nw)


def _ffn_kernel(x_ref, mod_ref, n1_ref, wi_ref, wo_ref, n2_ref, o_ref, *, hidden):
    x = x_ref[0]
    mod = mod_ref[0]
    h = (_rms(x, n1_ref[...]) * (1.0 + mod[4:5]) + mod[3:4]).astype(BF16)
    gate = jnp.dot(h, wi_ref[:, 0:hidden], preferred_element_type=F32)
    up = jnp.dot(h, wi_ref[:, hidden:2 * hidden], preferred_element_type=F32)
    act = (_silu(gate) * up).astype(BF16)
    y = jnp.dot(act, wo_ref[...], preferred_element_type=F32)
    o_ref[0] = x + (1.0 + mod[5:6]) * _rms(y, n2_ref[...])


def _ffn(x, mod, n1, wi, wo, n2, *, tm):
    b, t, d = x.shape
    hidden = wo.shape[0]
    tok = pl.BlockSpec((1, tm, d), lambda i, j: (i, j, 0))
    small = lambda a: pl.BlockSpec(a.shape, lambda i, j: (0,) * a.ndim)
    return pl.pallas_call(
        functools.partial(_ffn_kernel, hidden=hidden),
        grid=(b, t // tm),
        in_specs=[tok, pl.BlockSpec((1, 6, d), lambda i, j: (i, 0, 0)), small(n1), _const_spec(wi.shape),
                  _const_spec(wo.shape), small(n2)],
        out_specs=tok,
        out_shape=jax.ShapeDtypeStruct((b, t, d), F32),
        compiler_params=_params(("parallel", "parallel")),
        name="ffn",
    )(x, mod, n1, wi, wo, n2)


def _regroup_w_in(w_in, d, s5w):
    sizes = (HK, HK, HV, RANK, HV, HK, HK, HV, HEADS, HEADS, HV, s5w, 3 * d)
    offs, a = [], 0
    for s in sizes:
        offs.append(a)
        a += s
    col = lambda i: w_in[:, offs[i]:offs[i] + sizes[i]]
    zeros = lambda n: jnp.zeros((w_in.shape[0], n), w_in.dtype)
    small_a = jnp.concatenate([col(3), col(8), zeros(LANES - RANK - HEADS)], axis=1)
    small_b = jnp.concatenate([zeros(GATE_LANE), col(9), zeros(LANES - GATE_LANE - HEADS)], axis=1)
    return jnp.concatenate([col(0), col(1), col(2), col(4), col(5), col(6), col(7), col(10), col(11),
                            small_a, small_b, col(12)], axis=1).astype(BF16)


def _lane_pad(vec, lane0):
    return jnp.zeros((1, LANES), F32).at[0, lane0:lane0 + vec.shape[0]].set(vec)


def kernel(x, c, ada_w, ada_b, pre1_w, post1_w, pre2_w, post2_w, w_in, gla_a2, gla_a_b, gla_norm_w, ml_conv_w, ml_conv_b, ml_i_b, ml_f_b, s5_lam_re, s5_lam_im, s5_log_step, s5_b_re, s5_b_im, s5_c_re, s5_c_im, s5_d, s5_glu_w, s5_glu_b, proj_gla, proj_ml, proj_s5, branch_gate_b, w_out, ffn_w_in, ffn_w_out):
    b, t, d = x.shape
    depth = ada_w.shape[0]
    s5w = s5_d.shape[1]
    groups = s5w // S5_GROUP
    assert RANK + HEADS <= GATE_LANE + HEADS <= LANES and GATE_LANE == RANK
    assert t % 256 == 0 and d % LANES == 0
    tm = 256
    tt = 256
    seq_blocks = t // S5_BLOCK
    n_pow = max(1, (seq_blocks - 1).bit_length())

    mod_all = _ada_mod(c, ada_w, ada_b)
    row = lambda a: a.reshape(1, -1)
    for l in range(depth):
        mod = mod_all[l].reshape(b, 6, d)
        w1 = _regroup_w_in(w_in[l], d, s5w)
        zg, zm, zs5, zsm, zgate = _inproj(x, mod, row(pre1_w[l]), w1, tm=tm, s5w=s5w, ngate=3 * d)

        a2p = jnp.zeros((LANES, HK), F32).at[0:RANK].set(gla_a2[l])
        y_gla = _gla(zg, zsm, a2p, row(gla_a_b[l]), row(gla_norm_w[l]), tt=tt)
        y_ml = _mlstm(zm, zsm, ml_conv_w[l], row(ml_conv_b[l]), _lane_pad(ml_i_b[l], GATE_LANE),
                      _lane_pad(ml_f_b[l], GATE_LANE), tt=tt)

        tmat, wmat, vmat, pa, pb = _s5_prep(s5_lam_re[l], s5_lam_im[l], s5_log_step[l], s5_b_re[l], s5_b_im[l],
                                            s5_c_re[l], s5_c_im[l], n_pow=n_pow)
        u = zs5.reshape(b, seq_blocks, S5_BLOCK, groups, S5_GROUP).transpose(3, 0, 1, 2, 4)
        u = u.reshape(groups, b * seq_blocks, S5_BLOCK * S5_GROUP)
        dtile = jnp.tile(s5_d[l].reshape(groups, 1, S5_GROUP), (1, 1, S5_BLOCK))
        ys = _s5(u, tmat, wmat, vmat, pa, pb, dtile, seq_blocks=seq_blocks)
        ys = ys.reshape(groups, b, seq_blocks, S5_BLOCK, S5_GROUP).transpose(1, 2, 3, 0, 4).reshape(b, t, s5w)

        x = _merge(x, mod, y_gla, y_ml, ys, zgate, s5_glu_w[l].astype(BF16), row(s5_glu_b[l]),
                   proj_gla[l].astype(BF16), proj_ml[l].astype(BF16), proj_s5[l].astype(BF16),
                   row(branch_gate_b[l]), w_out[l].astype(BF16), row(post1_w[l]), tm=tm)
        x = _ffn(x, mod, row(pre2_w[l]), ffn_w_in[l].astype(BF16), ffn_w_out[l].astype(BF16),
                 row(post2_w[l]), tm=tm)
    return x
```

```python
import functools
import math

import jax
import jax.numpy as jnp
from jax import lax
from jax.experimental import pallas as pl
from jax.experimental.pallas import tpu as pltpu

HEADS = 4
DK = 64
DV = 128
RANK = 16
GATE_NORM = 16.0
CONV = 4
S5_GROUP = 16
S5_STATE = 64
CHUNK = 64
S5_BLOCK = 16
RMS_EPS = 1e-6
LANES = 128
GATE_LANE = 16
VMEM_LIMIT = 56 * 1024 * 1024

HK = HEADS * DK
HV = HEADS * DV
F32 = jnp.float32
BF16 = jnp.bfloat16
HIGHEST = lax.Precision.HIGHEST
NT = (((1,), (1,)), ((), ()))
TN = (((0,), (0,)), ((), ()))


def _params(sem):
    return pltpu.CompilerParams(dimension_semantics=sem, vmem_limit_bytes=VMEM_LIMIT)


def _const_spec(shape):
    nd = len(shape)
    return pl.BlockSpec(shape, lambda *_: (0,) * nd, pipeline_mode=pl.Buffered(1))


def _sigmoid(x):
    return 1.0 / (1.0 + jnp.exp(-x))


def _silu(x):
    return x * _sigmoid(x)


def _log_sigmoid(x):
    return jnp.minimum(x, 0.0) - jnp.log1p(jnp.exp(-jnp.abs(x)))


def _gelu_tanh(x):
    c = math.sqrt(2.0 / math.pi)
    return 0.5 * x * (1.0 + jnp.tanh(c * (x + 0.044715 * (x * x * x))))


def _rms(x, w):
    ms = jnp.mean(x * x, axis=-1, keepdims=True)
    return x * lax.rsqrt(ms + RMS_EPS) * w


def _ada_kernel(c_ref, w_ref, b_ref, o_ref):
    ca = _silu(c_ref[...]).astype(BF16)
    o_ref[0] = jnp.dot(ca, w_ref[0].astype(BF16), preferred_element_type=F32) + b_ref[0]


def _ada_mod(c, ada_w, ada_b):
    nl, d, n = ada_w.shape
    b = c.shape[0]
    tn = 1024
    return pl.pallas_call(
        _ada_kernel,
        grid=(nl, n // tn),
        in_specs=[pl.BlockSpec((b, d), lambda l, j: (0, 0)),
                  pl.BlockSpec((1, d, tn), lambda l, j: (l, 0, j)),
                  pl.BlockSpec((1, 1, tn), lambda l, j: (l, 0, j))],
        out_specs=pl.BlockSpec((1, b, tn), lambda l, j: (l, 0, j)),
        out_shape=jax.ShapeDtypeStruct((nl, b, n), F32),
        compiler_params=_params(("arbitrary", "arbitrary")),
        name="ada_mod",
    )(c, ada_w, ada_b.reshape(nl, 1, n))


N_GLA = 2 * HK + 2 * HV
N_ML = 2 * HK + 2 * HV
N_SMALL = 2 * LANES


def _inproj_kernel(x_ref, mod_ref, nw_ref, w_ref, zg_ref, zm_ref, zs_ref, zsm_ref, zgate_ref, *, offs):
    x = x_ref[0]
    mod = mod_ref[0]
    h = _rms(x, nw_ref[...]) * (1.0 + mod[1:2]) + mod[0:1]
    hb = h.astype(BF16)
    for ref, (a, b) in zip((zg_ref, zm_ref, zs_ref, zsm_ref, zgate_ref), offs):
        res = jnp.dot(hb, w_ref[:, a:b], preferred_element_type=F32).astype(ref.dtype)
        if ref is zs_ref:
            for s in range(ref.shape[1]):
                ref[0, s] = res[:, s * LANES:(s + 1) * LANES]
        else:
            ref[0] = res


def _inproj(x, mod, nw, w, *, tm, s5w, ngate):
    b, t, d = x.shape
    sizes = (N_GLA, N_ML, s5w, N_SMALL, ngate)
    offs, a = [], 0
    for s in sizes:
        offs.append((a, a + s))
        a += s
    dts = (BF16, BF16, F32, F32, BF16)
    tok = lambda n: pl.BlockSpec((1, tm, n), lambda i, j: (i, j, 0))
    slabs = s5w // LANES
    out_specs = [tok(s) for s in sizes]
    out_shape = [jax.ShapeDtypeStruct((b, t, s), dt) for s, dt in zip(sizes, dts)]
    out_specs[2] = pl.BlockSpec((1, slabs, tm, LANES), lambda i, j: (i, 0, j, 0))
    out_shape[2] = jax.ShapeDtypeStruct((b, slabs, t, LANES), F32)
    return pl.pallas_call(
        functools.partial(_inproj_kernel, offs=tuple(offs)),
        grid=(b, t // tm),
        in_specs=[tok(d),
                  pl.BlockSpec((1, 6, d), lambda i, j: (i, 0, 0)),
                  pl.BlockSpec((1, d), lambda i, j: (0, 0)),
                  _const_spec(w.shape)],
        out_specs=out_specs,
        out_shape=out_shape,
        compiler_params=_params(("parallel", "parallel")),
        name="in_proj",
    )(x, mod, nw, w)


def _block_diag_rows(a, width):
    lane_head = lax.broadcasted_iota(jnp.int32, a.shape, 1) // width
    zero = jnp.zeros_like(a)
    return jnp.concatenate([jnp.where(lane_head == h, a, zero) for h in range(HEADS)], axis=0)


def _head_mask(rows_per_head, cols_per_head):
    r = lax.broadcasted_iota(jnp.int32, (HEADS * rows_per_head, HEADS * cols_per_head), 0) // rows_per_head
    c = lax.broadcasted_iota(jnp.int32, (HEADS * rows_per_head, HEADS * cols_per_head), 1) // cols_per_head
    return r == c


def _lower_tri(n):
    r = lax.broadcasted_iota(jnp.int32, (n, n), 0)
    c = lax.broadcasted_iota(jnp.int32, (n, n), 1)
    return (c <= r).astype(F32)


def _gla_levels(q, k, g, c):
    n = q.shape[0]
    row = lax.broadcasted_iota(jnp.int32, q.shape, 0)
    zero = jnp.zeros_like(q)
    out = [(q, k, 1)]
    hs = 1
    while 2 * hs <= n:
        blk = 2 * hs
        pos = row % blk
        upper = pos >= hs
        if hs == 1:
            x = jnp.where(upper, g, 0.0)
        elif hs == 2:
            g_prev = pltpu.roll(g, 1, 0)
            g_next = pltpu.roll(g, n - 1, 0)
            x = jnp.where(pos == 2, g, jnp.where(pos == 3, g + g_prev, jnp.where(pos == 0, g_next, 0.0)))
        else:
            c3 = c.reshape(n // blk, blk, c.shape[1])
            anchor = jnp.broadcast_to(c3[:, hs - 1:hs, :], c3.shape).reshape(c.shape)
            x = -jnp.abs(c - anchor)
        e = jnp.exp(x)
        out.append((jnp.where(upper, q * e, zero), jnp.where(upper, zero, k * e), blk))
        hs = blk
    return out


def _gla_kernel(zg_ref, zs_ref, a2_ref, ab_ref, nw_ref, y_ref, st_ref, *, n_chunks):
    @pl.when(pl.program_id(1) == 0)
    def _():
        st_ref[...] = jnp.zeros_like(st_ref)

    L = CHUNK
    tri = _lower_tri(L)
    stmask = _head_mask(DV, DK)
    col = lax.broadcasted_iota(jnp.int32, (L, HK), 1) % DK
    rowi = lax.broadcasted_iota(jnp.int32, (L, HK), 0)
    for ci in range(n_chunks):
        r0 = ci * L
        zs = zs_ref[0, r0:r0 + L, 0:LANES]
        xg = jnp.dot(zs, a2_ref[...], precision=HIGHEST, preferred_element_type=F32) + ab_ref[...]
        g = _log_sigmoid(xg) * (1.0 / GATE_NORM)
        c = jnp.dot(tri, g, precision=HIGHEST, preferred_element_type=F32)
        q = zg_ref[0, r0:r0 + L, 0:HK].astype(F32) * (DK ** -0.5)
        k = zg_ref[0, r0:r0 + L, HK:2 * HK].astype(F32)
        v = zg_ref[0, r0:r0 + L, 2 * HK:2 * HK + HV]
        gr = zg_ref[0, r0:r0 + L, 2 * HK + HV:2 * HK + 2 * HV].astype(F32)

        scores = jnp.zeros((L, HK), F32)
        for ql, kl, blk in _gla_levels(q, k, g, c):
            s = lax.dot_general(ql.astype(BF16), _block_diag_rows(kl.astype(BF16), DK), NT,
                                preferred_element_type=F32)
            valid = (rowi == col) if blk == 1 else ((rowi // blk) == (col // blk))
            scores = scores + jnp.where(valid, s, 0.0)

        st = st_ref[...]
        o = jnp.dot(scores.astype(BF16), _block_diag_rows(v, DV), preferred_element_type=F32)
        o = o + lax.dot_general((q * jnp.exp(c)).astype(BF16), st.astype(BF16), NT,
                                preferred_element_type=F32)
        c_last = c[L - 1:L]
        kd = (k * jnp.exp(c_last - c)).astype(BF16)
        upd = lax.dot_general(v, kd, TN, preferred_element_type=F32)
        st_ref[...] = jnp.exp(c_last) * st + jnp.where(stmask, upd, 0.0)

        parts = []
        for h in range(HEADS):
            oh = o[:, h * DV:(h + 1) * DV]
            parts.append(oh * lax.rsqrt(jnp.mean(oh * oh, axis=-1, keepdims=True) + RMS_EPS))
        on = jnp.concatenate(parts, axis=1) * nw_ref[...]
        y_ref[0, r0:r0 + L, :] = (on * _silu(gr)).astype(y_ref.dtype)


def _gla(zg, zs, a2p, ab, nw, *, tt):
    b, t, _ = zg.shape
    return pl.pallas_call(
        functools.partial(_gla_kernel, n_chunks=tt // CHUNK),
        grid=(b, t // tt),
        in_specs=[pl.BlockSpec((1, tt, N_GLA), lambda i, j: (i, j, 0)),
                  pl.BlockSpec((1, tt, N_SMALL), lambda i, j: (i, j, 0)),
                  pl.BlockSpec((LANES, HK), lambda i, j: (0, 0)),
                  pl.BlockSpec((1, HK), lambda i, j: (0, 0)),
                  pl.BlockSpec((1, HV), lambda i, j: (0, 0))],
        out_specs=pl.BlockSpec((1, tt, HV), lambda i, j: (i, j, 0)),
        out_shape=jax.ShapeDtypeStruct((b, t, HV), BF16),
        scratch_shapes=[pltpu.VMEM((HV, HK), F32)],
        compiler_params=_params(("parallel", "arbitrary")),
        name="gla",
    )(zg, zs, a2p, ab, nw)


def _sel(n_out, per_head, lane0):
    r = lax.broadcasted_iota(jnp.int32, (LANES, n_out), 0)
    c = lax.broadcasted_iota(jnp.int32, (LANES, n_out), 1) // per_head
    return (r == c + lane0).astype(F32)


def _mlstm_kernel(zm_ref, zs_ref, cw_ref, cb_ref, ib_ref, fb_ref, y_ref,
                  mt_ref, nrm_ref, m_ref, tail_ref, *, n_chunks):
    tt = n_chunks * CHUNK

    @pl.when(pl.program_id(1) == 0)
    def _():
        mt_ref[...] = jnp.zeros_like(mt_ref)
        nrm_ref[...] = jnp.zeros_like(nrm_ref)
        m_ref[...] = jnp.zeros_like(m_ref)
        tail_ref[...] = jnp.zeros_like(tail_ref)

    qk_in = zm_ref[0, :, 0:2 * HK].astype(F32)
    ext = jnp.concatenate([tail_ref[...], qk_in], axis=0)
    acc = jnp.zeros((tt, 2 * HK), F32) + cb_ref[...]
    for j in range(CONV):
        off = 8 - (CONV - 1) + j
        acc = acc + cw_ref[j:j + 1, :] * ext[off:off + tt]
    tail_ref[...] = qk_in[tt - 8:tt]
    qk = _silu(acc)

    L = CHUNK
    tri = _lower_tri(L)
    lane = lax.broadcasted_iota(jnp.int32, (L, LANES), 1)
    gate_lanes = (lane >= GATE_LANE) & (lane < GATE_LANE + HEADS)
    rowg = lax.broadcasted_iota(jnp.int32, (L, LANES), 0)
    sel_v = _sel(HV, DV, GATE_LANE)
    sel_k = _sel(HK, DK, GATE_LANE)
    causal = (lax.broadcasted_iota(jnp.int32, (L, HK), 1) % DK) <= lax.broadcasted_iota(jnp.int32, (L, HK), 0)
    mtmask = _head_mask(DV, DK)
    ones_den = (lax.broadcasted_iota(jnp.int32, (HK, LANES), 0) // DK + GATE_LANE
                == lax.broadcasted_iota(jnp.int32, (HK, LANES), 1)).astype(BF16)
    den_row = lax.broadcasted_iota(jnp.int32, (LANES, HK), 0)
    den_head = lax.broadcasted_iota(jnp.int32, (LANES, HK), 1) // DK + GATE_LANE
    b_one_lane = lax.broadcasted_iota(jnp.int32, (L, LANES), 1)

    for ci in range(n_chunks):
        r0 = ci * L
        q = qk[r0:r0 + L, 0:HK]
        k = qk[r0:r0 + L, HK:2 * HK] * (DK ** -0.5)
        v = zm_ref[0, r0:r0 + L, 2 * HK:2 * HK + HV]
        og = zm_ref[0, r0:r0 + L, 2 * HK + HV:2 * HK + 2 * HV].astype(F32)
        ig = zs_ref[0, r0:r0 + L, 0:LANES] + ib_ref[...]
        lf = _log_sigmoid(zs_ref[0, r0:r0 + L, LANES:2 * LANES] + fb_ref[...])
        ig = jnp.where(gate_lanes, ig, 0.0)
        lf = jnp.where(gate_lanes, lf, 0.0)
        bcum = jnp.dot(tri, lf, precision=HIGHEST, preferred_element_type=F32)
        u = ig - bcum
        cmax = u
        sh = 1
        while sh < L:
            cmax = jnp.maximum(cmax, jnp.where(rowg >= sh, pltpu.roll(cmax, sh, 0), -jnp.inf))
            sh *= 2
        m_prev = m_ref[...]
        big_m = jnp.maximum(m_prev, cmax)
        m_t = bcum + big_m
        m_new = m_t[L - 1:L]
        b_last = bcum[L - 1:L]

        a_mat = jnp.where(gate_lanes, -big_m, jnp.where((lane >= GATE_LANE + 8) & (lane < GATE_LANE + 8 + HEADS), 1.0, 0.0))
        u8 = pltpu.roll(u, 8, 1)
        b_rows = []
        for h in range(HEADS):
            b_rows.append(jnp.where(b_one_lane == GATE_LANE + h, 1.0,
                                    jnp.where(b_one_lane == GATE_LANE + 8 + h, u8, 0.0)))
        b_mat = jnp.concatenate(b_rows, axis=0)
        dmat = lax.dot_general(a_mat, b_mat, NT, precision=HIGHEST, preferred_element_type=F32)
        pmat = jnp.where(causal, jnp.exp(jnp.where(causal, dmat, 0.0)), 0.0)

        qb = q.astype(BF16)
        qk_s = lax.dot_general(qb, _block_diag_rows(k.astype(BF16), DK), NT, preferred_element_type=F32)
        sm = (qk_s * pmat).astype(BF16)

        mt = mt_ref[...]
        nrm = nrm_ref[...]
        nrm_rows = jnp.where(den_row == den_head, jnp.broadcast_to(nrm, (LANES, HK)), 0.0)
        mem_ext = jnp.concatenate([mt, nrm_rows], axis=0).astype(BF16)
        inter = lax.dot_general(qb, mem_ext, NT, preferred_element_type=F32)
        v_ext = jnp.concatenate([_block_diag_rows(v, DV), ones_den], axis=1)
        intra = jnp.dot(sm, v_ext, preferred_element_type=F32)

        lw = jnp.where(gate_lanes, m_prev - big_m, 0.0)
        den = jnp.exp(lw) * inter[:, HV:] + intra[:, HV:]
        rden = 1.0 / jnp.maximum(jnp.abs(den), jnp.exp(-m_t))
        rden = jnp.where(gate_lanes, rden, 0.0)
        bc = jnp.dot(jnp.concatenate([lw, rden], axis=0), sel_v, precision=HIGHEST,
                     preferred_element_type=F32)
        hout = (jnp.exp(bc[0:L]) * inter[:, 0:HV] + intra[:, 0:HV]) * bc[L:2 * L]
        y_ref[0, r0:r0 + L, :] = (_sigmoid(og) * hout).astype(y_ref.dtype)

        wk = jnp.where(gate_lanes, b_last - m_new + u, 0.0)
        cd = jnp.where(gate_lanes[0:1], b_last + m_prev - m_new, 0.0)
        bk = jnp.dot(jnp.concatenate([wk, jnp.broadcast_to(cd, (8, LANES))], axis=0), sel_k,
                     precision=HIGHEST, preferred_element_type=F32)
        kw = k * jnp.exp(bk[0:L])
        decay = jnp.exp(bk[L:L + 1])
        upd = lax.dot_general(v, kw.astype(BF16), TN, preferred_element_type=F32)
        mt_ref[...] = decay * mt + jnp.where(mtmask, upd, 0.0)
        nrm_ref[...] = decay * nrm + jnp.sum(kw, axis=0, keepdims=True)
        m_ref[...] = m_new


def _mlstm(zm, zs, cw, cb, ibp, fbp, *, tt):
    b, t, _ = zm.shape
    return pl.pallas_call(
        functools.partial(_mlstm_kernel, n_chunks=tt // CHUNK),
        grid=(b, t // tt),
        in_specs=[pl.BlockSpec((1, tt, N_ML), lambda i, j: (i, j, 0)),
                  pl.BlockSpec((1, tt, N_SMALL), lambda i, j: (i, j, 0)),
                  pl.BlockSpec((CONV, 2 * HK), lambda i, j: (0, 0)),
                  pl.BlockSpec((1, 2 * HK), lambda i, j: (0, 0)),
                  pl.BlockSpec((1, LANES), lambda i, j: (0, 0)),
                  pl.BlockSpec((1, LANES), lambda i, j: (0, 0))],
        out_specs=pl.BlockSpec((1, tt, HV), lambda i, j: (i, j, 0)),
        out_shape=jax.ShapeDtypeStruct((b, t, HV), BF16),
        scratch_shapes=[pltpu.VMEM((HV, HK), F32), pltpu.VMEM((1, HK), F32),
                        pltpu.VMEM((1, LANES), F32), pltpu.VMEM((8, 2 * HK), F32)],
        compiler_params=_params(("parallel", "arbitrary")),
        name="mlstm",
    )(zm, zs, cw, cb, ibp, fbp)


def _s5_prep_kernel(lr_ref, li_ref, lrc_ref, lic_ref, dt_ref, btr_ref, bti_ref, ctr_ref, cti_ref, d_ref,
                    t_ref, w_ref, v_ref, pa_ref, pb_ref, *, n_pow):
    P = S5_STATE
    nb = S5_BLOCK
    dt = dt_ref[0]
    lr = jnp.minimum(lr_ref[0], -1e-4)
    li = li_ref[0]
    lrc = jnp.minimum(lrc_ref[0], -1e-4)
    lic = lic_ref[0]
    dtr = jnp.exp(dt)

    def power(lr_, li_, n):
        mag = jnp.exp(lr_ * dtr * n)
        ang = li_ * dtr * n
        return mag * jnp.cos(ang), mag * jnp.sin(ang)

    ar1, ai1 = power(lr, li, 1.0)
    inv = 1.0 / (lr * lr + li * li)
    zr = ((ar1 - 1.0) * lr + ai1 * li) * inv
    zi = (ai1 * lr - (ar1 - 1.0) * li) * inv
    bbr = zr * btr_ref[0] - zi * bti_ref[0]
    bbi = zr * bti_ref[0] + zi * btr_ref[0]

    tl = (lax.broadcasted_iota(jnp.int32, (P, nb * S5_GROUP), 1) // S5_GROUP).astype(F32)
    ctr, cti = ctr_ref[0], cti_ref[0]

    def out_map(shift):
        ar, ai = power(lrc, lic, tl + shift)
        return jnp.concatenate([ctr * ar - cti * ai, -(ctr * ai + cti * ar)], axis=0)

    v_ref[0] = out_map(1.0).T.astype(v_ref.dtype)
    base = jnp.dot(jnp.concatenate([bbr, bbi], axis=1), out_map(0.0), precision=HIGHEST,
                   preferred_element_type=F32)
    lane = lax.broadcasted_iota(jnp.int32, base.shape, 1)
    base = base + jnp.where(lane == lax.broadcasted_iota(jnp.int32, base.shape, 0), d_ref[0], 0.0)
    rows = [base]
    for s in range(1, nb):
        rows.append(jnp.where(lane >= s * S5_GROUP, pltpu.roll(base, s * S5_GROUP, 1), 0.0))
    t_ref[0] = jnp.concatenate(rows, axis=0).T.astype(t_ref.dtype)

    se = (nb - 1 - lax.broadcasted_iota(jnp.int32, (nb * S5_GROUP, P), 0) // S5_GROUP).astype(F32)
    ar, ai = power(lr, li, se)
    bbr_t = jnp.concatenate([bbr] * nb, axis=0)
    bbi_t = jnp.concatenate([bbi] * nb, axis=0)
    wt = jnp.concatenate([ar * bbr_t - ai * bbi_t, ar * bbi_t + ai * bbr_t], axis=1).T
    w_ref[0] = jnp.concatenate([wt, wt[P:2 * P], wt[0:P]], axis=0).astype(w_ref.dtype)

    pr, pi = power(lr, li, float(nb))
    pa, pb = [], []
    for _ in range(n_pow):
        pa.append(jnp.concatenate([pr, pr], axis=1))
        pb.append(jnp.concatenate([-pi, pi], axis=1))
        pr, pi = pr * pr - pi * pi, 2.0 * pr * pi
    pa_ref[0] = jnp.concatenate(pa, axis=0)
    pb_ref[0] = jnp.concatenate(pb, axis=0)


def _s5_prep(lam_re, lam_im, log_step, b_re, b_im, c_re, c_im, d_skip, *, n_pow):
    g, p = lam_re.shape
    nbg = S5_BLOCK * S5_GROUP
    bt = lambda a: jnp.swapaxes(a, 1, 2)
    ct = lambda a: jnp.tile(jnp.swapaxes(a, 1, 2), (1, 1, S5_BLOCK))
    row = lambda a: a.reshape(g, 1, p)
    colv = lambda a: a.reshape(g, p, 1)
    dtile = jnp.tile(d_skip.reshape(g, 1, S5_GROUP), (1, 1, S5_BLOCK))
    spec = lambda *s: pl.BlockSpec((1,) + s, lambda i: (i,) + (0,) * len(s))
    return pl.pallas_call(
        functools.partial(_s5_prep_kernel, n_pow=n_pow),
        grid=(g,),
        in_specs=[spec(1, p), spec(1, p), spec(p, 1), spec(p, 1), spec(1, 1),
                  spec(S5_GROUP, p), spec(S5_GROUP, p), spec(p, nbg), spec(p, nbg), spec(1, nbg)],
        out_specs=[spec(nbg, nbg), spec(4 * p, nbg), spec(nbg, 2 * p), spec(n_pow, 2 * p), spec(n_pow, 2 * p)],
        out_shape=[jax.ShapeDtypeStruct((g, nbg, nbg), BF16), jax.ShapeDtypeStruct((g, 4 * p, nbg), BF16),
                   jax.ShapeDtypeStruct((g, nbg, 2 * p), BF16), jax.ShapeDtypeStruct((g, n_pow, 2 * p), F32),
                   jax.ShapeDtypeStruct((g, n_pow, 2 * p), F32)],
        compiler_params=_params(("parallel",)),
        name="s5_prep",
    )(row(lam_re), row(lam_im), colv(lam_re), colv(lam_im), log_step.reshape(g, 1, 1),
      bt(b_re), bt(b_im), ct(c_re), ct(c_im), dtile)


S5_SCAN_GROUPS = 4


def _s5_kernel(u_ref, t_ref, w_ref, v_ref, pa_ref, pb_ref, y_ref,
               ut_ref, yt_ref, zt_ref, zw_ref, xin_ref, xc_ref, *, groups, n_pow):
    nb = S5_BLOCK
    p2 = 2 * S5_STATE
    slabs, tt = u_ref.shape[1], u_ref.shape[2]
    nblk = tt // nb

    @pl.when(pl.program_id(1) == 0)
    def _():
        xc_ref[...] = jnp.zeros_like(xc_ref)

    for tl in range(nb):
        for s in range(slabs):
            ut_ref[tl, s * LANES:(s + 1) * LANES, :] = (
                u_ref[0, s, pl.ds(tl, nblk, stride=nb), :].T.astype(ut_ref.dtype))

    def group_input(g):
        c0 = pl.multiple_of(g * S5_GROUP, S5_GROUP)
        return c0, ut_ref[:, pl.ds(c0, S5_GROUP), :].reshape(nb * S5_GROUP, nblk)

    def state_in(g, carry):
        _, ut = group_input(g)
        r0 = pl.multiple_of(g * p2, p2)
        z2 = jnp.dot(w_ref[g], ut, preferred_element_type=F32)
        zt_ref[pl.ds(r0, p2), :] = z2[0:p2]
        zw_ref[pl.ds(r0, p2), :] = z2[p2:2 * p2]
        return carry

    lax.fori_loop(0, groups, state_in, 0, unroll=4)

    width = S5_SCAN_GROUPS * p2
    row = lax.broadcasted_iota(jnp.int32, (nblk, width), 0)

    def scan(ci, carry):
        l0 = pl.multiple_of(ci * width, width)
        x = zt_ref[pl.ds(l0, width), :].T
        xw = zw_ref[pl.ds(l0, width), :].T
        pa = pa_ref[:, pl.ds(l0, width)]
        pb = pb_ref[:, pl.ds(l0, width)]
        xc = xc_ref[0:1, pl.ds(l0, width)]
        xcw = xc_ref[1:2, pl.ds(l0, width)]
        first = row == 0
        x = x + jnp.where(first, xc * pa[0:1] + xcw * pb[0:1], 0.0)
        xw = xw + jnp.where(first, xcw * pa[0:1] - xc * pb[0:1], 0.0)
        for k in range(n_pow):
            sh = 1 << k
            if sh >= nblk:
                break
            xs, xws, m = pltpu.roll(x, sh, 0), pltpu.roll(xw, sh, 0), row >= sh
            x, xw = (x + jnp.where(m, xs * pa[k:k + 1] + xws * pb[k:k + 1], 0.0),
                     xw + jnp.where(m, xws * pa[k:k + 1] - xs * pb[k:k + 1], 0.0))
        x_in = jnp.where(row >= 1, pltpu.roll(x, 1, 0), xc)
        xc_ref[0:1, pl.ds(l0, width)] = x[nblk - 1:nblk]
        xc_ref[1:2, pl.ds(l0, width)] = xw[nblk - 1:nblk]
        xin_ref[pl.ds(l0, width), :] = x_in.T.astype(xin_ref.dtype)
        return carry

    lax.fori_loop(0, groups // S5_SCAN_GROUPS, scan, 0)

    def block_out(g, carry):
        c0, ut = group_input(g)
        r0 = pl.multiple_of(g * p2, p2)
        yt = jnp.dot(t_ref[g], ut, preferred_element_type=F32)
        yt = yt + jnp.dot(v_ref[g], xin_ref[pl.ds(r0, p2), :], preferred_element_type=F32)
        yt_ref[:, pl.ds(c0, S5_GROUP), :] = yt.reshape(nb, S5_GROUP, nblk)
        return carry

    lax.fori_loop(0, groups, block_out, 0, unroll=4)
    for tl in range(nb):
        for s in range(slabs):
            y_ref[0, s, pl.ds(tl, nblk, stride=nb), :] = yt_ref[tl, s * LANES:(s + 1) * LANES, :].T


def _s5(u, tmat, wmat, vmat, pa, pb, *, tt):
    b, slabs, t, _ = u.shape
    width = slabs * LANES
    groups = width // S5_GROUP
    assert groups % S5_SCAN_GROUPS == 0
    n_pow = pa.shape[0]
    nblk = tt // S5_BLOCK
    states = groups * 2 * S5_STATE
    tok = pl.BlockSpec((1, slabs, tt, LANES), lambda i, j: (i, 0, j, 0))
    return pl.pallas_call(
        functools.partial(_s5_kernel, groups=groups, n_pow=n_pow),
        grid=(b, t // tt),
        in_specs=[tok, _const_spec(tmat.shape), _const_spec(wmat.shape), _const_spec(vmat.shape),
                  _const_spec(pa.shape), _const_spec(pb.shape)],
        out_specs=tok,
        out_shape=jax.ShapeDtypeStruct((b, slabs, t, LANES), F32),
        scratch_shapes=[pltpu.VMEM((S5_BLOCK, width, nblk), BF16), pltpu.VMEM((S5_BLOCK, width, nblk), F32),
                        pltpu.VMEM((states, nblk), F32), pltpu.VMEM((states, nblk), F32),
                        pltpu.VMEM((states, nblk), BF16), pltpu.VMEM((8, states), F32)],
        compiler_params=_params(("parallel", "arbitrary")),
        name="s5",
    )(u, tmat, wmat, vmat, pa, pb)


def _merge_kernel(x_ref, mod_ref, yg_ref, ym_ref, ys_ref, zgate_ref, gluw_ref, glub_ref, pg_ref, pm_ref,
                  ps_ref, gb_ref, wo_ref, nw_ref, o_ref, *, d):
    mod = mod_ref[0]
    s = _gelu_tanh(jnp.concatenate([ys_ref[0, k] for k in range(ys_ref.shape[1])], axis=1))
    s = s * _sigmoid(jnp.dot(s.astype(BF16), gluw_ref[...], preferred_element_type=F32) + glub_ref[...])
    merged = None
    for idx, (y, p) in enumerate(((yg_ref[0], pg_ref), (ym_ref[0], pm_ref), (s.astype(BF16), ps_ref))):
        gate = _sigmoid(zgate_ref[0, :, idx * d:(idx + 1) * d].astype(F32) + gb_ref[:, idx * d:(idx + 1) * d])
        term = gate * jnp.dot(y, p[...], preferred_element_type=F32)
        merged = term if merged is None else merged + term
    y = jnp.dot(merged.astype(BF16), wo_ref[...], preferred_element_type=F32)
    o_ref[0] = x_ref[0] + (1.0 + mod[2:3]) * _rms(y, nw_ref[...])


def _merge(x, mod, yg, ym, ys, zgate, gluw, glub, pg, pm, ps, gb, wo, nw, *, tm):
    b, t, d = x.shape
    tok = lambda n: pl.BlockSpec((1, tm, n), lambda i, j: (i, j, 0))
    small = lambda a: pl.BlockSpec(a.shape, lambda i, j: (0,) * a.ndim)
    return pl.pallas_call(
        functools.partial(_merge_kernel, d=d),
        grid=(b, t // tm),
        in_specs=[tok(d), pl.BlockSpec((1, 6, d), lambda i, j: (i, 0, 0)), tok(HV), tok(HV),
                  pl.BlockSpec((1, ys.shape[1], tm, LANES), lambda i, j: (i, 0, j, 0)), tok(zgate.shape[2]), _const_spec(gluw.shape), small(glub), _const_spec(pg.shape),
                  _const_spec(pm.shape), _const_spec(ps.shape), small(gb), _const_spec(wo.shape), small(nw)],
        out_specs=tok(d),
        out_shape=jax.ShapeDtypeStruct((b, t, d), F32),
        compiler_params=_params(("parallel", "parallel")),
        name="merge_out",
    )(x, mod, yg, ym, ys, zgate, gluw, glub, pg, pm, ps, gb, wo, nw)


def _ffn_kernel(x_ref, mod_ref, n1_ref, wi_ref, wo_ref, n2_ref, o_ref, *, hidden):
    x = x_ref[0]
    mod = mod_ref[0]
    h = (_rms(x, n1_ref[...]) * (1.0 + mod[4:5]) + mod[3:4]).astype(BF16)
    gate = jnp.dot(h, wi_ref[:, 0:hidden], preferred_element_type=F32)
    up = jnp.dot(h, wi_ref[:, hidden:2 * hidden], preferred_element_type=F32)
    act = (_silu(gate) * up).astype(BF16)
    y = jnp.dot(act, wo_ref[...], preferred_element_type=F32)
    o_ref[0] = x + (1.0 + mod[5:6]) * _rms(y, n2_ref[...])


def _ffn(x, mod, n1, wi, wo, n2, *, tm):
    b, t, d = x.shape
    hidden = wo.shape[0]
    tok = pl.BlockSpec((1, tm, d), lambda i, j: (i, j, 0))
    small = lambda a: pl.BlockSpec(a.shape, lambda i, j: (0,) * a.ndim)
    return pl.pallas_call(
        functools.partial(_ffn_kernel, hidden=hidden),
        grid=(b, t // tm),
        in_specs=[tok, pl.BlockSpec((1, 6, d), lambda i, j: (i, 0, 0)), small(n1), _const_spec(wi.shape),
                  _const_spec(wo.shape), small(n2)],
        out_specs=tok,
        out_shape=jax.ShapeDtypeStruct((b, t, d), F32),
        compiler_params=_params(("parallel", "parallel")),
        name="ffn",
    )(x, mod, n1, wi, wo, n2)


def _regroup_w_in(w_in, d, s5w):
    sizes = (HK, HK, HV, RANK, HV, HK, HK, HV, HEADS, HEADS, HV, s5w, 3 * d)
    offs, a = [], 0
    for s in sizes:
        offs.append(a)
        a += s
    col = lambda i: w_in[:, offs[i]:offs[i] + sizes[i]]
    zeros = lambda n: jnp.zeros((w_in.shape[0], n), w_in.dtype)
    small_a = jnp.concatenate([col(3), col(8), zeros(LANES - RANK - HEADS)], axis=1)
    small_b = jnp.concatenate([zeros(GATE_LANE), col(9), zeros(LANES - GATE_LANE - HEADS)], axis=1)
    return jnp.concatenate([col(0), col(1), col(2), col(4), col(5), col(6), col(7), col(10), col(11),
                            small_a, small_b, col(12)], axis=1).astype(BF16)


def _lane_pad(vec, lane0):
    return jnp.zeros((1, LANES), F32).at[0, lane0:lane0 + vec.shape[0]].set(vec)


def kernel(x, c, ada_w, ada_b, pre1_w, post1_w, pre2_w, post2_w, w_in, gla_a2, gla_a_b, gla_norm_w, ml_conv_w, ml_conv_b, ml_i_b, ml_f_b, s5_lam_re, s5_lam_im, s5_log_step, s5_b_re, s5_b_im, s5_c_re, s5_c_im, s5_d, s5_glu_w, s5_glu_b, proj_gla, proj_ml, proj_s5, branch_gate_b, w_out, ffn_w_in, ffn_w_out):
    b, t, d = x.shape
    depth = ada_w.shape[0]
    s5w = s5_d.shape[1]
    assert RANK + HEADS <= GATE_LANE + HEADS <= LANES and GATE_LANE == RANK
    assert t % 256 == 0 and d % LANES == 0
    tm = 512 if t % 512 == 0 else 256
    tt = 256
    tt_s5 = 2048 if t % 2048 == 0 else t
    n_pow = max(1, (tt_s5 // S5_BLOCK - 1).bit_length())

    mod_all = _ada_mod(c, ada_w, ada_b)
    row = lambda a: a.reshape(1, -1)
    for l in range(depth):
        mod = mod_all[l].reshape(b, 6, d)
        w1 = _regroup_w_in(w_in[l], d, s5w)
        zg, zm, zs5, zsm, zgate = _inproj(x, mod, row(pre1_w[l]), w1, tm=tm, s5w=s5w, ngate=3 * d)

        a2p = jnp.zeros((LANES, HK), F32).at[0:RANK].set(gla_a2[l])
        y_gla = _gla(zg, zsm, a2p, row(gla_a_b[l]), row(gla_norm_w[l]), tt=tt)
        y_ml = _mlstm(zm, zsm, ml_conv_w[l], row(ml_conv_b[l]), _lane_pad(ml_i_b[l], GATE_LANE),
                      _lane_pad(ml_f_b[l], GATE_LANE), tt=tt)

        tmat, wmat, vmat, pa, pb = _s5_prep(s5_lam_re[l], s5_lam_im[l], s5_log_step[l], s5_b_re[l], s5_b_im[l],
                                            s5_c_re[l], s5_c_im[l], s5_d[l], n_pow=n_pow)
        lanes_of = lambda a: a.transpose(1, 0, 2).reshape(n_pow, -1)
        ys = _s5(zs5, tmat, wmat, vmat, lanes_of(pa), lanes_of(pb), tt=tt_s5)

        x = _merge(x, mod, y_gla, y_ml, ys, zgate, s5_glu_w[l].astype(BF16), row(s5_glu_b[l]),
                   proj_gla[l].astype(BF16), proj_ml[l].astype(BF16), proj_s5[l].astype(BF16),
                   row(branch_gate_b[l]), w_out[l].astype(BF16), row(post1_w[l]), tm=tm)
        x = _ffn(x, mod, row(pre2_w[l]), ffn_w_in[l].astype(BF16), ffn_w_out[l].astype(BF16),
                 row(post2_w[l]), tm=tm)
    return x
```

```python
import functools
import math

import jax
import jax.numpy as jnp
from jax import lax
from jax.experimental import pallas as pl
from jax.experimental.pallas import tpu as pltpu

HEADS = 4
DK = 64
DV = 128
RANK = 16
GATE_NORM = 16.0
CONV = 4
S5_GROUP = 16
S5_STATE = 64
CHUNK = 64
S5_BLOCK = 16
RMS_EPS = 1e-6
LANES = 128
GATE_LANE = 16
VMEM_LIMIT = 56 * 1024 * 1024

HK = HEADS * DK
HV = HEADS * DV
F32 = jnp.float32
BF16 = jnp.bfloat16
HIGHEST = lax.Precision.HIGHEST
NT = (((1,), (1,)), ((), ()))
TN = (((0,), (0,)), ((), ()))


def _params(sem):
    return pltpu.CompilerParams(dimension_semantics=sem, vmem_limit_bytes=VMEM_LIMIT)


def _const_spec(shape):
    nd = len(shape)
    return pl.BlockSpec(shape, lambda *_: (0,) * nd, pipeline_mode=pl.Buffered(1))


def _sigmoid(x):
    return 0.5 * jnp.tanh(0.5 * x) + 0.5


def _silu(x):
    return x * _sigmoid(x)


def _log_sigmoid(x):
    return jnp.minimum(x, 0.0) - jnp.log(1.0 + jnp.exp(-jnp.abs(x)))


def _gelu_tanh(x):
    c = math.sqrt(2.0 / math.pi)
    return 0.5 * x * (1.0 + jnp.tanh(c * (x + 0.044715 * (x * x * x))))


def _rms(x, w):
    ms = jnp.mean(x * x, axis=-1, keepdims=True)
    return x * lax.rsqrt(ms + RMS_EPS) * w


def _ada_kernel(c_ref, w_ref, b_ref, o_ref):
    ca = _silu(c_ref[...]).astype(BF16)
    o_ref[0] = jnp.dot(ca, w_ref[0].astype(BF16), preferred_element_type=F32) + b_ref[0]


def _ada_mod(c, ada_w, ada_b):
    nl, d, n = ada_w.shape
    b = c.shape[0]
    tn = 1024
    return pl.pallas_call(
        _ada_kernel,
        grid=(nl, n // tn),
        in_specs=[pl.BlockSpec((b, d), lambda l, j: (0, 0)),
                  pl.BlockSpec((1, d, tn), lambda l, j: (l, 0, j)),
                  pl.BlockSpec((1, 1, tn), lambda l, j: (l, 0, j))],
        out_specs=pl.BlockSpec((1, b, tn), lambda l, j: (l, 0, j)),
        out_shape=jax.ShapeDtypeStruct((nl, b, n), F32),
        compiler_params=_params(("arbitrary", "arbitrary")),
        name="ada_mod",
    )(c, ada_w, ada_b.reshape(nl, 1, n))


N_GLA = 2 * HK + 2 * HV
N_ML = 2 * HK + 2 * HV
N_SMALL = 2 * LANES


def _inproj_kernel(x_ref, mod_ref, nw_ref, w_ref, zg_ref, zm_ref, zs_ref, zsm_ref, zgate_ref, *, offs):
    x = x_ref[0]
    mod = mod_ref[0]
    h = _rms(x, nw_ref[...]) * (1.0 + mod[1:2]) + mod[0:1]
    hb = h.astype(BF16)
    for ref, (a, b) in zip((zg_ref, zm_ref, zs_ref, zsm_ref, zgate_ref), offs):
        res = jnp.dot(hb, w_ref[:, a:b], preferred_element_type=F32).astype(ref.dtype)
        if ref is zs_ref:
            for s in range(ref.shape[1]):
                ref[0, s] = res[:, s * LANES:(s + 1) * LANES]
        else:
            ref[0] = res


def _inproj(x, mod, nw, w, *, tm, s5w, ngate):
    b, t, d = x.shape
    sizes = (N_GLA, N_ML, s5w, N_SMALL, ngate)
    offs, a = [], 0
    for s in sizes:
        offs.append((a, a + s))
        a += s
    dts = (BF16, BF16, F32, F32, BF16)
    tok = lambda n: pl.BlockSpec((1, tm, n), lambda i, j: (i, j, 0))
    slabs = s5w // LANES
    out_specs = [tok(s) for s in sizes]
    out_shape = [jax.ShapeDtypeStruct((b, t, s), dt) for s, dt in zip(sizes, dts)]
    out_specs[2] = pl.BlockSpec((1, slabs, tm, LANES), lambda i, j: (i, 0, j, 0))
    out_shape[2] = jax.ShapeDtypeStruct((b, slabs, t, LANES), F32)
    return pl.pallas_call(
        functools.partial(_inproj_kernel, offs=tuple(offs)),
        grid=(b, t // tm),
        in_specs=[tok(d),
                  pl.BlockSpec((1, 6, d), lambda i, j: (i, 0, 0)),
                  pl.BlockSpec((1, d), lambda i, j: (0, 0)),
                  _const_spec(w.shape)],
        out_specs=out_specs,
        out_shape=out_shape,
        compiler_params=_params(("parallel", "parallel")),
        name="in_proj",
    )(x, mod, nw, w)


def _block_diag_rows(a, width):
    lane_head = lax.broadcasted_iota(jnp.int32, a.shape, 1) // width
    zero = jnp.zeros_like(a)
    return jnp.concatenate([jnp.where(lane_head == h, a, zero) for h in range(HEADS)], axis=0)


def _head_mask(rows_per_head, cols_per_head):
    r = lax.broadcasted_iota(jnp.int32, (HEADS * rows_per_head, HEADS * cols_per_head), 0) // rows_per_head
    c = lax.broadcasted_iota(jnp.int32, (HEADS * rows_per_head, HEADS * cols_per_head), 1) // cols_per_head
    return r == c


def _scan_rows(x, period, op, fill):
    pos = lax.broadcasted_iota(jnp.int32, x.shape, 0) % period
    sh = 1
    while sh < period:
        x = op(x, jnp.where(pos >= sh, pltpu.roll(x, sh, 0), fill))
        sh *= 2
    return x


def _bcast_heads(tile, per_head):
    rows = tile.shape[0]
    full = [jnp.broadcast_to(tile[:, GATE_LANE + h:GATE_LANE + h + 1], (rows, LANES)) for h in range(HEADS)]
    if per_head == LANES:
        return jnp.concatenate(full, axis=1)
    assert 2 * per_head == LANES and HEADS % 2 == 0
    low = lax.broadcasted_iota(jnp.int32, (rows, LANES), 1) < per_head
    return jnp.concatenate([jnp.where(low, full[h], full[h + 1]) for h in range(0, HEADS, 2)], axis=1)


def _gla_levels(q, k, g, c):
    n = q.shape[0]
    row = lax.broadcasted_iota(jnp.int32, q.shape, 0)
    out = []
    hs = 1
    while 2 * hs <= n:
        blk = 2 * hs
        pos = row % blk
        upper = pos >= hs
        if hs == 1:
            x = jnp.where(upper, g, 0.0)
        elif hs == 2:
            g_prev = pltpu.roll(g, 1, 0)
            g_next = pltpu.roll(g, n - 1, 0)
            x = jnp.where(pos == 2, g, jnp.where(pos == 3, g + g_prev, jnp.where(pos == 0, g_next, 0.0)))
        else:
            c3 = c.reshape(n // blk, blk, c.shape[1])
            anchor = jnp.broadcast_to(c3[:, hs - 1:hs, :], c3.shape).reshape(c.shape)
            x = -jnp.abs(c - anchor)
        out.append((jnp.where(upper, q, k) * jnp.exp(x), hs))
        hs = blk
    return out


def _pair_block_diag(a):
    low = lax.broadcasted_iota(jnp.int32, a.shape, 1) < a.shape[1] // 2
    zero = jnp.zeros_like(a)
    return jnp.concatenate([jnp.where(low, a, zero), jnp.where(low, zero, a)], axis=0)


def _update_state(st_ref, decay_row, upd):
    for h in range(HEADS):
        rs = slice(h * DV, (h + 1) * DV)
        c0 = (h * DK) // LANES * LANES
        cs = slice(c0, c0 + LANES)
        own = (lax.broadcasted_iota(jnp.int32, (DV, LANES), 1) + c0) // DK == h
        st_ref[rs, cs] = decay_row[:, cs] * st_ref[rs, cs] + jnp.where(own, upd[rs, cs], 0.0)


def _gla_chunk(q, k, v, gr, g, c, nw, st_ref):
    L = q.shape[0]
    col = lax.broadcasted_iota(jnp.int32, (L, LANES), 1) % DK
    rowi = lax.broadcasted_iota(jnp.int32, (L, LANES), 0)
    qb, kb = q.astype(BF16), k.astype(BF16)
    levels = [(t.astype(BF16), hs) for t, hs in _gla_levels(q, k, g, c)]
    pairs = []
    for p in range(HK // LANES):
        ls = slice(p * LANES, (p + 1) * LANES)
        s = lax.dot_general(qb[:, ls], _pair_block_diag(kb[:, ls]), NT, preferred_element_type=F32)
        acc = jnp.where(rowi == col, s, 0.0)
        for tb, hs in levels:
            s = lax.dot_general(tb[:, ls], _pair_block_diag(tb[:, ls]), NT, preferred_element_type=F32)
            blk = 2 * hs
            valid = ((rowi // blk) == (col // blk)) & (rowi % blk >= hs) & (col % blk < hs)
            acc = acc + jnp.where(valid, s, 0.0)
        pairs.append(acc)
    scores = jnp.concatenate(pairs, axis=1)

    st = st_ref[...]
    o = jnp.dot(scores.astype(BF16), _block_diag_rows(v, DV), preferred_element_type=F32)
    o = o + lax.dot_general((q * jnp.exp(c)).astype(BF16), st.astype(BF16), NT,
                            preferred_element_type=F32)
    c_last = c[L - 1:L]
    kd = (k * jnp.exp(c_last - c)).astype(BF16)
    upd = lax.dot_general(v, kd, TN, preferred_element_type=F32)
    _update_state(st_ref, jnp.exp(c_last), upd)

    parts = []
    for h in range(HEADS):
        oh = o[:, h * DV:(h + 1) * DV]
        parts.append(oh * lax.rsqrt(jnp.mean(oh * oh, axis=-1, keepdims=True) + RMS_EPS))
    return jnp.concatenate(parts, axis=1) * nw * _silu(gr)


def _mlstm_chunk(q, k, v, og, u, bcum, cmax, mt_ref, nrm_ref, m_ref):
    L = q.shape[0]
    lane = lax.broadcasted_iota(jnp.int32, (L, LANES), 1)
    gate_lanes = (lane >= GATE_LANE) & (lane < GATE_LANE + HEADS)
    m_prev = m_ref[...]
    big_m = jnp.maximum(m_prev, cmax)
    m_t = bcum + big_m
    m_new = m_t[L - 1:L]
    b_last = bcum[L - 1:L]

    ut = jnp.concatenate([u, u], axis=0).T
    low = lax.broadcasted_iota(jnp.int32, (1, 2 * L), 1) < L
    urow = jnp.concatenate([jnp.where(low, ut[GATE_LANE + h:GATE_LANE + h + 1], ut[GATE_LANE + h + 1:GATE_LANE + h + 2])
                            for h in range(0, HEADS, 2)], axis=1)
    causal = (lax.broadcasted_iota(jnp.int32, (L, HK), 1) % DK) <= lax.broadcasted_iota(jnp.int32, (L, HK), 0)
    dmat = _bcast_heads(-big_m, DK) + urow
    pmat = jnp.where(causal, jnp.exp(jnp.where(causal, dmat, 0.0)), 0.0)

    qb = q.astype(BF16)
    qk_s = lax.dot_general(qb, _block_diag_rows(k.astype(BF16), DK), NT, preferred_element_type=F32)
    sm = (qk_s * pmat).astype(BF16)

    mt = mt_ref[...]
    nrm = nrm_ref[...]
    den_row = lax.broadcasted_iota(jnp.int32, (LANES, HK), 0)
    den_head = lax.broadcasted_iota(jnp.int32, (LANES, HK), 1) // DK + GATE_LANE
    nrm_rows = jnp.where(den_row == den_head, jnp.broadcast_to(nrm, (LANES, HK)), 0.0)
    mem_ext = jnp.concatenate([mt, nrm_rows], axis=0).astype(BF16)
    inter = lax.dot_general(qb, mem_ext, NT, preferred_element_type=F32)
    ones_den = (lax.broadcasted_iota(jnp.int32, (HK, LANES), 0) // DK + GATE_LANE
                == lax.broadcasted_iota(jnp.int32, (HK, LANES), 1)).astype(BF16)
    v_ext = jnp.concatenate([_block_diag_rows(v, DV), ones_den], axis=1)
    intra = jnp.dot(sm, v_ext, preferred_element_type=F32)

    lw = jnp.where(gate_lanes, m_prev - big_m, 0.0)
    den = jnp.exp(lw) * inter[:, HV:] + intra[:, HV:]
    rden = 1.0 / jnp.maximum(jnp.abs(den), jnp.exp(-m_t))
    hout = (jnp.exp(_bcast_heads(lw, DV)) * inter[:, 0:HV] + intra[:, 0:HV]) * _bcast_heads(rden, DV)

    wk = jnp.where(gate_lanes, b_last - m_new + u, 0.0)
    cd = jnp.where(gate_lanes[0:1], b_last + m_prev - m_new, 0.0)
    kw = k * jnp.exp(_bcast_heads(wk, DK))
    decay = jnp.exp(_bcast_heads(cd, DK))
    upd = lax.dot_general(v, kw.astype(BF16), TN, preferred_element_type=F32)
    _update_state(mt_ref, decay, upd)
    nrm_ref[...] = decay * nrm + jnp.sum(kw, axis=0, keepdims=True)
    m_ref[...] = m_new
    return _sigmoid(og) * hout


def _mixer_kernel(zg_ref, zm_ref, zs_ref, a2_ref, ab_ref, gnw_ref, cw_ref, cb_ref, ib_ref, fb_ref,
                  yg_ref, ym_ref, st_ref, mt_ref, nrm_ref, m_ref, ext_ref, *, n_chunks):
    L = CHUNK
    tt = n_chunks * L

    @pl.when(pl.program_id(1) == 0)
    def _():
        for ref in (st_ref, mt_ref, nrm_ref, m_ref):
            ref[...] = jnp.zeros_like(ref)
        ext_ref[0:8, :] = jnp.zeros((8, 2 * HK), F32)

    zs_a = zs_ref[0, :, 0:LANES]
    xg = jnp.dot(zs_a.astype(BF16), a2_ref[...], preferred_element_type=F32) + ab_ref[...]
    g = _log_sigmoid(xg) * (1.0 / GATE_NORM)
    c = _scan_rows(g, L, jnp.add, 0.0)

    qk_in = zm_ref[0, :, 0:2 * HK].astype(F32)
    ext_ref[8:8 + tt, :] = qk_in
    acc = jnp.zeros((tt, 2 * HK), F32) + cb_ref[...]
    for j in range(CONV):
        off = 8 - (CONV - 1) + j
        acc = acc + cw_ref[j:j + 1, :] * ext_ref[off:off + tt, :]
    ext_ref[0:8, :] = qk_in[tt - 8:tt]
    qk = _silu(acc)

    lane = lax.broadcasted_iota(jnp.int32, (tt, LANES), 1)
    gate_lanes = (lane >= GATE_LANE) & (lane < GATE_LANE + HEADS)
    ig = jnp.where(gate_lanes, zs_a + ib_ref[...], 0.0)
    lf = jnp.where(gate_lanes, _log_sigmoid(zs_ref[0, :, LANES:2 * LANES] + fb_ref[...]), 0.0)
    bcum = _scan_rows(lf, L, jnp.add, 0.0)
    u = ig - bcum
    cmax = _scan_rows(u, L, jnp.maximum, -jnp.inf)

    for ci in range(n_chunks):
        r = slice(ci * L, (ci + 1) * L)
        yg = _gla_chunk(zg_ref[0, r, 0:HK].astype(F32) * (DK ** -0.5), zg_ref[0, r, HK:2 * HK].astype(F32),
                        zg_ref[0, r, 2 * HK:2 * HK + HV], zg_ref[0, r, 2 * HK + HV:2 * HK + 2 * HV].astype(F32),
                        g[r], c[r], gnw_ref[...], st_ref)
        yg_ref[0, r, :] = yg.astype(yg_ref.dtype)
        ym = _mlstm_chunk(qk[r, 0:HK], qk[r, HK:2 * HK] * (DK ** -0.5), zm_ref[0, r, 2 * HK:2 * HK + HV],
                          zm_ref[0, r, 2 * HK + HV:2 * HK + 2 * HV].astype(F32), u[r], bcum[r], cmax[r],
                          mt_ref, nrm_ref, m_ref)
        ym_ref[0, r, :] = ym.astype(ym_ref.dtype)


def _mixer(zg, zm, zs, a2p, ab, gnw, cw, cb, ibp, fbp, *, tt):
    b, t, _ = zg.shape
    tok = lambda n: pl.BlockSpec((1, tt, n), lambda i, j: (i, j, 0))
    small = lambda a: pl.BlockSpec(a.shape, lambda i, j: (0,) * a.ndim)
    return pl.pallas_call(
        functools.partial(_mixer_kernel, n_chunks=tt // CHUNK),
        grid=(b, t // tt),
        in_specs=[tok(N_GLA), tok(N_ML), tok(N_SMALL)] + [small(a) for a in (a2p, ab, gnw, cw, cb, ibp, fbp)],
        out_specs=[tok(HV), tok(HV)],
        out_shape=[jax.ShapeDtypeStruct((b, t, HV), BF16)] * 2,
        scratch_shapes=[pltpu.VMEM((HV, HK), F32), pltpu.VMEM((HV, HK), F32), pltpu.VMEM((1, HK), F32),
                        pltpu.VMEM((1, LANES), F32), pltpu.VMEM((8 + tt, 2 * HK), F32)],
        compiler_params=_params(("parallel", "arbitrary")),
        name="mixer",
    )(zg, zm, zs, a2p, ab, gnw, cw, cb, ibp, fbp)


def _s5_prep_kernel(lr_ref, li_ref, lrc_ref, lic_ref, dt_ref, btr_ref, bti_ref, ctr_ref, cti_ref, d_ref,
                    t_ref, w_ref, v_ref, pa_ref, pb_ref, *, n_pow):
    P = S5_STATE
    nb = S5_BLOCK
    dt = dt_ref[0]
    lr = jnp.minimum(lr_ref[0], -1e-4)
    li = li_ref[0]
    lrc = jnp.minimum(lrc_ref[0], -1e-4)
    lic = lic_ref[0]
    dtr = jnp.exp(dt)

    def power(lr_, li_, n):
        mag = jnp.exp(lr_ * dtr * n)
        ang = li_ * dtr * n
        return mag * jnp.cos(ang), mag * jnp.sin(ang)

    ar1, ai1 = power(lr, li, 1.0)
    inv = 1.0 / (lr * lr + li * li)
    zr = ((ar1 - 1.0) * lr + ai1 * li) * inv
    zi = (ai1 * lr - (ar1 - 1.0) * li) * inv
    bbr = zr * btr_ref[0] - zi * bti_ref[0]
    bbi = zr * bti_ref[0] + zi * btr_ref[0]

    tl = (lax.broadcasted_iota(jnp.int32, (P, nb * S5_GROUP), 1) // S5_GROUP).astype(F32)
    ctr, cti = ctr_ref[0], cti_ref[0]

    def out_map(shift):
        ar, ai = power(lrc, lic, tl + shift)
        return jnp.concatenate([ctr * ar - cti * ai, -(ctr * ai + cti * ar)], axis=0)

    v_ref[0] = out_map(1.0).T.astype(v_ref.dtype)
    base = jnp.dot(jnp.concatenate([bbr, bbi], axis=1), out_map(0.0), precision=HIGHEST,
                   preferred_element_type=F32)
    lane = lax.broadcasted_iota(jnp.int32, base.shape, 1)
    base = base + jnp.where(lane == lax.broadcasted_iota(jnp.int32, base.shape, 0), d_ref[0], 0.0)
    rows = [base]
    for s in range(1, nb):
        rows.append(jnp.where(lane >= s * S5_GROUP, pltpu.roll(base, s * S5_GROUP, 1), 0.0))
    t_ref[0] = jnp.concatenate(rows, axis=0).T.astype(t_ref.dtype)

    se = (nb - 1 - lax.broadcasted_iota(jnp.int32, (nb * S5_GROUP, P), 0) // S5_GROUP).astype(F32)
    ar, ai = power(lr, li, se)
    bbr_t = jnp.concatenate([bbr] * nb, axis=0)
    bbi_t = jnp.concatenate([bbi] * nb, axis=0)
    wt = jnp.concatenate([ar * bbr_t - ai * bbi_t, ar * bbi_t + ai * bbr_t], axis=1).T
    w_ref[0] = jnp.concatenate([wt, wt[P:2 * P], wt[0:P]], axis=0).astype(w_ref.dtype)

    pr, pi = power(lr, li, float(nb))
    pa, pb = [], []
    for _ in range(n_pow):
        pa.append(jnp.concatenate([pr, pr], axis=1))
        pb.append(jnp.concatenate([-pi, pi], axis=1))
        pr, pi = pr * pr - pi * pi, 2.0 * pr * pi
    pa_ref[0] = jnp.concatenate(pa, axis=0)
    pb_ref[0] = jnp.concatenate(pb, axis=0)


def _s5_prep(lam_re, lam_im, log_step, b_re, b_im, c_re, c_im, d_skip, *, n_pow):
    g, p = lam_re.shape
    nbg = S5_BLOCK * S5_GROUP
    bt = lambda a: jnp.swapaxes(a, 1, 2)
    ct = lambda a: jnp.tile(jnp.swapaxes(a, 1, 2), (1, 1, S5_BLOCK))
    row = lambda a: a.reshape(g, 1, p)
    colv = lambda a: a.reshape(g, p, 1)
    dtile = jnp.tile(d_skip.reshape(g, 1, S5_GROUP), (1, 1, S5_BLOCK))
    spec = lambda *s: pl.BlockSpec((1,) + s, lambda i: (i,) + (0,) * len(s))
    return pl.pallas_call(
        functools.partial(_s5_prep_kernel, n_pow=n_pow),
        grid=(g,),
        in_specs=[spec(1, p), spec(1, p), spec(p, 1), spec(p, 1), spec(1, 1),
                  spec(S5_GROUP, p), spec(S5_GROUP, p), spec(p, nbg), spec(p, nbg), spec(1, nbg)],
        out_specs=[spec(nbg, nbg), spec(4 * p, nbg), spec(nbg, 2 * p), spec(n_pow, 2 * p), spec(n_pow, 2 * p)],
        out_shape=[jax.ShapeDtypeStruct((g, nbg, nbg), BF16), jax.ShapeDtypeStruct((g, 4 * p, nbg), BF16),
                   jax.ShapeDtypeStruct((g, nbg, 2 * p), BF16), jax.ShapeDtypeStruct((g, n_pow, 2 * p), F32),
                   jax.ShapeDtypeStruct((g, n_pow, 2 * p), F32)],
        compiler_params=_params(("parallel",)),
        name="s5_prep",
    )(row(lam_re), row(lam_im), colv(lam_re), colv(lam_im), log_step.reshape(g, 1, 1),
      bt(b_re), bt(b_im), ct(c_re), ct(c_im), dtile)


S5_SCAN_GROUPS = 4


def _s5_kernel(u_ref, t_ref, w_ref, v_ref, pa_ref, pb_ref, y_ref,
               ut_ref, yt_ref, zt_ref, zw_ref, xin_ref, xc_ref, *, groups, n_pow):
    nb = S5_BLOCK
    p2 = 2 * S5_STATE
    slabs, tt = u_ref.shape[1], u_ref.shape[2]
    nblk = tt // nb

    @pl.when(pl.program_id(1) == 0)
    def _():
        xc_ref[...] = jnp.zeros_like(xc_ref)

    for tl in range(nb):
        for s in range(slabs):
            ut_ref[tl, s * LANES:(s + 1) * LANES, :] = (
                u_ref[0, s, pl.ds(tl, nblk, stride=nb), :].T.astype(ut_ref.dtype))

    def group_input(g):
        c0 = pl.multiple_of(g * S5_GROUP, S5_GROUP)
        return c0, ut_ref[:, pl.ds(c0, S5_GROUP), :].reshape(nb * S5_GROUP, nblk)

    def state_in(g, carry):
        _, ut = group_input(g)
        r0 = pl.multiple_of(g * p2, p2)
        z2 = jnp.dot(w_ref[g], ut, preferred_element_type=F32)
        zt_ref[pl.ds(r0, p2), :] = z2[0:p2]
        zw_ref[pl.ds(r0, p2), :] = z2[p2:2 * p2]
        return carry

    lax.fori_loop(0, groups, state_in, 0, unroll=4)

    width = S5_SCAN_GROUPS * p2
    row = lax.broadcasted_iota(jnp.int32, (nblk, width), 0)

    def scan(ci, carry):
        l0 = pl.multiple_of(ci * width, width)
        x = zt_ref[pl.ds(l0, width), :].T
        xw = zw_ref[pl.ds(l0, width), :].T
        pa = pa_ref[:, pl.ds(l0, width)]
        pb = pb_ref[:, pl.ds(l0, width)]
        xc = xc_ref[0:1, pl.ds(l0, width)]
        xcw = xc_ref[1:2, pl.ds(l0, width)]
        first = row == 0
        x = x + jnp.where(first, xc * pa[0:1] + xcw * pb[0:1], 0.0)
        xw = xw + jnp.where(first, xcw * pa[0:1] - xc * pb[0:1], 0.0)
        for k in range(n_pow):
            sh = 1 << k
            if sh >= nblk:
                break
            xs, xws, m = pltpu.roll(x, sh, 0), pltpu.roll(xw, sh, 0), row >= sh
            x, xw = (x + jnp.where(m, xs * pa[k:k + 1] + xws * pb[k:k + 1], 0.0),
                     xw + jnp.where(m, xws * pa[k:k + 1] - xs * pb[k:k + 1], 0.0))
        x_in = jnp.where(row >= 1, pltpu.roll(x, 1, 0), xc)
        xc_ref[0:1, pl.ds(l0, width)] = x[nblk - 1:nblk]
        xc_ref[1:2, pl.ds(l0, width)] = xw[nblk - 1:nblk]
        xin_ref[pl.ds(l0, width), :] = x_in.T.astype(xin_ref.dtype)
        return carry

    lax.fori_loop(0, groups // S5_SCAN_GROUPS, scan, 0)

    def block_out(g, carry):
        c0, ut = group_input(g)
        r0 = pl.multiple_of(g * p2, p2)
        yt = jnp.dot(t_ref[g], ut, preferred_element_type=F32)
        yt = yt + jnp.dot(v_ref[g], xin_ref[pl.ds(r0, p2), :], preferred_element_type=F32)
        yt_ref[:, pl.ds(c0, S5_GROUP), :] = yt.reshape(nb, S5_GROUP, nblk)
        return carry

    lax.fori_loop(0, groups, block_out, 0, unroll=4)
    for tl in range(nb):
        for s in range(slabs):
            y_ref[0, s, pl.ds(tl, nblk, stride=nb), :] = yt_ref[tl, s * LANES:(s + 1) * LANES, :].T


def _s5(u, tmat, wmat, vmat, pa, pb, *, tt):
    b, slabs, t, _ = u.shape
    width = slabs * LANES
    groups = width // S5_GROUP
    assert groups % S5_SCAN_GROUPS == 0
    n_pow = pa.shape[0]
    nblk = tt // S5_BLOCK
    states = groups * 2 * S5_STATE
    tok = pl.BlockSpec((1, slabs, tt, LANES), lambda i, j: (i, 0, j, 0))
    return pl.pallas_call(
        functools.partial(_s5_kernel, groups=groups, n_pow=n_pow),
        grid=(b, t // tt),
        in_specs=[tok, _const_spec(tmat.shape), _const_spec(wmat.shape), _const_spec(vmat.shape),
                  _const_spec(pa.shape), _const_spec(pb.shape)],
        out_specs=tok,
        out_shape=jax.ShapeDtypeStruct((b, slabs, t, LANES), F32),
        scratch_shapes=[pltpu.VMEM((S5_BLOCK, width, nblk), BF16), pltpu.VMEM((S5_BLOCK, width, nblk), F32),
                        pltpu.VMEM((states, nblk), F32), pltpu.VMEM((states, nblk), F32),
                        pltpu.VMEM((states, nblk), BF16), pltpu.VMEM((8, states), F32)],
        compiler_params=_params(("parallel", "arbitrary")),
        name="s5",
    )(u, tmat, wmat, vmat, pa, pb)


def _merge_kernel(x_ref, mod_ref, yg_ref, ym_ref, ys_ref, zgate_ref, gluw_ref, glub_ref, pg_ref, pm_ref,
                  ps_ref, gb_ref, wo_ref, nw_ref, o_ref, *, d):
    mod = mod_ref[0]
    s = _gelu_tanh(jnp.concatenate([ys_ref[0, k] for k in range(ys_ref.shape[1])], axis=1))
    s = s * _sigmoid(jnp.dot(s.astype(BF16), gluw_ref[...], preferred_element_type=F32) + glub_ref[...])
    merged = None
    for idx, (y, p) in enumerate(((yg_ref[0], pg_ref), (ym_ref[0], pm_ref), (s.astype(BF16), ps_ref))):
        gate = _sigmoid(zgate_ref[0, :, idx * d:(idx + 1) * d].astype(F32) + gb_ref[:, idx * d:(idx + 1) * d])
        term = gate * jnp.dot(y, p[...], preferred_element_type=F32)
        merged = term if merged is None else merged + term
    y = jnp.dot(merged.astype(BF16), wo_ref[...], preferred_element_type=F32)
    o_ref[0] = x_ref[0] + (1.0 + mod[2:3]) * _rms(y, nw_ref[...])


def _merge(x, mod, yg, ym, ys, zgate, gluw, glub, pg, pm, ps, gb, wo, nw, *, tm):
    b, t, d = x.shape
    tok = lambda n: pl.BlockSpec((1, tm, n), lambda i, j: (i, j, 0))
    small = lambda a: pl.BlockSpec(a.shape, lambda i, j: (0,) * a.ndim)
    return pl.pallas_call(
        functools.partial(_merge_kernel, d=d),
        grid=(b, t // tm),
        in_specs=[tok(d), pl.BlockSpec((1, 6, d), lambda i, j: (i, 0, 0)), tok(HV), tok(HV),
                  pl.BlockSpec((1, ys.shape[1], tm, LANES), lambda i, j: (i, 0, j, 0)), tok(zgate.shape[2]), _const_spec(gluw.shape), small(glub), _const_spec(pg.shape),
                  _const_spec(pm.shape), _const_spec(ps.shape), small(gb), _const_spec(wo.shape), small(nw)],
        out_specs=tok(d),
        out_shape=jax.ShapeDtypeStruct((b, t, d), F32),
        compiler_params=_params(("parallel", "parallel")),
        name="merge_out",
    )(x, mod, yg, ym, ys, zgate, gluw, glub, pg, pm, ps, gb, wo, nw)


def _ffn_kernel(x_ref, mod_ref, n1_ref, wi_ref, wo_ref, n2_ref, o_ref, *, hidden):
    x = x_ref[0]
    mod = mod_ref[0]
    h = (_rms(x, n1_ref[...]) * (1.0 + mod[4:5]) + mod[3:4]).astype(BF16)
    gate = jnp.dot(h, wi_ref[:, 0:hidden], preferred_element_type=F32)
    up = jnp.dot(h, wi_ref[:, hidden:2 * hidden], preferred_element_type=F32)
    act = (_silu(gate) * up).astype(BF16)
    y = jnp.dot(act, wo_ref[...], preferred_element_type=F32)
    o_ref[0] = x + (1.0 + mod[5:6]) * _rms(y, n2_ref[...])


def _ffn(x, mod, n1, wi, wo, n2, *, tm):
    b, t, d = x.shape
    hidden = wo.shape[0]
    tok = pl.BlockSpec((1, tm, d), lambda i, j: (i, j, 0))
    small = lambda a: pl.BlockSpec(a.shape, lambda i, j: (0,) * a.ndim)
    return pl.pallas_call(
        functools.partial(_ffn_kernel, hidden=hidden),
        grid=(b, t // tm),
        in_specs=[tok, pl.BlockSpec((1, 6, d), lambda i, j: (i, 0, 0)), small(n1), _const_spec(wi.shape),
                  _const_spec(wo.shape), small(n2)],
        out_specs=tok,
        out_shape=jax.ShapeDtypeStruct((b, t, d), F32),
        compiler_params=_params(("parallel", "parallel")),
        name="ffn",
    )(x, mod, n1, wi, wo, n2)


def _regroup_w_in(w_in, d, s5w):
    sizes = (HK, HK, HV, RANK, HV, HK, HK, HV, HEADS, HEADS, HV, s5w, 3 * d)
    offs, a = [], 0
    for s in sizes:
        offs.append(a)
        a += s
    col = lambda i: w_in[:, offs[i]:offs[i] + sizes[i]]
    zeros = lambda n: jnp.zeros((w_in.shape[0], n), w_in.dtype)
    small_a = jnp.concatenate([col(3), col(8), zeros(LANES - RANK - HEADS)], axis=1)
    small_b = jnp.concatenate([zeros(GATE_LANE), col(9), zeros(LANES - GATE_LANE - HEADS)], axis=1)
    return jnp.concatenate([col(0), col(1), col(2), col(4), col(5), col(6), col(7), col(10), col(11),
                            small_a, small_b, col(12)], axis=1).astype(BF16)


def _lane_pad(vec, lane0):
    return jnp.zeros((1, LANES), F32).at[0, lane0:lane0 + vec.shape[0]].set(vec)


def kernel(x, c, ada_w, ada_b, pre1_w, post1_w, pre2_w, post2_w, w_in, gla_a2, gla_a_b, gla_norm_w, ml_conv_w, ml_conv_b, ml_i_b, ml_f_b, s5_lam_re, s5_lam_im, s5_log_step, s5_b_re, s5_b_im, s5_c_re, s5_c_im, s5_d, s5_glu_w, s5_glu_b, proj_gla, proj_ml, proj_s5, branch_gate_b, w_out, ffn_w_in, ffn_w_out):
    b, t, d = x.shape
    depth = ada_w.shape[0]
    s5w = s5_d.shape[1]
    assert RANK + HEADS <= GATE_LANE + HEADS <= LANES and GATE_LANE == RANK
    assert t % 256 == 0 and d % LANES == 0
    tm = 512 if t % 512 == 0 else 256
    tt = 256
    tt_s5 = 2048 if t % 2048 == 0 else t
    n_pow = max(1, (tt_s5 // S5_BLOCK - 1).bit_length())

    mod_all = _ada_mod(c, ada_w, ada_b)
    row = lambda a: a.reshape(1, -1)
    for l in range(depth):
        mod = mod_all[l].reshape(b, 6, d)
        w1 = _regroup_w_in(w_in[l], d, s5w)
        zg, zm, zs5, zsm, zgate = _inproj(x, mod, row(pre1_w[l]), w1, tm=tm, s5w=s5w, ngate=3 * d)

        a2p = jnp.zeros((LANES, HK), BF16).at[0:RANK].set(gla_a2[l].astype(BF16))
        y_gla, y_ml = _mixer(zg, zm, zsm, a2p, row(gla_a_b[l]), row(gla_norm_w[l]), ml_conv_w[l],
                             row(ml_conv_b[l]), _lane_pad(ml_i_b[l], GATE_LANE),
                             _lane_pad(ml_f_b[l], GATE_LANE), tt=tt)

        tmat, wmat, vmat, pa, pb = _s5_prep(s5_lam_re[l], s5_lam_im[l], s5_log_step[l], s5_b_re[l], s5_b_im[l],
                                            s5_c_re[l], s5_c_im[l], s5_d[l], n_pow=n_pow)
        lanes_of = lambda a: a.transpose(1, 0, 2).reshape(n_pow, -1)
        ys = _s5(zs5, tmat, wmat, vmat, lanes_of(pa), lanes_of(pb), tt=tt_s5)

        x = _merge(x, mod, y_gla, y_ml, ys, zgate, s5_glu_w[l].astype(BF16), row(s5_glu_b[l]),
                   proj_gla[l].astype(BF16), proj_ml[l].astype(BF16), proj_s5[l].astype(BF16),
                   row(branch_gate_b[l]), w_out[l].astype(BF16), row(post1_w[l]), tm=tm)
        x = _ffn(x, mod, row(pre2_w[l]), ffn_w_in[l].astype(BF16), ffn_w_out[l].astype(BF16),
                 row(post2_w[l]), tm=tm)
    return x
```

```python
import functools
import math

import jax
import jax.numpy as jnp
from jax import lax
from jax.experimental import pallas as pl
from jax.experimental.pallas import tpu as pltpu

HEADS = 4
DK = 64
DV = 128
RANK = 16
GATE_NORM = 16.0
CONV = 4
S5_GROUP = 16
S5_STATE = 64
CHUNK = 64
S5_BLOCK = 16
RMS_EPS = 1e-6
LANES = 128
GATE_LANE = 16
VMEM_LIMIT = 56 * 1024 * 1024

HK = HEADS * DK
HV = HEADS * DV
F32 = jnp.float32
BF16 = jnp.bfloat16
HIGHEST = lax.Precision.HIGHEST
NT = (((1,), (1,)), ((), ()))
TN = (((0,), (0,)), ((), ()))


def _params(sem):
    return pltpu.CompilerParams(dimension_semantics=sem, vmem_limit_bytes=VMEM_LIMIT)


def _const_spec(shape):
    nd = len(shape)
    return pl.BlockSpec(shape, lambda *_: (0,) * nd, pipeline_mode=pl.Buffered(1))


def _sigmoid(x):
    return 0.5 * jnp.tanh(0.5 * x) + 0.5


def _silu(x):
    return x * _sigmoid(x)


def _log_sigmoid(x):
    return jnp.minimum(x, 0.0) - jnp.log(1.0 + jnp.exp(-jnp.abs(x)))


def _gelu_tanh(x):
    c = math.sqrt(2.0 / math.pi)
    return 0.5 * x * (1.0 + jnp.tanh(c * (x + 0.044715 * (x * x * x))))


def _rms(x, w):
    ms = jnp.mean(x * x, axis=-1, keepdims=True)
    return x * lax.rsqrt(ms + RMS_EPS) * w


def _ada_kernel(c_ref, w_ref, b_ref, o_ref):
    ca = _silu(c_ref[...]).astype(BF16)
    o_ref[0] = jnp.dot(ca, w_ref[0].astype(BF16), preferred_element_type=F32) + b_ref[0]


def _ada_mod(c, ada_w, ada_b):
    nl, d, n = ada_w.shape
    b = c.shape[0]
    tn = 1024
    return pl.pallas_call(
        _ada_kernel,
        grid=(nl, n // tn),
        in_specs=[pl.BlockSpec((b, d), lambda l, j: (0, 0)),
                  pl.BlockSpec((1, d, tn), lambda l, j: (l, 0, j)),
                  pl.BlockSpec((1, 1, tn), lambda l, j: (l, 0, j))],
        out_specs=pl.BlockSpec((1, b, tn), lambda l, j: (l, 0, j)),
        out_shape=jax.ShapeDtypeStruct((nl, b, n), F32),
        compiler_params=_params(("arbitrary", "arbitrary")),
        name="ada_mod",
    )(c, ada_w, ada_b.reshape(nl, 1, n))


N_GLA = 2 * HK + 2 * HV
N_ML = 2 * HK + 2 * HV
N_SMALL = 2 * LANES


def _inproj_kernel(x_ref, mod_ref, nw_ref, w_ref, zg_ref, zm_ref, zs_ref, zsm_ref, zgate_ref, *, offs):
    x = x_ref[0]
    mod = mod_ref[0]
    h = _rms(x, nw_ref[...]) * (1.0 + mod[1:2]) + mod[0:1]
    hb = h.astype(BF16)
    for ref, (a, b) in zip((zg_ref, zm_ref, zs_ref, zsm_ref, zgate_ref), offs):
        res = jnp.dot(hb, w_ref[:, a:b], preferred_element_type=F32).astype(ref.dtype)
        if ref is zs_ref:
            for s in range(ref.shape[1]):
                ref[0, s] = res[:, s * LANES:(s + 1) * LANES]
        else:
            ref[0] = res


def _inproj(x, mod, nw, w, *, tm, s5w, ngate):
    b, t, d = x.shape
    sizes = (N_GLA, N_ML, s5w, N_SMALL, ngate)
    offs, a = [], 0
    for s in sizes:
        offs.append((a, a + s))
        a += s
    dts = (BF16, BF16, F32, F32, BF16)
    tok = lambda n: pl.BlockSpec((1, tm, n), lambda i, j: (i, j, 0))
    slabs = s5w // LANES
    out_specs = [tok(s) for s in sizes]
    out_shape = [jax.ShapeDtypeStruct((b, t, s), dt) for s, dt in zip(sizes, dts)]
    out_specs[2] = pl.BlockSpec((1, slabs, tm, LANES), lambda i, j: (i, 0, j, 0))
    out_shape[2] = jax.ShapeDtypeStruct((b, slabs, t, LANES), F32)
    return pl.pallas_call(
        functools.partial(_inproj_kernel, offs=tuple(offs)),
        grid=(b, t // tm),
        in_specs=[tok(d),
                  pl.BlockSpec((1, 6, d), lambda i, j: (i, 0, 0)),
                  pl.BlockSpec((1, d), lambda i, j: (0, 0)),
                  _const_spec(w.shape)],
        out_specs=out_specs,
        out_shape=out_shape,
        compiler_params=_params(("parallel", "parallel")),
        name="in_proj",
    )(x, mod, nw, w)


def _block_diag_rows(a, width):
    lane_head = lax.broadcasted_iota(jnp.int32, a.shape, 1) // width
    zero = jnp.zeros_like(a)
    return jnp.concatenate([jnp.where(lane_head == h, a, zero) for h in range(HEADS)], axis=0)


def _head_mask(rows_per_head, cols_per_head):
    r = lax.broadcasted_iota(jnp.int32, (HEADS * rows_per_head, HEADS * cols_per_head), 0) // rows_per_head
    c = lax.broadcasted_iota(jnp.int32, (HEADS * rows_per_head, HEADS * cols_per_head), 1) // cols_per_head
    return r == c


def _scan_rows(x, period, op, fill):
    pos = lax.broadcasted_iota(jnp.int32, x.shape, 0) % period
    sh = 1
    while sh < period:
        x = op(x, jnp.where(pos >= sh, pltpu.roll(x, sh, 0), fill))
        sh *= 2
    return x


def _bcast_heads(tile, per_head):
    rows = tile.shape[0]
    full = [jnp.broadcast_to(tile[:, GATE_LANE + h:GATE_LANE + h + 1], (rows, LANES)) for h in range(HEADS)]
    if per_head == LANES:
        return jnp.concatenate(full, axis=1)
    assert 2 * per_head == LANES and HEADS % 2 == 0
    low = lax.broadcasted_iota(jnp.int32, (rows, LANES), 1) < per_head
    return jnp.concatenate([jnp.where(low, full[h], full[h + 1]) for h in range(0, HEADS, 2)], axis=1)


def _gla_levels(q, k, g, c):
    n = q.shape[0]
    row = lax.broadcasted_iota(jnp.int32, q.shape, 0)
    out = []
    hs = 1
    while 2 * hs <= n:
        blk = 2 * hs
        pos = row % blk
        upper = pos >= hs
        if hs == 1:
            x = jnp.where(upper, g, 0.0)
        elif hs == 2:
            g_prev = pltpu.roll(g, 1, 0)
            g_next = pltpu.roll(g, n - 1, 0)
            x = jnp.where(pos == 2, g, jnp.where(pos == 3, g + g_prev, jnp.where(pos == 0, g_next, 0.0)))
        else:
            c3 = c.reshape(n // blk, blk, c.shape[1])
            anchor = jnp.broadcast_to(c3[:, hs - 1:hs, :], c3.shape).reshape(c.shape)
            x = -jnp.abs(c - anchor)
        out.append((jnp.where(upper, q, k) * jnp.exp(x), hs))
        hs = blk
    return out


def _pair_block_diag(a):
    low = lax.broadcasted_iota(jnp.int32, a.shape, 1) < a.shape[1] // 2
    zero = jnp.zeros_like(a)
    return jnp.concatenate([jnp.where(low, a, zero), jnp.where(low, zero, a)], axis=0)


def _update_state(st_ref, decay_row, upd):
    for h in range(HEADS):
        rs = slice(h * DV, (h + 1) * DV)
        c0 = (h * DK) // LANES * LANES
        cs = slice(c0, c0 + LANES)
        own = (lax.broadcasted_iota(jnp.int32, (DV, LANES), 1) + c0) // DK == h
        st_ref[rs, cs] = decay_row[:, cs] * st_ref[rs, cs] + jnp.where(own, upd[rs, cs], 0.0)


def _gla_chunk(q, k, v, gr, g, c, nw, st_ref):
    L = q.shape[0]
    col = lax.broadcasted_iota(jnp.int32, (L, LANES), 1) % DK
    rowi = lax.broadcasted_iota(jnp.int32, (L, LANES), 0)
    qb, kb = q.astype(BF16), k.astype(BF16)
    levels = [(t.astype(BF16), hs) for t, hs in _gla_levels(q, k, g, c)]
    pairs = []
    for p in range(HK // LANES):
        ls = slice(p * LANES, (p + 1) * LANES)
        s = lax.dot_general(qb[:, ls], _pair_block_diag(kb[:, ls]), NT, preferred_element_type=F32)
        acc = jnp.where(rowi == col, s, 0.0)
        for tb, hs in levels:
            s = lax.dot_general(tb[:, ls], _pair_block_diag(tb[:, ls]), NT, preferred_element_type=F32)
            blk = 2 * hs
            valid = ((rowi // blk) == (col // blk)) & (rowi % blk >= hs) & (col % blk < hs)
            acc = acc + jnp.where(valid, s, 0.0)
        pairs.append(acc)
    scores = jnp.concatenate(pairs, axis=1)

    st = st_ref[...]
    o = jnp.dot(scores.astype(BF16), _block_diag_rows(v, DV), preferred_element_type=F32)
    o = o + lax.dot_general((q * jnp.exp(c)).astype(BF16), st.astype(BF16), NT,
                            preferred_element_type=F32)
    c_last = c[L - 1:L]
    kd = (k * jnp.exp(c_last - c)).astype(BF16)
    upd = lax.dot_general(v, kd, TN, preferred_element_type=F32)
    _update_state(st_ref, jnp.exp(c_last), upd)

    parts = []
    for h in range(HEADS):
        oh = o[:, h * DV:(h + 1) * DV]
        parts.append(oh * lax.rsqrt(jnp.mean(oh * oh, axis=-1, keepdims=True) + RMS_EPS))
    return jnp.concatenate(parts, axis=1) * nw * _silu(gr)


def _mlstm_chunk(q, k, v, og, u, bcum, cmax, mt_ref, nrm_ref, m_ref):
    L = q.shape[0]
    lane = lax.broadcasted_iota(jnp.int32, (L, LANES), 1)
    gate_lanes = (lane >= GATE_LANE) & (lane < GATE_LANE + HEADS)
    m_prev = m_ref[...]
    big_m = jnp.maximum(m_prev, cmax)
    m_t = bcum + big_m
    m_new = m_t[L - 1:L]
    b_last = bcum[L - 1:L]

    ut = jnp.concatenate([u, u], axis=0).T
    low = lax.broadcasted_iota(jnp.int32, (1, 2 * L), 1) < L
    urow = jnp.concatenate([jnp.where(low, ut[GATE_LANE + h:GATE_LANE + h + 1], ut[GATE_LANE + h + 1:GATE_LANE + h + 2])
                            for h in range(0, HEADS, 2)], axis=1)
    causal = (lax.broadcasted_iota(jnp.int32, (L, HK), 1) % DK) <= lax.broadcasted_iota(jnp.int32, (L, HK), 0)
    dmat = _bcast_heads(-big_m, DK) + urow
    pmat = jnp.where(causal, jnp.exp(jnp.where(causal, dmat, 0.0)), 0.0)

    qb = q.astype(BF16)
    qk_s = lax.dot_general(qb, _block_diag_rows(k.astype(BF16), DK), NT, preferred_element_type=F32)
    sm = (qk_s * pmat).astype(BF16)

    mt = mt_ref[...]
    nrm = nrm_ref[...]
    den_row = lax.broadcasted_iota(jnp.int32, (LANES, HK), 0)
    den_head = lax.broadcasted_iota(jnp.int32, (LANES, HK), 1) // DK + GATE_LANE
    nrm_rows = jnp.where(den_row == den_head, jnp.broadcast_to(nrm, (LANES, HK)), 0.0)
    mem_ext = jnp.concatenate([mt, nrm_rows], axis=0).astype(BF16)
    inter = lax.dot_general(qb, mem_ext, NT, preferred_element_type=F32)
    ones_den = (lax.broadcasted_iota(jnp.int32, (HK, LANES), 0) // DK + GATE_LANE
                == lax.broadcasted_iota(jnp.int32, (HK, LANES), 1)).astype(BF16)
    v_ext = jnp.concatenate([_block_diag_rows(v, DV), ones_den], axis=1)
    intra = jnp.dot(sm, v_ext, preferred_element_type=F32)

    lw = jnp.where(gate_lanes, m_prev - big_m, 0.0)
    den = jnp.exp(lw) * inter[:, HV:] + intra[:, HV:]
    rden = 1.0 / jnp.maximum(jnp.abs(den), jnp.exp(-m_t))
    hout = (jnp.exp(_bcast_heads(lw, DV)) * inter[:, 0:HV] + intra[:, 0:HV]) * _bcast_heads(rden, DV)

    wk = jnp.where(gate_lanes, b_last - m_new + u, 0.0)
    cd = jnp.where(gate_lanes[0:1], b_last + m_prev - m_new, 0.0)
    kw = k * jnp.exp(_bcast_heads(wk, DK))
    decay = jnp.exp(_bcast_heads(cd, DK))
    upd = lax.dot_general(v, kw.astype(BF16), TN, preferred_element_type=F32)
    _update_state(mt_ref, decay, upd)
    nrm_ref[...] = decay * nrm + jnp.sum(kw, axis=0, keepdims=True)
    m_ref[...] = m_new
    return _sigmoid(og) * hout


def _mixers(zg_ref, zm_ref, zs_ref, a2_ref, ab_ref, gnw_ref, cw_ref, cb_ref, ib_ref, fb_ref,
            yg_ref, ym_ref, st_ref, mt_ref, nrm_ref, m_ref, ext_ref, n_chunks):
    L = CHUNK
    tt = n_chunks * L

    zs_a = zs_ref[:, 0:LANES]
    xg = jnp.dot(zs_a.astype(BF16), a2_ref[...], preferred_element_type=F32) + ab_ref[...]
    g = _log_sigmoid(xg) * (1.0 / GATE_NORM)
    c = _scan_rows(g, L, jnp.add, 0.0)

    qk_in = zm_ref[:, 0:2 * HK].astype(F32)
    ext_ref[8:8 + tt, :] = qk_in
    acc = jnp.zeros((tt, 2 * HK), F32) + cb_ref[...]
    for j in range(CONV):
        off = 8 - (CONV - 1) + j
        acc = acc + cw_ref[j:j + 1, :] * ext_ref[off:off + tt, :]
    ext_ref[0:8, :] = qk_in[tt - 8:tt]
    qk = _silu(acc)

    lane = lax.broadcasted_iota(jnp.int32, (tt, LANES), 1)
    gate_lanes = (lane >= GATE_LANE) & (lane < GATE_LANE + HEADS)
    ig = jnp.where(gate_lanes, zs_a + ib_ref[...], 0.0)
    lf = jnp.where(gate_lanes, _log_sigmoid(zs_ref[:, LANES:2 * LANES] + fb_ref[...]), 0.0)
    bcum = _scan_rows(lf, L, jnp.add, 0.0)
    u = ig - bcum
    cmax = _scan_rows(u, L, jnp.maximum, -jnp.inf)

    for ci in range(n_chunks):
        r = slice(ci * L, (ci + 1) * L)
        yg = _gla_chunk(zg_ref[r, 0:HK].astype(F32) * (DK ** -0.5), zg_ref[r, HK:2 * HK].astype(F32),
                        zg_ref[r, 2 * HK:2 * HK + HV], zg_ref[r, 2 * HK + HV:2 * HK + 2 * HV].astype(F32),
                        g[r], c[r], gnw_ref[...], st_ref)
        yg_ref[r, :] = yg.astype(yg_ref.dtype)
        ym = _mlstm_chunk(qk[r, 0:HK], qk[r, HK:2 * HK] * (DK ** -0.5), zm_ref[r, 2 * HK:2 * HK + HV],
                          zm_ref[r, 2 * HK + HV:2 * HK + 2 * HV].astype(F32), u[r], bcum[r], cmax[r],
                          mt_ref, nrm_ref, m_ref)
        ym_ref[r, :] = ym.astype(ym_ref.dtype)


def _mixer_kernel(zg_ref, zm_ref, zs_ref, a2_ref, ab_ref, gnw_ref, cw_ref, cb_ref, ib_ref, fb_ref,
                  yg_ref, ym_ref, st_ref, mt_ref, nrm_ref, m_ref, ext_ref, *, n_chunks):
    @pl.when(pl.program_id(1) == 0)
    def _():
        for ref in (st_ref, mt_ref, nrm_ref, m_ref):
            ref[...] = jnp.zeros_like(ref)
        ext_ref[0:8, :] = jnp.zeros((8, 2 * HK), F32)

    _mixers(zg_ref.at[0], zm_ref.at[0], zs_ref.at[0], a2_ref, ab_ref, gnw_ref, cw_ref, cb_ref, ib_ref, fb_ref,
            yg_ref.at[0], ym_ref.at[0], st_ref, mt_ref, nrm_ref, m_ref, ext_ref, n_chunks)


def _mixer(zg, zm, zs, a2p, ab, gnw, cw, cb, ibp, fbp, *, tt):
    b, t, _ = zg.shape
    tok = lambda n: pl.BlockSpec((1, tt, n), lambda i, j: (i, j, 0))
    small = lambda a: pl.BlockSpec(a.shape, lambda i, j: (0,) * a.ndim)
    return pl.pallas_call(
        functools.partial(_mixer_kernel, n_chunks=tt // CHUNK),
        grid=(b, t // tt),
        in_specs=[tok(N_GLA), tok(N_ML), tok(N_SMALL)] + [small(a) for a in (a2p, ab, gnw, cw, cb, ibp, fbp)],
        out_specs=[tok(HV), tok(HV)],
        out_shape=[jax.ShapeDtypeStruct((b, t, HV), BF16)] * 2,
        scratch_shapes=[pltpu.VMEM((HV, HK), F32), pltpu.VMEM((HV, HK), F32), pltpu.VMEM((1, HK), F32),
                        pltpu.VMEM((1, LANES), F32), pltpu.VMEM((8 + tt, 2 * HK), F32)],
        compiler_params=_params(("parallel", "arbitrary")),
        name="mixer",
    )(zg, zm, zs, a2p, ab, gnw, cw, cb, ibp, fbp)


def _s5_prep_kernel(lr_ref, li_ref, lrc_ref, lic_ref, dt_ref, btr_ref, bti_ref, ctr_ref, cti_ref, d_ref,
                    t_ref, w_ref, v_ref, pa_ref, pb_ref, *, n_pow):
    P = S5_STATE
    nb = S5_BLOCK
    dt = dt_ref[0]
    lr = jnp.minimum(lr_ref[0], -1e-4)
    li = li_ref[0]
    lrc = jnp.minimum(lrc_ref[0], -1e-4)
    lic = lic_ref[0]
    dtr = jnp.exp(dt)

    def power(lr_, li_, n):
        mag = jnp.exp(lr_ * dtr * n)
        ang = li_ * dtr * n
        return mag * jnp.cos(ang), mag * jnp.sin(ang)

    ar1, ai1 = power(lr, li, 1.0)
    inv = 1.0 / (lr * lr + li * li)
    zr = ((ar1 - 1.0) * lr + ai1 * li) * inv
    zi = (ai1 * lr - (ar1 - 1.0) * li) * inv
    bbr = zr * btr_ref[0] - zi * bti_ref[0]
    bbi = zr * bti_ref[0] + zi * btr_ref[0]

    tl = (lax.broadcasted_iota(jnp.int32, (P, nb * S5_GROUP), 1) // S5_GROUP).astype(F32)
    ctr, cti = ctr_ref[0], cti_ref[0]

    def out_map(shift):
        ar, ai = power(lrc, lic, tl + shift)
        return jnp.concatenate([ctr * ar - cti * ai, -(ctr * ai + cti * ar)], axis=0)

    v_ref[0] = out_map(1.0).T.astype(v_ref.dtype)
    base = jnp.dot(jnp.concatenate([bbr, bbi], axis=1), out_map(0.0), precision=HIGHEST,
                   preferred_element_type=F32)
    lane = lax.broadcasted_iota(jnp.int32, base.shape, 1)
    base = base + jnp.where(lane == lax.broadcasted_iota(jnp.int32, base.shape, 0), d_ref[0], 0.0)
    rows = [base]
    for s in range(1, nb):
        rows.append(jnp.where(lane >= s * S5_GROUP, pltpu.roll(base, s * S5_GROUP, 1), 0.0))
    t_ref[0] = jnp.concatenate(rows, axis=0).T.astype(t_ref.dtype)

    se = (nb - 1 - lax.broadcasted_iota(jnp.int32, (nb * S5_GROUP, P), 0) // S5_GROUP).astype(F32)
    ar, ai = power(lr, li, se)
    bbr_t = jnp.concatenate([bbr] * nb, axis=0)
    bbi_t = jnp.concatenate([bbi] * nb, axis=0)
    wt = jnp.concatenate([ar * bbr_t - ai * bbi_t, ar * bbi_t + ai * bbr_t], axis=1).T
    w_ref[0] = jnp.concatenate([wt, wt[P:2 * P], wt[0:P]], axis=0).astype(w_ref.dtype)

    pr, pi = power(lr, li, float(nb))
    pa, pb = [], []
    for _ in range(n_pow):
        pa.append(jnp.concatenate([pr, pr], axis=1))
        pb.append(jnp.concatenate([-pi, pi], axis=1))
        pr, pi = pr * pr - pi * pi, 2.0 * pr * pi
    pa_ref[0] = jnp.concatenate(pa, axis=0)
    pb_ref[0] = jnp.concatenate(pb, axis=0)


def _s5_prep(lam_re, lam_im, log_step, b_re, b_im, c_re, c_im, d_skip, *, n_pow):
    g, p = lam_re.shape
    nbg = S5_BLOCK * S5_GROUP
    bt = lambda a: jnp.swapaxes(a, 1, 2)
    ct = lambda a: jnp.tile(jnp.swapaxes(a, 1, 2), (1, 1, S5_BLOCK))
    row = lambda a: a.reshape(g, 1, p)
    colv = lambda a: a.reshape(g, p, 1)
    dtile = jnp.tile(d_skip.reshape(g, 1, S5_GROUP), (1, 1, S5_BLOCK))
    spec = lambda *s: pl.BlockSpec((1,) + s, lambda i: (i,) + (0,) * len(s))
    return pl.pallas_call(
        functools.partial(_s5_prep_kernel, n_pow=n_pow),
        grid=(g,),
        in_specs=[spec(1, p), spec(1, p), spec(p, 1), spec(p, 1), spec(1, 1),
                  spec(S5_GROUP, p), spec(S5_GROUP, p), spec(p, nbg), spec(p, nbg), spec(1, nbg)],
        out_specs=[spec(nbg, nbg), spec(4 * p, nbg), spec(nbg, 2 * p), spec(n_pow, 2 * p), spec(n_pow, 2 * p)],
        out_shape=[jax.ShapeDtypeStruct((g, nbg, nbg), BF16), jax.ShapeDtypeStruct((g, 4 * p, nbg), BF16),
                   jax.ShapeDtypeStruct((g, nbg, 2 * p), BF16), jax.ShapeDtypeStruct((g, n_pow, 2 * p), F32),
                   jax.ShapeDtypeStruct((g, n_pow, 2 * p), F32)],
        compiler_params=_params(("parallel",)),
        name="s5_prep",
    )(row(lam_re), row(lam_im), colv(lam_re), colv(lam_im), log_step.reshape(g, 1, 1),
      bt(b_re), bt(b_im), ct(c_re), ct(c_im), dtile)


S5_SCAN_GROUPS = 4


def _s5_kernel(u_ref, t_ref, w_ref, v_ref, pa_ref, pb_ref, y_ref,
               ut_ref, yt_ref, zt_ref, zw_ref, xin_ref, xc_ref, *, groups, n_pow):
    nb = S5_BLOCK
    p2 = 2 * S5_STATE
    slabs, tt = u_ref.shape[1], u_ref.shape[2]
    nblk = tt // nb

    @pl.when(pl.program_id(1) == 0)
    def _():
        xc_ref[...] = jnp.zeros_like(xc_ref)

    gps = S5_SCAN_GROUPS
    n_chunks = groups // gps
    chunks_per_slab = LANES // S5_GROUP // gps

    def relayout_in(s):
        for tl in range(nb):
            ut_ref[tl, s * LANES:(s + 1) * LANES, :] = (
                u_ref[0, s, pl.ds(tl, nblk, stride=nb), :].T.astype(ut_ref.dtype))

    def relayout_out(s):
        for tl in range(nb):
            y_ref[0, s, pl.ds(tl, nblk, stride=nb), :] = yt_ref[tl, s * LANES:(s + 1) * LANES, :].T

    def group_input(g):
        return ut_ref[:, g * S5_GROUP:(g + 1) * S5_GROUP, :].reshape(nb * S5_GROUP, nblk)

    def state_in(ci):
        for g in range(ci * gps, (ci + 1) * gps):
            z2 = jnp.dot(w_ref[g], group_input(g), preferred_element_type=F32)
            zt_ref[g * p2:(g + 1) * p2, :] = z2[0:p2]
            zw_ref[g * p2:(g + 1) * p2, :] = z2[p2:2 * p2]

    width = gps * p2
    row = lax.broadcasted_iota(jnp.int32, (nblk, width), 0)

    def scan(ci):
        l0 = ci * width
        x = zt_ref[pl.ds(l0, width), :].T
        xw = zw_ref[pl.ds(l0, width), :].T
        pa = pa_ref[:, pl.ds(l0, width)]
        pb = pb_ref[:, pl.ds(l0, width)]
        xc = xc_ref[0:1, pl.ds(l0, width)]
        xcw = xc_ref[1:2, pl.ds(l0, width)]
        first = row == 0
        x = x + jnp.where(first, xc * pa[0:1] + xcw * pb[0:1], 0.0)
        xw = xw + jnp.where(first, xcw * pa[0:1] - xc * pb[0:1], 0.0)
        for k in range(n_pow):
            sh = 1 << k
            if sh >= nblk:
                break
            xs, xws, m = pltpu.roll(x, sh, 0), pltpu.roll(xw, sh, 0), row >= sh
            x, xw = (x + jnp.where(m, xs * pa[k:k + 1] + xws * pb[k:k + 1], 0.0),
                     xw + jnp.where(m, xws * pa[k:k + 1] - xs * pb[k:k + 1], 0.0))
        x_in = jnp.where(row >= 1, pltpu.roll(x, 1, 0), xc)
        xc_ref[0:1, pl.ds(l0, width)] = x[nblk - 1:nblk]
        xc_ref[1:2, pl.ds(l0, width)] = xw[nblk - 1:nblk]
        xin_ref[pl.ds(l0, width), :] = x_in.T.astype(xin_ref.dtype)

    def block_out(ci):
        for g in range(ci * gps, (ci + 1) * gps):
            yt = jnp.dot(t_ref[g], group_input(g), preferred_element_type=F32)
            yt = yt + jnp.dot(v_ref[g], xin_ref[g * p2:(g + 1) * p2, :], preferred_element_type=F32)
            yt_ref[:, g * S5_GROUP:(g + 1) * S5_GROUP, :] = yt.reshape(nb, S5_GROUP, nblk)

    relayout_in(0)
    for step in range(n_chunks + 2):
        nxt = step // chunks_per_slab + 1
        if step % chunks_per_slab == 0 and nxt < slabs:
            relayout_in(nxt)
        if step < n_chunks:
            state_in(step)
        if 1 <= step <= n_chunks:
            scan(step - 1)
        if step >= 2:
            block_out(step - 2)
            if (step - 2) % chunks_per_slab == chunks_per_slab - 1:
                relayout_out((step - 2) // chunks_per_slab)


def _s5(u, tmat, wmat, vmat, pa, pb, *, tt):
    b, slabs, t, _ = u.shape
    width = slabs * LANES
    groups = width // S5_GROUP
    assert groups % S5_SCAN_GROUPS == 0
    n_pow = pa.shape[0]
    nblk = tt // S5_BLOCK
    states = groups * 2 * S5_STATE
    tok = pl.BlockSpec((1, slabs, tt, LANES), lambda i, j: (i, 0, j, 0))
    return pl.pallas_call(
        functools.partial(_s5_kernel, groups=groups, n_pow=n_pow),
        grid=(b, t // tt),
        in_specs=[tok, _const_spec(tmat.shape), _const_spec(wmat.shape), _const_spec(vmat.shape),
                  _const_spec(pa.shape), _const_spec(pb.shape)],
        out_specs=tok,
        out_shape=jax.ShapeDtypeStruct((b, slabs, t, LANES), F32),
        scratch_shapes=[pltpu.VMEM((S5_BLOCK, width, nblk), BF16), pltpu.VMEM((S5_BLOCK, width, nblk), F32),
                        pltpu.VMEM((states, nblk), F32), pltpu.VMEM((states, nblk), F32),
                        pltpu.VMEM((states, nblk), BF16), pltpu.VMEM((8, states), F32)],
        compiler_params=_params(("parallel", "arbitrary")),
        name="s5",
    )(u, tmat, wmat, vmat, pa, pb)


def _merge_kernel(x_ref, mod_ref, yg_ref, ym_ref, ys_ref, zgate_ref, gluw_ref, glub_ref, pg_ref, pm_ref,
                  ps_ref, gb_ref, wo_ref, nw_ref, o_ref, *, d):
    mod = mod_ref[0]
    s = _gelu_tanh(jnp.concatenate([ys_ref[0, k] for k in range(ys_ref.shape[1])], axis=1))
    s = s * _sigmoid(jnp.dot(s.astype(BF16), gluw_ref[...], preferred_element_type=F32) + glub_ref[...])
    merged = None
    for idx, (y, p) in enumerate(((yg_ref[0], pg_ref), (ym_ref[0], pm_ref), (s.astype(BF16), ps_ref))):
        th = jnp.tanh((zgate_ref[0, :, idx * d:(idx + 1) * d] + gb_ref[:, idx * d:(idx + 1) * d]) * 0.5)
        term = (th.astype(F32) + 1.0) * jnp.dot(y, p[...], preferred_element_type=F32)
        merged = term if merged is None else merged + term
    y = jnp.dot((0.5 * merged).astype(BF16), wo_ref[...], preferred_element_type=F32)
    o_ref[0] = x_ref[0] + (1.0 + mod[2:3]) * _rms(y, nw_ref[...])


def _merge(x, mod, yg, ym, ys, zgate, gluw, glub, pg, pm, ps, gb, wo, nw, *, tm):
    b, t, d = x.shape
    tok = lambda n: pl.BlockSpec((1, tm, n), lambda i, j: (i, j, 0))
    small = lambda a: pl.BlockSpec(a.shape, lambda i, j: (0,) * a.ndim)
    return pl.pallas_call(
        functools.partial(_merge_kernel, d=d),
        grid=(b, t // tm),
        in_specs=[tok(d), pl.BlockSpec((1, 6, d), lambda i, j: (i, 0, 0)), tok(HV), tok(HV),
                  pl.BlockSpec((1, ys.shape[1], tm, LANES), lambda i, j: (i, 0, j, 0)), tok(zgate.shape[2]), _const_spec(gluw.shape), small(glub), _const_spec(pg.shape),
                  _const_spec(pm.shape), _const_spec(ps.shape), small(gb), _const_spec(wo.shape), small(nw)],
        out_specs=tok(d),
        out_shape=jax.ShapeDtypeStruct((b, t, d), F32),
        compiler_params=_params(("parallel", "parallel")),
        name="merge_out",
    )(x, mod, yg, ym, ys, zgate, gluw, glub, pg, pm, ps, gb, wo, nw)


SUBTILES = 2


def _ffn_kernel(x_ref, mod_ref, n1_ref, wi_ref, wo_ref, n2_ref, o_ref, *, hidden):
    mod = mod_ref[0]
    rows = x_ref.shape[1] // SUBTILES
    for s in range(SUBTILES):
        r = slice(s * rows, (s + 1) * rows)
        x = x_ref[0, r, :]
        h = (_rms(x, n1_ref[...]) * (1.0 + mod[4:5]) + mod[3:4]).astype(BF16)
        gate = jnp.dot(h, wi_ref[:, 0:hidden], preferred_element_type=F32)
        up = jnp.dot(h, wi_ref[:, hidden:2 * hidden], preferred_element_type=F32)
        act = (_silu(gate) * up).astype(BF16)
        y = jnp.dot(act, wo_ref[...], preferred_element_type=F32)
        o_ref[0, r, :] = x + (1.0 + mod[5:6]) * _rms(y, n2_ref[...])


def _ffn(x, mod, n1, wi, wo, n2, *, tm):
    b, t, d = x.shape
    hidden = wo.shape[0]
    tok = pl.BlockSpec((1, tm, d), lambda i, j: (i, j, 0))
    small = lambda a: pl.BlockSpec(a.shape, lambda i, j: (0,) * a.ndim)
    return pl.pallas_call(
        functools.partial(_ffn_kernel, hidden=hidden),
        grid=(b, t // tm),
        in_specs=[tok, pl.BlockSpec((1, 6, d), lambda i, j: (i, 0, 0)), small(n1), _const_spec(wi.shape),
                  _const_spec(wo.shape), small(n2)],
        out_specs=tok,
        out_shape=jax.ShapeDtypeStruct((b, t, d), F32),
        compiler_params=_params(("parallel", "parallel")),
        name="ffn",
    )(x, mod, n1, wi, wo, n2)


def _regroup_w_in(w_in, d, s5w):
    sizes = (HK, HK, HV, RANK, HV, HK, HK, HV, HEADS, HEADS, HV, s5w, 3 * d)
    offs, a = [], 0
    for s in sizes:
        offs.append(a)
        a += s
    col = lambda i: w_in[:, offs[i]:offs[i] + sizes[i]]
    zeros = lambda n: jnp.zeros((w_in.shape[0], n), w_in.dtype)
    small_a = jnp.concatenate([col(3), col(8), zeros(LANES - RANK - HEADS)], axis=1)
    small_b = jnp.concatenate([zeros(GATE_LANE), col(9), zeros(LANES - GATE_LANE - HEADS)], axis=1)
    return jnp.concatenate([col(0), col(1), col(2), col(4), col(5), col(6), col(7), col(10), col(11),
                            small_a, small_b, col(12)], axis=1).astype(BF16)


def _lane_pad(vec, lane0):
    return jnp.zeros((1, LANES), F32).at[0, lane0:lane0 + vec.shape[0]].set(vec)


def kernel(x, c, ada_w, ada_b, pre1_w, post1_w, pre2_w, post2_w, w_in, gla_a2, gla_a_b, gla_norm_w, ml_conv_w, ml_conv_b, ml_i_b, ml_f_b, s5_lam_re, s5_lam_im, s5_log_step, s5_b_re, s5_b_im, s5_c_re, s5_c_im, s5_d, s5_glu_w, s5_glu_b, proj_gla, proj_ml, proj_s5, branch_gate_b, w_out, ffn_w_in, ffn_w_out):
    b, t, d = x.shape
    depth = ada_w.shape[0]
    s5w = s5_d.shape[1]
    assert RANK + HEADS <= GATE_LANE + HEADS <= LANES and GATE_LANE == RANK
    assert t % 256 == 0 and d % LANES == 0
    tm = 512 if t % 512 == 0 else 256
    tt = 512 if t % 512 == 0 else 256
    tt_s5 = 2048 if t % 2048 == 0 else t
    n_pow = max(1, (tt_s5 // S5_BLOCK - 1).bit_length())

    mod_all = _ada_mod(c, ada_w, ada_b)
    row = lambda a: a.reshape(1, -1)
    for l in range(depth):
        mod = mod_all[l].reshape(b, 6, d)
        w1 = _regroup_w_in(w_in[l], d, s5w)
        zg, zm, zs5, zsm, zgate = _inproj(x, mod, row(pre1_w[l]), w1, tm=tm, s5w=s5w, ngate=3 * d)

        a2p = jnp.zeros((LANES, HK), BF16).at[0:RANK].set(gla_a2[l].astype(BF16))
        y_gla, y_ml = _mixer(zg, zm, zsm, a2p, row(gla_a_b[l]), row(gla_norm_w[l]), ml_conv_w[l],
                             row(ml_conv_b[l]), _lane_pad(ml_i_b[l], GATE_LANE),
                             _lane_pad(ml_f_b[l], GATE_LANE), tt=tt)

        tmat, wmat, vmat, pa, pb = _s5_prep(s5_lam_re[l], s5_lam_im[l], s5_log_step[l], s5_b_re[l], s5_b_im[l],
                                            s5_c_re[l], s5_c_im[l], s5_d[l], n_pow=n_pow)
        lanes_of = lambda a: a.transpose(1, 0, 2).reshape(n_pow, -1)
        ys = _s5(zs5, tmat, wmat, vmat, lanes_of(pa), lanes_of(pb), tt=tt_s5)

        x = _merge(x, mod, y_gla, y_ml, ys, zgate, s5_glu_w[l].astype(BF16), row(s5_glu_b[l]),
                   proj_gla[l].astype(BF16), proj_ml[l].astype(BF16), proj_s5[l].astype(BF16),
                   row(branch_gate_b[l]).astype(BF16), w_out[l].astype(BF16), row(post1_w[l]), tm=tm)
        x = _ffn(x, mod, row(pre2_w[l]), ffn_w_in[l].astype(BF16), ffn_w_out[l].astype(BF16),
                 row(post2_w[l]), tm=tm)
    return x
```

```python
import functools
import math

import jax
import jax.numpy as jnp
from jax import lax
from jax.experimental import pallas as pl
from jax.experimental.pallas import tpu as pltpu

HEADS = 4
DK = 64
DV = 128
RANK = 16
GATE_NORM = 16.0
CONV = 4
S5_GROUP = 16
S5_STATE = 64
CHUNK = 64
S5_BLOCK = 16
RMS_EPS = 1e-6
LANES = 128
GATE_LANE = 16
SUBTILES = 2
VMEM_LIMIT = 56 * 1024 * 1024

HK = HEADS * DK
HV = HEADS * DV
F32 = jnp.float32
BF16 = jnp.bfloat16
HIGHEST = lax.Precision.HIGHEST
NT = (((1,), (1,)), ((), ()))
TN = (((0,), (0,)), ((), ()))


def _params(sem):
    return pltpu.CompilerParams(dimension_semantics=sem, vmem_limit_bytes=VMEM_LIMIT)


def _const_spec(shape):
    nd = len(shape)
    return pl.BlockSpec(shape, lambda *_: (0,) * nd, pipeline_mode=pl.Buffered(1))


def _sigmoid(x):
    return 0.5 * jnp.tanh(0.5 * x) + 0.5


def _silu(x):
    return x * _sigmoid(x)


def _log_sigmoid(x):
    return jnp.minimum(x, 0.0) - jnp.log(1.0 + jnp.exp(-jnp.abs(x)))


def _gelu_tanh(x):
    c = math.sqrt(2.0 / math.pi)
    return 0.5 * x * (1.0 + jnp.tanh(c * (x + 0.044715 * (x * x * x))))


def _rms(x, w):
    ms = jnp.mean(x * x, axis=-1, keepdims=True)
    return x * lax.rsqrt(ms + RMS_EPS) * w


def _ada_kernel(c_ref, w_ref, b_ref, o_ref):
    ca = _silu(c_ref[...]).astype(BF16)
    o_ref[0] = jnp.dot(ca, w_ref[0].astype(BF16), preferred_element_type=F32) + b_ref[0]


def _ada_mod(c, ada_w, ada_b):
    nl, d, n = ada_w.shape
    b = c.shape[0]
    tn = 1024
    return pl.pallas_call(
        _ada_kernel,
        grid=(nl, n // tn),
        in_specs=[pl.BlockSpec((b, d), lambda l, j: (0, 0)),
                  pl.BlockSpec((1, d, tn), lambda l, j: (l, 0, j)),
                  pl.BlockSpec((1, 1, tn), lambda l, j: (l, 0, j))],
        out_specs=pl.BlockSpec((1, b, tn), lambda l, j: (l, 0, j)),
        out_shape=jax.ShapeDtypeStruct((nl, b, n), F32),
        compiler_params=_params(("arbitrary", "arbitrary")),
        name="ada_mod",
    )(c, ada_w, ada_b.reshape(nl, 1, n))


N_GLA = 2 * HK + 2 * HV
N_ML = 2 * HK + 2 * HV
N_SMALL = 2 * LANES


def _inproj_kernel(x_ref, mod_ref, nw_ref, w_ref, zg_ref, zm_ref, zs_ref, zsm_ref, *, offs):
    x = x_ref[0]
    mod = mod_ref[0]
    h = _rms(x, nw_ref[...]) * (1.0 + mod[1:2]) + mod[0:1]
    hb = h.astype(BF16)
    for ref, (a, b) in zip((zg_ref, zm_ref, zs_ref, zsm_ref), offs):
        res = jnp.dot(hb, w_ref[:, a:b], preferred_element_type=F32).astype(ref.dtype)
        if ref is zs_ref:
            for s in range(ref.shape[1]):
                ref[0, s] = res[:, s * LANES:(s + 1) * LANES]
        else:
            ref[0] = res


def _inproj(x, mod, nw, w, *, tm, s5w):
    b, t, d = x.shape
    sizes = (N_GLA, N_ML, s5w, N_SMALL)
    offs, a = [], 0
    for s in sizes:
        offs.append((a, a + s))
        a += s
    dts = (BF16, BF16, F32, F32)
    tok = lambda n: pl.BlockSpec((1, tm, n), lambda i, j: (i, j, 0))
    slabs = s5w // LANES
    out_specs = [tok(s) for s in sizes]
    out_shape = [jax.ShapeDtypeStruct((b, t, s), dt) for s, dt in zip(sizes, dts)]
    out_specs[2] = pl.BlockSpec((1, slabs, tm, LANES), lambda i, j: (i, 0, j, 0))
    out_shape[2] = jax.ShapeDtypeStruct((b, slabs, t, LANES), F32)
    return pl.pallas_call(
        functools.partial(_inproj_kernel, offs=tuple(offs)),
        grid=(b, t // tm),
        in_specs=[tok(d),
                  pl.BlockSpec((1, 6, d), lambda i, j: (i, 0, 0)),
                  pl.BlockSpec((1, d), lambda i, j: (0, 0)),
                  _const_spec(w.shape)],
        out_specs=out_specs,
        out_shape=out_shape,
        compiler_params=_params(("parallel", "parallel")),
        name="in_proj",
    )(x, mod, nw, w)


def _block_diag_rows(a, width):
    lane_head = lax.broadcasted_iota(jnp.int32, a.shape, 1) // width
    zero = jnp.zeros_like(a)
    return jnp.concatenate([jnp.where(lane_head == h, a, zero) for h in range(HEADS)], axis=0)


def _head_mask(rows_per_head, cols_per_head):
    r = lax.broadcasted_iota(jnp.int32, (HEADS * rows_per_head, HEADS * cols_per_head), 0) // rows_per_head
    c = lax.broadcasted_iota(jnp.int32, (HEADS * rows_per_head, HEADS * cols_per_head), 1) // cols_per_head
    return r == c


def _scan_rows(x, period, op, fill):
    pos = lax.broadcasted_iota(jnp.int32, x.shape, 0) % period
    sh = 1
    while sh < period:
        x = op(x, jnp.where(pos >= sh, pltpu.roll(x, sh, 0), fill))
        sh *= 2
    return x


def _bcast_heads(tile, per_head):
    rows = tile.shape[0]
    full = [jnp.broadcast_to(tile[:, GATE_LANE + h:GATE_LANE + h + 1], (rows, LANES)) for h in range(HEADS)]
    if per_head == LANES:
        return jnp.concatenate(full, axis=1)
    assert 2 * per_head == LANES and HEADS % 2 == 0
    low = lax.broadcasted_iota(jnp.int32, (rows, LANES), 1) < per_head
    return jnp.concatenate([jnp.where(low, full[h], full[h + 1]) for h in range(0, HEADS, 2)], axis=1)


def _gla_levels(q, k, g, c):
    n = q.shape[0]
    row = lax.broadcasted_iota(jnp.int32, q.shape, 0)
    out = []
    hs = 1
    while 2 * hs <= n:
        blk = 2 * hs
        pos = row % blk
        upper = pos >= hs
        if hs == 1:
            x = jnp.where(upper, g, 0.0)
        elif hs == 2:
            g_prev = pltpu.roll(g, 1, 0)
            g_next = pltpu.roll(g, n - 1, 0)
            x = jnp.where(pos == 2, g, jnp.where(pos == 3, g + g_prev, jnp.where(pos == 0, g_next, 0.0)))
        else:
            c3 = c.reshape(n // blk, blk, c.shape[1])
            anchor = jnp.broadcast_to(c3[:, hs - 1:hs, :], c3.shape).reshape(c.shape)
            x = -jnp.abs(c - anchor)
        out.append((jnp.where(upper, q, k) * jnp.exp(x), hs))
        hs = blk
    return out


def _pair_block_diag(a):
    low = lax.broadcasted_iota(jnp.int32, a.shape, 1) < a.shape[1] // 2
    zero = jnp.zeros_like(a)
    return jnp.concatenate([jnp.where(low, a, zero), jnp.where(low, zero, a)], axis=0)


def _update_state(st_ref, decay_row, upd):
    for h in range(HEADS):
        rs = slice(h * DV, (h + 1) * DV)
        c0 = (h * DK) // LANES * LANES
        cs = slice(c0, c0 + LANES)
        own = (lax.broadcasted_iota(jnp.int32, (DV, LANES), 1) + c0) // DK == h
        st_ref[rs, cs] = decay_row[:, cs] * st_ref[rs, cs] + jnp.where(own, upd[rs, cs], 0.0)


def _gla_chunk(q, k, v, gr, g, c, nw, st_ref):
    L = q.shape[0]
    col = lax.broadcasted_iota(jnp.int32, (L, LANES), 1) % DK
    rowi = lax.broadcasted_iota(jnp.int32, (L, LANES), 0)
    qb, kb = q.astype(BF16), k.astype(BF16)
    levels = [(t.astype(BF16), hs) for t, hs in _gla_levels(q, k, g, c)]
    pairs = []
    for p in range(HK // LANES):
        ls = slice(p * LANES, (p + 1) * LANES)
        s = lax.dot_general(qb[:, ls], _pair_block_diag(kb[:, ls]), NT, preferred_element_type=F32)
        acc = jnp.where(rowi == col, s, 0.0)
        for tb, hs in levels:
            s = lax.dot_general(tb[:, ls], _pair_block_diag(tb[:, ls]), NT, preferred_element_type=F32)
            blk = 2 * hs
            valid = ((rowi // blk) == (col // blk)) & (rowi % blk >= hs) & (col % blk < hs)
            acc = acc + jnp.where(valid, s, 0.0)
        pairs.append(acc)
    scores = jnp.concatenate(pairs, axis=1)

    st = st_ref[...]
    o = jnp.dot(scores.astype(BF16), _block_diag_rows(v, DV), preferred_element_type=F32)
    o = o + lax.dot_general((q * jnp.exp(c)).astype(BF16), st.astype(BF16), NT,
                            preferred_element_type=F32)
    c_last = c[L - 1:L]
    kd = (k * jnp.exp(c_last - c)).astype(BF16)
    upd = lax.dot_general(v, kd, TN, preferred_element_type=F32)
    _update_state(st_ref, jnp.exp(c_last), upd)

    parts = []
    for h in range(HEADS):
        oh = o[:, h * DV:(h + 1) * DV]
        parts.append(oh * lax.rsqrt(jnp.mean(oh * oh, axis=-1, keepdims=True) + RMS_EPS))
    return jnp.concatenate(parts, axis=1) * nw * _silu(gr)


def _mlstm_chunk(q, k, v, og, u, bcum, cmax, mt_ref, nrm_ref, m_ref):
    L = q.shape[0]
    lane = lax.broadcasted_iota(jnp.int32, (L, LANES), 1)
    gate_lanes = (lane >= GATE_LANE) & (lane < GATE_LANE + HEADS)
    m_prev = m_ref[...]
    big_m = jnp.maximum(m_prev, cmax)
    m_t = bcum + big_m
    m_new = m_t[L - 1:L]
    b_last = bcum[L - 1:L]

    ut = jnp.concatenate([u, u], axis=0).T
    low = lax.broadcasted_iota(jnp.int32, (1, 2 * L), 1) < L
    urow = jnp.concatenate([jnp.where(low, ut[GATE_LANE + h:GATE_LANE + h + 1], ut[GATE_LANE + h + 1:GATE_LANE + h + 2])
                            for h in range(0, HEADS, 2)], axis=1)
    causal = (lax.broadcasted_iota(jnp.int32, (L, HK), 1) % DK) <= lax.broadcasted_iota(jnp.int32, (L, HK), 0)
    dmat = _bcast_heads(-big_m, DK) + urow
    pmat = jnp.where(causal, jnp.exp(jnp.where(causal, dmat, 0.0)), 0.0)

    qb = q.astype(BF16)
    qk_s = lax.dot_general(qb, _block_diag_rows(k.astype(BF16), DK), NT, preferred_element_type=F32)
    sm = (qk_s * pmat).astype(BF16)

    mt = mt_ref[...]
    nrm = nrm_ref[...]
    den_row = lax.broadcasted_iota(jnp.int32, (LANES, HK), 0)
    den_head = lax.broadcasted_iota(jnp.int32, (LANES, HK), 1) // DK + GATE_LANE
    nrm_rows = jnp.where(den_row == den_head, jnp.broadcast_to(nrm, (LANES, HK)), 0.0)
    mem_ext = jnp.concatenate([mt, nrm_rows], axis=0).astype(BF16)
    inter = lax.dot_general(qb, mem_ext, NT, preferred_element_type=F32)
    ones_den = (lax.broadcasted_iota(jnp.int32, (HK, LANES), 0) // DK + GATE_LANE
                == lax.broadcasted_iota(jnp.int32, (HK, LANES), 1)).astype(BF16)
    v_ext = jnp.concatenate([_block_diag_rows(v, DV), ones_den], axis=1)
    intra = jnp.dot(sm, v_ext, preferred_element_type=F32)

    lw = jnp.where(gate_lanes, m_prev - big_m, 0.0)
    den = jnp.exp(lw) * inter[:, HV:] + intra[:, HV:]
    rden = 1.0 / jnp.maximum(jnp.abs(den), jnp.exp(-m_t))
    hout = (jnp.exp(_bcast_heads(lw, DV)) * inter[:, 0:HV] + intra[:, 0:HV]) * _bcast_heads(rden, DV)

    wk = jnp.where(gate_lanes, b_last - m_new + u, 0.0)
    cd = jnp.where(gate_lanes[0:1], b_last + m_prev - m_new, 0.0)
    kw = k * jnp.exp(_bcast_heads(wk, DK))
    decay = jnp.exp(_bcast_heads(cd, DK))
    upd = lax.dot_general(v, kw.astype(BF16), TN, preferred_element_type=F32)
    _update_state(mt_ref, decay, upd)
    nrm_ref[...] = decay * nrm + jnp.sum(kw, axis=0, keepdims=True)
    m_ref[...] = m_new
    return _sigmoid(og) * hout


def _mixers(zg_ref, zm_ref, zs_ref, a2_ref, ab_ref, gnw_ref, cw_ref, cb_ref, ib_ref, fb_ref,
            yg_ref, ym_ref, st_ref, mt_ref, nrm_ref, m_ref, ext_ref, n_chunks):
    L = CHUNK
    tt = n_chunks * L

    zs_a = zs_ref[:, 0:LANES]
    xg = jnp.dot(zs_a.astype(BF16), a2_ref[...], preferred_element_type=F32) + ab_ref[...]
    g = _log_sigmoid(xg) * (1.0 / GATE_NORM)
    c = _scan_rows(g, L, jnp.add, 0.0)

    qk_in = zm_ref[:, 0:2 * HK].astype(F32)
    ext_ref[8:8 + tt, :] = qk_in
    acc = jnp.zeros((tt, 2 * HK), F32) + cb_ref[...]
    for j in range(CONV):
        off = 8 - (CONV - 1) + j
        acc = acc + cw_ref[j:j + 1, :] * ext_ref[off:off + tt, :]
    ext_ref[0:8, :] = qk_in[tt - 8:tt]
    qk = _silu(acc)

    lane = lax.broadcasted_iota(jnp.int32, (tt, LANES), 1)
    gate_lanes = (lane >= GATE_LANE) & (lane < GATE_LANE + HEADS)
    ig = jnp.where(gate_lanes, zs_a + ib_ref[...], 0.0)
    lf = jnp.where(gate_lanes, _log_sigmoid(zs_ref[:, LANES:2 * LANES] + fb_ref[...]), 0.0)
    bcum = _scan_rows(lf, L, jnp.add, 0.0)
    u = ig - bcum
    cmax = _scan_rows(u, L, jnp.maximum, -jnp.inf)

    for ci in range(n_chunks):
        r = slice(ci * L, (ci + 1) * L)
        yg = _gla_chunk(zg_ref[r, 0:HK].astype(F32) * (DK ** -0.5), zg_ref[r, HK:2 * HK].astype(F32),
                        zg_ref[r, 2 * HK:2 * HK + HV], zg_ref[r, 2 * HK + HV:2 * HK + 2 * HV].astype(F32),
                        g[r], c[r], gnw_ref[...], st_ref)
        yg_ref[r, :] = yg.astype(yg_ref.dtype)
        ym = _mlstm_chunk(qk[r, 0:HK], qk[r, HK:2 * HK] * (DK ** -0.5), zm_ref[r, 2 * HK:2 * HK + HV],
                          zm_ref[r, 2 * HK + HV:2 * HK + 2 * HV].astype(F32), u[r], bcum[r], cmax[r],
                          mt_ref, nrm_ref, m_ref)
        ym_ref[r, :] = ym.astype(ym_ref.dtype)


def _mixer_kernel(zg_ref, zm_ref, zs_ref, a2_ref, ab_ref, gnw_ref, cw_ref, cb_ref, ib_ref, fb_ref,
                  yg_ref, ym_ref, st_ref, mt_ref, nrm_ref, m_ref, ext_ref, *, n_chunks):
    @pl.when(pl.program_id(1) == 0)
    def _():
        for ref in (st_ref, mt_ref, nrm_ref, m_ref):
            ref[...] = jnp.zeros_like(ref)
        ext_ref[0:8, :] = jnp.zeros((8, 2 * HK), F32)

    _mixers(zg_ref.at[0], zm_ref.at[0], zs_ref.at[0], a2_ref, ab_ref, gnw_ref, cw_ref, cb_ref, ib_ref, fb_ref,
            yg_ref.at[0], ym_ref.at[0], st_ref, mt_ref, nrm_ref, m_ref, ext_ref, n_chunks)


def _mixer(zg, zm, zs, a2p, ab, gnw, cw, cb, ibp, fbp, *, tt):
    b, t, _ = zg.shape
    tok = lambda n: pl.BlockSpec((1, tt, n), lambda i, j: (i, j, 0))
    small = lambda a: pl.BlockSpec(a.shape, lambda i, j: (0,) * a.ndim)
    return pl.pallas_call(
        functools.partial(_mixer_kernel, n_chunks=tt // CHUNK),
        grid=(b, t // tt),
        in_specs=[tok(N_GLA), tok(N_ML), tok(N_SMALL)] + [small(a) for a in (a2p, ab, gnw, cw, cb, ibp, fbp)],
        out_specs=[tok(HV), tok(HV)],
        out_shape=[jax.ShapeDtypeStruct((b, t, HV), BF16)] * 2,
        scratch_shapes=[pltpu.VMEM((HV, HK), F32), pltpu.VMEM((HV, HK), F32), pltpu.VMEM((1, HK), F32),
                        pltpu.VMEM((1, LANES), F32), pltpu.VMEM((8 + tt, 2 * HK), F32)],
        compiler_params=_params(("parallel", "arbitrary")),
        name="mixer",
    )(zg, zm, zs, a2p, ab, gnw, cw, cb, ibp, fbp)


def _s5_prep_kernel(lr_ref, li_ref, lrc_ref, lic_ref, dt_ref, btr_ref, bti_ref, ctr_ref, cti_ref, d_ref,
                    t_ref, w_ref, v_ref, pa_ref, pb_ref, *, n_pow):
    P = S5_STATE
    nb = S5_BLOCK
    dt = dt_ref[0]
    lr = jnp.minimum(lr_ref[0], -1e-4)
    li = li_ref[0]
    lrc = jnp.minimum(lrc_ref[0], -1e-4)
    lic = lic_ref[0]
    dtr = jnp.exp(dt)

    def power(lr_, li_, n):
        mag = jnp.exp(lr_ * dtr * n)
        ang = li_ * dtr * n
        return mag * jnp.cos(ang), mag * jnp.sin(ang)

    n_rows = jnp.where(lax.broadcasted_iota(jnp.int32, (nb + 8, 1), 0) < nb,
                       nb - 1 - lax.broadcasted_iota(jnp.int32, (nb + 8, 1), 0),
                       jnp.where(lax.broadcasted_iota(jnp.int32, (nb + 8, 1), 0) == nb, 1, nb)).astype(F32)
    row_r, row_i = power(lr, li, n_rows)
    col_r, col_i = power(lrc, lic, lax.broadcasted_iota(jnp.int32, (P, LANES), 1).astype(F32))

    ar1, ai1 = row_r[nb:nb + 1], row_i[nb:nb + 1]
    inv = 1.0 / (lr * lr + li * li)
    zr = ((ar1 - 1.0) * lr + ai1 * li) * inv
    zi = (ai1 * lr - (ar1 - 1.0) * li) * inv
    bbr = zr * btr_ref[0] - zi * bti_ref[0]
    bbi = zr * bti_ref[0] + zi * btr_ref[0]

    ctr, cti = ctr_ref[0], cti_ref[0]
    src = lax.broadcasted_iota(jnp.int32, (LANES, nb * S5_GROUP), 0)
    dst_t = lax.broadcasted_iota(jnp.int32, (LANES, nb * S5_GROUP), 1) // S5_GROUP

    def out_map(shift):
        expand = (src == dst_t + shift).astype(F32)
        ar = jnp.dot(col_r, expand, precision=HIGHEST, preferred_element_type=F32)
        ai = jnp.dot(col_i, expand, precision=HIGHEST, preferred_element_type=F32)
        return jnp.concatenate([ctr * ar - cti * ai, -(ctr * ai + cti * ar)], axis=0)

    v_ref[0] = out_map(1).T.astype(v_ref.dtype)
    base = jnp.dot(jnp.concatenate([bbr, bbi], axis=1), out_map(0), precision=HIGHEST,
                   preferred_element_type=F32)
    lane = lax.broadcasted_iota(jnp.int32, base.shape, 1)
    base = base + jnp.where(lane == lax.broadcasted_iota(jnp.int32, base.shape, 0), d_ref[0], 0.0)
    rows = [base]
    for s in range(1, nb):
        rows.append(jnp.where(lane >= s * S5_GROUP, pltpu.roll(base, s * S5_GROUP, 1), 0.0))
    t_ref[0] = jnp.concatenate(rows, axis=0).T.astype(t_ref.dtype)

    pick = (lax.broadcasted_iota(jnp.int32, (nb * S5_GROUP, nb + 8), 0) // S5_GROUP
            == lax.broadcasted_iota(jnp.int32, (nb * S5_GROUP, nb + 8), 1)).astype(F32)
    rep = lambda a: jnp.dot(pick, a, precision=HIGHEST, preferred_element_type=F32)
    ar, ai = rep(row_r), rep(row_i)
    bbr_t = jnp.concatenate([bbr] * nb, axis=0)
    bbi_t = jnp.concatenate([bbi] * nb, axis=0)
    wt = jnp.concatenate([ar * bbr_t - ai * bbi_t, ar * bbi_t + ai * bbr_t], axis=1).T
    w_ref[0] = jnp.concatenate([wt, wt[P:2 * P], wt[0:P]], axis=0).astype(w_ref.dtype)

    pr, pi = row_r[nb + 1:nb + 2], row_i[nb + 1:nb + 2]
    pa, pb = [], []
    for _ in range(n_pow):
        pa.append(jnp.concatenate([pr, pr], axis=1))
        pb.append(jnp.concatenate([-pi, pi], axis=1))
        pr, pi = pr * pr - pi * pi, 2.0 * pr * pi
    pa_ref[0] = jnp.concatenate(pa, axis=0)
    pb_ref[0] = jnp.concatenate(pb, axis=0)


def _s5_prep(lam_re, lam_im, log_step, b_re, b_im, c_re, c_im, d_skip, *, n_pow):
    g, p = lam_re.shape
    nbg = S5_BLOCK * S5_GROUP
    bt = lambda a: jnp.swapaxes(a, 1, 2)
    ct = lambda a: jnp.tile(jnp.swapaxes(a, 1, 2), (1, 1, S5_BLOCK))
    row = lambda a: a.reshape(g, 1, p)
    colv = lambda a: a.reshape(g, p, 1)
    dtile = jnp.tile(d_skip.reshape(g, 1, S5_GROUP), (1, 1, S5_BLOCK))
    spec = lambda *s: pl.BlockSpec((1,) + s, lambda i: (i,) + (0,) * len(s))
    return pl.pallas_call(
        functools.partial(_s5_prep_kernel, n_pow=n_pow),
        grid=(g,),
        in_specs=[spec(1, p), spec(1, p), spec(p, 1), spec(p, 1), spec(1, 1),
                  spec(S5_GROUP, p), spec(S5_GROUP, p), spec(p, nbg), spec(p, nbg), spec(1, nbg)],
        out_specs=[spec(nbg, nbg), spec(4 * p, nbg), spec(nbg, 2 * p), spec(n_pow, 2 * p), spec(n_pow, 2 * p)],
        out_shape=[jax.ShapeDtypeStruct((g, nbg, nbg), BF16), jax.ShapeDtypeStruct((g, 4 * p, nbg), BF16),
                   jax.ShapeDtypeStruct((g, nbg, 2 * p), BF16), jax.ShapeDtypeStruct((g, n_pow, 2 * p), F32),
                   jax.ShapeDtypeStruct((g, n_pow, 2 * p), F32)],
        compiler_params=_params(("parallel",)),
        name="s5_prep",
    )(row(lam_re), row(lam_im), colv(lam_re), colv(lam_im), log_step.reshape(g, 1, 1),
      bt(b_re), bt(b_im), ct(c_re), ct(c_im), dtile)


S5_SCAN_GROUPS = 4


def _s5_kernel(u_ref, t_ref, w_ref, v_ref, pa_ref, pb_ref, y_ref,
               ut_ref, yt_ref, zt_ref, zw_ref, xin_ref, xc_ref, *, groups, n_pow):
    nb = S5_BLOCK
    p2 = 2 * S5_STATE
    slabs, tt = u_ref.shape[1], u_ref.shape[2]
    nblk = tt // nb

    @pl.when(pl.program_id(1) == 0)
    def _():
        xc_ref[...] = jnp.zeros_like(xc_ref)

    gps = S5_SCAN_GROUPS
    n_chunks = groups // gps
    chunks_per_slab = LANES // S5_GROUP // gps

    def relayout_in(s):
        for tl in range(nb):
            ut_ref[tl, s * LANES:(s + 1) * LANES, :] = (
                u_ref[0, s, pl.ds(tl, nblk, stride=nb), :].T.astype(ut_ref.dtype))

    def relayout_out(s):
        for tl in range(nb):
            y_ref[0, s, pl.ds(tl, nblk, stride=nb), :] = yt_ref[tl, s * LANES:(s + 1) * LANES, :].T

    def group_input(g):
        return ut_ref[:, g * S5_GROUP:(g + 1) * S5_GROUP, :].reshape(nb * S5_GROUP, nblk)

    def state_in(ci):
        for g in range(ci * gps, (ci + 1) * gps):
            z2 = jnp.dot(w_ref[g], group_input(g), preferred_element_type=F32)
            zt_ref[g * p2:(g + 1) * p2, :] = z2[0:p2]
            zw_ref[g * p2:(g + 1) * p2, :] = z2[p2:2 * p2]

    width = gps * p2
    row = lax.broadcasted_iota(jnp.int32, (nblk, width), 0)

    def scan(ci):
        l0 = ci * width
        x = zt_ref[pl.ds(l0, width), :].T
        xw = zw_ref[pl.ds(l0, width), :].T
        pa = pa_ref[:, pl.ds(l0, width)]
        pb = pb_ref[:, pl.ds(l0, width)]
        xc = xc_ref[0:1, pl.ds(l0, width)]
        xcw = xc_ref[1:2, pl.ds(l0, width)]
        first = row == 0
        x = x + jnp.where(first, xc * pa[0:1] + xcw * pb[0:1], 0.0)
        xw = xw + jnp.where(first, xcw * pa[0:1] - xc * pb[0:1], 0.0)
        for k in range(n_pow):
            sh = 1 << k
            if sh >= nblk:
                break
            xs, xws, m = pltpu.roll(x, sh, 0), pltpu.roll(xw, sh, 0), row >= sh
            x, xw = (x + jnp.where(m, xs * pa[k:k + 1] + xws * pb[k:k + 1], 0.0),
                     xw + jnp.where(m, xws * pa[k:k + 1] - xs * pb[k:k + 1], 0.0))
        x_in = jnp.where(row >= 1, pltpu.roll(x, 1, 0), xc)
        xc_ref[0:1, pl.ds(l0, width)] = x[nblk - 1:nblk]
        xc_ref[1:2, pl.ds(l0, width)] = xw[nblk - 1:nblk]
        xin_ref[pl.ds(l0, width), :] = x_in.T.astype(xin_ref.dtype)

    def block_out(ci):
        for g in range(ci * gps, (ci + 1) * gps):
            yt = jnp.dot(t_ref[g], group_input(g), preferred_element_type=F32)
            yt = yt + jnp.dot(v_ref[g], xin_ref[g * p2:(g + 1) * p2, :], preferred_element_type=F32)
            yt_ref[:, g * S5_GROUP:(g + 1) * S5_GROUP, :] = yt.reshape(nb, S5_GROUP, nblk)

    relayout_in(0)
    for step in range(n_chunks + 2):
        nxt = step // chunks_per_slab + 1
        if step % chunks_per_slab == 0 and nxt < slabs:
            relayout_in(nxt)
        if step < n_chunks:
            state_in(step)
        if 1 <= step <= n_chunks:
            scan(step - 1)
        if step >= 2:
            block_out(step - 2)
            if (step - 2) % chunks_per_slab == chunks_per_slab - 1:
                relayout_out((step - 2) // chunks_per_slab)


def _s5(u, tmat, wmat, vmat, pa, pb, *, tt):
    b, slabs, t, _ = u.shape
    width = slabs * LANES
    groups = width // S5_GROUP
    assert groups % S5_SCAN_GROUPS == 0
    n_pow = pa.shape[0]
    nblk = tt // S5_BLOCK
    states = groups * 2 * S5_STATE
    tok = pl.BlockSpec((1, slabs, tt, LANES), lambda i, j: (i, 0, j, 0))
    return pl.pallas_call(
        functools.partial(_s5_kernel, groups=groups, n_pow=n_pow),
        grid=(b, t // tt),
        in_specs=[tok, _const_spec(tmat.shape), _const_spec(wmat.shape), _const_spec(vmat.shape),
                  _const_spec(pa.shape), _const_spec(pb.shape)],
        out_specs=tok,
        out_shape=jax.ShapeDtypeStruct((b, slabs, t, LANES), F32),
        scratch_shapes=[pltpu.VMEM((S5_BLOCK, width, nblk), BF16), pltpu.VMEM((S5_BLOCK, width, nblk), F32),
                        pltpu.VMEM((states, nblk), F32), pltpu.VMEM((states, nblk), F32),
                        pltpu.VMEM((states, nblk), BF16), pltpu.VMEM((8, states), F32)],
        compiler_params=_params(("parallel", "arbitrary")),
        name="s5",
    )(u, tmat, wmat, vmat, pa, pb)


def _merge_kernel(x_ref, mod_ref, n1_ref, wg_ref, yg_ref, ym_ref, ys_ref, gluw_ref, glub_ref, pg_ref, pm_ref,
                  ps_ref, gb_ref, wo_ref, nw_ref, o_ref, *, d):
    mod = mod_ref[0]
    rows = x_ref.shape[1] // SUBTILES
    for sub in range(SUBTILES):
        r = slice(sub * rows, (sub + 1) * rows)
        x = x_ref[0, r, :]
        hb = (_rms(x, n1_ref[...]) * (1.0 + mod[1:2]) + mod[0:1]).astype(BF16)
        s = _gelu_tanh(jnp.concatenate([ys_ref[0, k, r, :] for k in range(ys_ref.shape[1])], axis=1))
        s = s * _sigmoid(jnp.dot(s.astype(BF16), gluw_ref[...], preferred_element_type=F32) + glub_ref[...])
        merged = None
        for idx, (y, p) in enumerate(((yg_ref[0, r, :], pg_ref), (ym_ref[0, r, :], pm_ref), (s.astype(BF16), ps_ref))):
            cs = slice(idx * d, (idx + 1) * d)
            zg = jnp.dot(hb, wg_ref[:, cs], preferred_element_type=F32) + gb_ref[:, cs]
            term = (jnp.tanh(0.5 * zg) + 1.0) * jnp.dot(y, p[...], preferred_element_type=F32)
            merged = term if merged is None else merged + term
        y = jnp.dot((0.5 * merged).astype(BF16), wo_ref[...], preferred_element_type=F32)
        o_ref[0, r, :] = x + (1.0 + mod[2:3]) * _rms(y, nw_ref[...])


def _merge(x, mod, n1, wg, yg, ym, ys, gluw, glub, pg, pm, ps, gb, wo, nw, *, tm):
    b, t, d = x.shape
    tok = lambda n: pl.BlockSpec((1, tm, n), lambda i, j: (i, j, 0))
    small = lambda a: pl.BlockSpec(a.shape, lambda i, j: (0,) * a.ndim)
    return pl.pallas_call(
        functools.partial(_merge_kernel, d=d),
        grid=(b, t // tm),
        in_specs=[tok(d), pl.BlockSpec((1, 6, d), lambda i, j: (i, 0, 0)), small(n1), _const_spec(wg.shape),
                  tok(HV), tok(HV), pl.BlockSpec((1, ys.shape[1], tm, LANES), lambda i, j: (i, 0, j, 0)),
                  _const_spec(gluw.shape), small(glub), _const_spec(pg.shape), _const_spec(pm.shape),
                  _const_spec(ps.shape), small(gb), _const_spec(wo.shape), small(nw)],
        out_specs=tok(d),
        out_shape=jax.ShapeDtypeStruct((b, t, d), F32),
        compiler_params=_params(("parallel", "parallel")),
        name="merge_out",
    )(x, mod, n1, wg, yg, ym, ys, gluw, glub, pg, pm, ps, gb, wo, nw)


def _ffn_kernel(x_ref, mod_ref, n1_ref, wi_ref, wo_ref, n2_ref, o_ref, *, hidden):
    mod = mod_ref[0]
    rows = x_ref.shape[1] // SUBTILES
    for s in range(SUBTILES):
        r = slice(s * rows, (s + 1) * rows)
        x = x_ref[0, r, :]
        h = (_rms(x, n1_ref[...]) * (1.0 + mod[4:5]) + mod[3:4]).astype(BF16)
        gate = jnp.dot(h, wi_ref[:, 0:hidden], preferred_element_type=F32)
        up = jnp.dot(h, wi_ref[:, hidden:2 * hidden], preferred_element_type=F32)
        act = (_silu(gate) * up).astype(BF16)
        y = jnp.dot(act, wo_ref[...], preferred_element_type=F32)
        o_ref[0, r, :] = x + (1.0 + mod[5:6]) * _rms(y, n2_ref[...])


def _ffn(x, mod, n1, wi, wo, n2, *, tm):
    b, t, d = x.shape
    hidden = wo.shape[0]
    tok = pl.BlockSpec((1, tm, d), lambda i, j: (i, j, 0))
    small = lambda a: pl.BlockSpec(a.shape, lambda i, j: (0,) * a.ndim)
    return pl.pallas_call(
        functools.partial(_ffn_kernel, hidden=hidden),
        grid=(b, t // tm),
        in_specs=[tok, pl.BlockSpec((1, 6, d), lambda i, j: (i, 0, 0)), small(n1), _const_spec(wi.shape),
                  _const_spec(wo.shape), small(n2)],
        out_specs=tok,
        out_shape=jax.ShapeDtypeStruct((b, t, d), F32),
        compiler_params=_params(("parallel", "parallel")),
        name="ffn",
    )(x, mod, n1, wi, wo, n2)


def _regroup_w_in(w_in, d, s5w):
    sizes = (HK, HK, HV, RANK, HV, HK, HK, HV, HEADS, HEADS, HV, s5w, 3 * d)
    offs, a = [], 0
    for s in sizes:
        offs.append(a)
        a += s
    col = lambda i: w_in[:, offs[i]:offs[i] + sizes[i]]
    zeros = lambda n: jnp.zeros((w_in.shape[0], n), w_in.dtype)
    small_a = jnp.concatenate([col(3), col(8), zeros(LANES - RANK - HEADS)], axis=1)
    small_b = jnp.concatenate([zeros(GATE_LANE), col(9), zeros(LANES - GATE_LANE - HEADS)], axis=1)
    mixers = jnp.concatenate([col(0), col(1), col(2), col(4), col(5), col(6), col(7), col(10), col(11),
                              small_a, small_b], axis=1)
    return mixers.astype(BF16), col(12).astype(BF16)


def _lane_pad(vec, lane0):
    return jnp.zeros((1, LANES), F32).at[0, lane0:lane0 + vec.shape[0]].set(vec)


def kernel(x, c, ada_w, ada_b, pre1_w, post1_w, pre2_w, post2_w, w_in, gla_a2, gla_a_b, gla_norm_w, ml_conv_w, ml_conv_b, ml_i_b, ml_f_b, s5_lam_re, s5_lam_im, s5_log_step, s5_b_re, s5_b_im, s5_c_re, s5_c_im, s5_d, s5_glu_w, s5_glu_b, proj_gla, proj_ml, proj_s5, branch_gate_b, w_out, ffn_w_in, ffn_w_out):
    b, t, d = x.shape
    depth = ada_w.shape[0]
    s5w = s5_d.shape[1]
    assert RANK + HEADS <= GATE_LANE + HEADS <= LANES and GATE_LANE == RANK
    assert t % 256 == 0 and d % LANES == 0
    tm = 512 if t % 512 == 0 else 256
    tm_sub = SUBTILES * tm if t % (SUBTILES * tm) == 0 else tm
    tt = 1024 if t % 1024 == 0 else 256
    tt_s5 = 2048 if t % 2048 == 0 else t
    n_pow = max(1, (tt_s5 // S5_BLOCK - 1).bit_length())

    mod_all = _ada_mod(c, ada_w, ada_b)
    row = lambda a: a.reshape(1, -1)
    for l in range(depth):
        mod = mod_all[l].reshape(b, 6, d)
        w1, w_gate = _regroup_w_in(w_in[l], d, s5w)
        zg, zm, zs5, zsm = _inproj(x, mod, row(pre1_w[l]), w1, tm=tm_sub, s5w=s5w)

        a2p = jnp.zeros((LANES, HK), BF16).at[0:RANK].set(gla_a2[l].astype(BF16))
        y_gla, y_ml = _mixer(zg, zm, zsm, a2p, row(gla_a_b[l]), row(gla_norm_w[l]), ml_conv_w[l],
                             row(ml_conv_b[l]), _lane_pad(ml_i_b[l], GATE_LANE),
                             _lane_pad(ml_f_b[l], GATE_LANE), tt=tt)

        tmat, wmat, vmat, pa, pb = _s5_prep(s5_lam_re[l], s5_lam_im[l], s5_log_step[l], s5_b_re[l], s5_b_im[l],
                                            s5_c_re[l], s5_c_im[l], s5_d[l], n_pow=n_pow)
        lanes_of = lambda a: a.transpose(1, 0, 2).reshape(n_pow, -1)
        ys = _s5(zs5, tmat, wmat, vmat, lanes_of(pa), lanes_of(pb), tt=tt_s5)

        x = _merge(x, mod, row(pre1_w[l]), w_gate, y_gla, y_ml, ys, s5_glu_w[l].astype(BF16), row(s5_glu_b[l]),
                   proj_gla[l].astype(BF16), proj_ml[l].astype(BF16), proj_s5[l].astype(BF16),
                   row(branch_gate_b[l]), w_out[l].astype(BF16), row(post1_w[l]), tm=tm_sub)
        x = _ffn(x, mod, row(pre2_w[l]), ffn_w_in[l].astype(BF16), ffn_w_out[l].astype(BF16),
                 row(post2_w[l]), tm=tm_sub)
    return x
```

```python
import functools
import math

import jax
import jax.numpy as jnp
from jax import lax
from jax.experimental import pallas as pl
from jax.experimental.pallas import tpu as pltpu

HEADS = 4
DK = 64
DV = 128
RANK = 16
GATE_NORM = 16.0
CONV = 4
S5_GROUP = 16
S5_STATE = 64
CHUNK = 64
S5_BLOCK = 16
RMS_EPS = 1e-6
LANES = 128
GATE_LANE = 16
SUBTILES = 2
VMEM_LIMIT = 56 * 1024 * 1024

HK = HEADS * DK
HV = HEADS * DV
F32 = jnp.float32
BF16 = jnp.bfloat16
HIGHEST = lax.Precision.HIGHEST
NT = (((1,), (1,)), ((), ()))
TN = (((0,), (0,)), ((), ()))


def _params(sem):
    return pltpu.CompilerParams(dimension_semantics=sem, vmem_limit_bytes=VMEM_LIMIT)


def _layer_spec(a, layer, single_buffer=False):
    nd = a.ndim - 1
    kw = dict(pipeline_mode=pl.Buffered(1)) if single_buffer else {}
    return pl.BlockSpec((None,) + a.shape[1:], lambda *_: (layer,) + (0,) * nd, **kw)


def _mod_spec(mod, layer):
    return pl.BlockSpec((None, 1) + mod.shape[2:], lambda i, j: (layer, i, 0, 0))


def _sigmoid(x):
    return 0.5 * jnp.tanh(0.5 * x) + 0.5


def _silu(x):
    return x * _sigmoid(x)


def _log_sigmoid(x):
    return jnp.minimum(x, 0.0) - jnp.log(1.0 + jnp.exp(-jnp.abs(x)))


def _gelu_tanh(x):
    c = math.sqrt(2.0 / math.pi)
    return 0.5 * x * (1.0 + jnp.tanh(c * (x + 0.044715 * (x * x * x))))


def _rms(x, w):
    ms = jnp.mean(x * x, axis=-1, keepdims=True)
    return x * lax.rsqrt(ms + RMS_EPS) * w


def _ada_kernel(c_ref, w_ref, b_ref, o_ref):
    ca = _silu(c_ref[...]).astype(BF16)
    o_ref[0] = jnp.dot(ca, w_ref[0].astype(BF16), preferred_element_type=F32) + b_ref[0]


def _ada_mod(c, ada_w, ada_b):
    nl, d, n = ada_w.shape
    b = c.shape[0]
    tn = 1024
    return pl.pallas_call(
        _ada_kernel,
        grid=(nl, n // tn),
        in_specs=[pl.BlockSpec((b, d), lambda l, j: (0, 0)),
                  pl.BlockSpec((1, d, tn), lambda l, j: (l, 0, j)),
                  pl.BlockSpec((1, 1, tn), lambda l, j: (l, 0, j))],
        out_specs=pl.BlockSpec((1, b, tn), lambda l, j: (l, 0, j)),
        out_shape=jax.ShapeDtypeStruct((nl, b, n), F32),
        compiler_params=_params(("arbitrary", "arbitrary")),
        name="ada_mod",
    )(c, ada_w, ada_b.reshape(nl, 1, n))


N_GLA = 2 * HK + 2 * HV
N_ML = 2 * HK + 2 * HV
N_SMALL = 2 * LANES


def _inproj_kernel(x_ref, mod_ref, nw_ref, w_ref, zg_ref, zm_ref, zs_ref, zsm_ref, *, offs):
    x = x_ref[0]
    mod = mod_ref[0]
    h = _rms(x, nw_ref[...]) * (1.0 + mod[1:2]) + mod[0:1]
    hb = h.astype(BF16)
    for ref, (a, b) in zip((zg_ref, zm_ref, zs_ref, zsm_ref), offs):
        res = jnp.dot(hb, w_ref[:, a:b], preferred_element_type=F32).astype(ref.dtype)
        if ref is zs_ref:
            for s in range(ref.shape[1]):
                ref[0, s] = res[:, s * LANES:(s + 1) * LANES]
        else:
            ref[0] = res


def _inproj(x, mod, nw, w, *, layer, tm, s5w):
    b, t, d = x.shape
    sizes = (N_GLA, N_ML, s5w, N_SMALL)
    offs, a = [], 0
    for s in sizes:
        offs.append((a, a + s))
        a += s
    dts = (BF16, BF16, F32, F32)
    tok = lambda n: pl.BlockSpec((1, tm, n), lambda i, j: (i, j, 0))
    slabs = s5w // LANES
    out_specs = [tok(s) for s in sizes]
    out_shape = [jax.ShapeDtypeStruct((b, t, s), dt) for s, dt in zip(sizes, dts)]
    out_specs[2] = pl.BlockSpec((1, slabs, tm, LANES), lambda i, j: (i, 0, j, 0))
    out_shape[2] = jax.ShapeDtypeStruct((b, slabs, t, LANES), F32)
    return pl.pallas_call(
        functools.partial(_inproj_kernel, offs=tuple(offs)),
        grid=(b, t // tm),
        in_specs=[tok(d), _mod_spec(mod, layer), _layer_spec(nw, layer), _layer_spec(w, layer, True)],
        out_specs=out_specs,
        out_shape=out_shape,
        compiler_params=_params(("parallel", "parallel")),
        name="in_proj",
    )(x, mod, nw, w)


def _block_diag_rows(a, width):
    lane_head = lax.broadcasted_iota(jnp.int32, a.shape, 1) // width
    zero = jnp.zeros_like(a)
    return jnp.concatenate([jnp.where(lane_head == h, a, zero) for h in range(HEADS)], axis=0)


def _scan_rows(x, period, op, fill):
    pos = lax.broadcasted_iota(jnp.int32, x.shape, 0) % period
    sh = 1
    while sh < period:
        x = op(x, jnp.where(pos >= sh, pltpu.roll(x, sh, 0), fill))
        sh *= 2
    return x


def _bcast_heads(tile, per_head):
    rows = tile.shape[0]
    full = [jnp.broadcast_to(tile[:, GATE_LANE + h:GATE_LANE + h + 1], (rows, LANES)) for h in range(HEADS)]
    if per_head == LANES:
        return jnp.concatenate(full, axis=1)
    assert 2 * per_head == LANES and HEADS % 2 == 0
    low = lax.broadcasted_iota(jnp.int32, (rows, LANES), 1) < per_head
    return jnp.concatenate([jnp.where(low, full[h], full[h + 1]) for h in range(0, HEADS, 2)], axis=1)


def _gla_levels(q, k, g, c):
    n = q.shape[0]
    row = lax.broadcasted_iota(jnp.int32, q.shape, 0)
    out = []
    hs = 1
    while 2 * hs <= n:
        blk = 2 * hs
        pos = row % blk
        upper = pos >= hs
        if hs == 1:
            x = jnp.where(upper, g, 0.0)
        elif hs == 2:
            g_prev = pltpu.roll(g, 1, 0)
            g_next = pltpu.roll(g, n - 1, 0)
            x = jnp.where(pos == 2, g, jnp.where(pos == 3, g + g_prev, jnp.where(pos == 0, g_next, 0.0)))
        else:
            c3 = c.reshape(n // blk, blk, c.shape[1])
            anchor = jnp.broadcast_to(c3[:, hs - 1:hs, :], c3.shape).reshape(c.shape)
            x = -jnp.abs(c - anchor)
        out.append((jnp.where(upper, q, k) * jnp.exp(x), hs))
        hs = blk
    return out


def _pair_block_diag(a):
    low = lax.broadcasted_iota(jnp.int32, a.shape, 1) < a.shape[1] // 2
    zero = jnp.zeros_like(a)
    return jnp.concatenate([jnp.where(low, a, zero), jnp.where(low, zero, a)], axis=0)


def _own_block(h):
    return slice(h * DK, (h + 1) * DK), slice(h * DV, (h + 1) * DV)


def _state_operand(st_ref):
    zero = jnp.zeros((DK, DV), BF16)
    return jnp.concatenate(
        [jnp.concatenate([st_ref[_own_block(h)].astype(BF16) if j == h else zero for j in range(HEADS)], axis=1)
         for h in range(HEADS)], axis=0)


def _gla_chunk(q, k, v, gr, g, c, nw, st_ref):
    L = q.shape[0]
    col = lax.broadcasted_iota(jnp.int32, (L, LANES), 1) % DK
    rowi = lax.broadcasted_iota(jnp.int32, (L, LANES), 0)
    qb, kb = q.astype(BF16), k.astype(BF16)
    levels = [(t.astype(BF16), hs) for t, hs in _gla_levels(q, k, g, c)]
    pairs = []
    for p in range(HK // LANES):
        ls = slice(p * LANES, (p + 1) * LANES)
        s = lax.dot_general(qb[:, ls], _pair_block_diag(kb[:, ls]), NT, preferred_element_type=F32)
        acc = jnp.where(rowi == col, s, 0.0)
        for tb, hs in levels:
            s = lax.dot_general(tb[:, ls], _pair_block_diag(tb[:, ls]), NT, preferred_element_type=F32)
            blk = 2 * hs
            valid = ((rowi // blk) == (col // blk)) & (rowi % blk >= hs) & (col % blk < hs)
            acc = acc + jnp.where(valid, s, 0.0)
        pairs.append(acc)
    scores = jnp.concatenate(pairs, axis=1)

    o = jnp.dot(scores.astype(BF16), _block_diag_rows(v, DV), preferred_element_type=F32)
    o = o + jnp.dot((q * jnp.exp(c)).astype(BF16), _state_operand(st_ref), preferred_element_type=F32)
    c_last = c[L - 1:L]
    kd = (k * jnp.exp(c_last - c)).astype(BF16)
    upd = lax.dot_general(kd, v, TN, preferred_element_type=F32)
    decay = jnp.broadcast_to(jnp.exp(c_last), (LANES, HK)).T
    for h in range(HEADS):
        blk = _own_block(h)
        st_ref[blk] = decay[blk[0], :] * st_ref[blk] + upd[blk]

    parts = []
    for h in range(HEADS):
        oh = o[:, h * DV:(h + 1) * DV]
        parts.append(oh * lax.rsqrt(jnp.mean(oh * oh, axis=-1, keepdims=True) + RMS_EPS))
    return jnp.concatenate(parts, axis=1) * nw * _silu(gr)


def _mlstm_chunk(q, k, v, og, u, bcum, cmax, mt_ref, nrm_ref, m_ref):
    L = q.shape[0]
    lane = lax.broadcasted_iota(jnp.int32, (L, LANES), 1)
    gate_lanes = (lane >= GATE_LANE) & (lane < GATE_LANE + HEADS)
    m_prev = m_ref[...]
    big_m = jnp.maximum(m_prev, cmax)
    m_t = bcum + big_m
    m_new = m_t[L - 1:L]
    b_last = bcum[L - 1:L]

    ut = jnp.concatenate([u, u], axis=0).T
    low = lax.broadcasted_iota(jnp.int32, (1, 2 * L), 1) < L
    urow = jnp.concatenate([jnp.where(low, ut[GATE_LANE + h:GATE_LANE + h + 1], ut[GATE_LANE + h + 1:GATE_LANE + h + 2])
                            for h in range(0, HEADS, 2)], axis=1)
    causal = (lax.broadcasted_iota(jnp.int32, (L, HK), 1) % DK) <= lax.broadcasted_iota(jnp.int32, (L, HK), 0)
    dmat = _bcast_heads(-big_m, DK) + urow
    pmat = jnp.where(causal, jnp.exp(jnp.where(causal, dmat, 0.0)), 0.0)

    qb = q.astype(BF16)
    qk_s = lax.dot_general(qb, _block_diag_rows(k.astype(BF16), DK), NT, preferred_element_type=F32)
    sm = (qk_s * pmat).astype(BF16)

    den_lane = (lax.broadcasted_iota(jnp.int32, (HK, LANES), 0) // DK + GATE_LANE
                == lax.broadcasted_iota(jnp.int32, (HK, LANES), 1))
    nrm_cols = jnp.where(den_lane, nrm_ref[...], 0.0).astype(BF16)
    inter = jnp.dot(qb, jnp.concatenate([_state_operand(mt_ref), nrm_cols], axis=1),
                    preferred_element_type=F32)
    v_ext = jnp.concatenate([_block_diag_rows(v, DV), den_lane.astype(BF16)], axis=1)
    intra = jnp.dot(sm, v_ext, preferred_element_type=F32)

    lw = jnp.where(gate_lanes, m_prev - big_m, 0.0)
    den = jnp.exp(lw) * inter[:, HV:] + intra[:, HV:]
    rden = 1.0 / jnp.maximum(jnp.abs(den), jnp.exp(-m_t))
    hout = (jnp.exp(_bcast_heads(lw, DV)) * inter[:, 0:HV] + intra[:, 0:HV]) * _bcast_heads(rden, DV)

    wk = jnp.where(gate_lanes, b_last - m_new + u, 0.0)
    cd = jnp.where(gate_lanes[0:1], b_last + m_prev - m_new, 0.0)
    kw = (k * jnp.exp(_bcast_heads(wk, DK))).astype(BF16)
    upd = lax.dot_general(kw, jnp.concatenate([v, jnp.ones((L, LANES), BF16)], axis=1), TN,
                          preferred_element_type=F32)
    decay = jnp.exp(cd)
    for h in range(HEADS):
        blk = _own_block(h)
        dh = jnp.broadcast_to(decay[:, GATE_LANE + h:GATE_LANE + h + 1], (DK, LANES))
        mt_ref[blk] = dh * mt_ref[blk] + upd[blk]
        nrm_ref[blk[0], :] = dh * nrm_ref[blk[0], :] + upd[blk[0], HV:]
    m_ref[...] = m_new
    return _sigmoid(og) * hout


def _mixers(zg_ref, zm_ref, zs_ref, a2_ref, ab_ref, gnw_ref, cw_ref, cb_ref, ib_ref, fb_ref,
            yg_ref, ym_ref, st_ref, mt_ref, nrm_ref, m_ref, ext_ref, n_chunks):
    L = CHUNK
    tt = n_chunks * L

    zs_a = zs_ref[:, 0:LANES]
    xg = jnp.dot(zs_a.astype(BF16), a2_ref[...], preferred_element_type=F32) + ab_ref[...]
    g = _log_sigmoid(xg) * (1.0 / GATE_NORM)
    c = _scan_rows(g, L, jnp.add, 0.0)

    qk_in = zm_ref[:, 0:2 * HK].astype(F32)
    ext_ref[8:8 + tt, :] = qk_in
    acc = jnp.zeros((tt, 2 * HK), F32) + cb_ref[...]
    for j in range(CONV):
        off = 8 - (CONV - 1) + j
        acc = acc + cw_ref[j:j + 1, :] * ext_ref[off:off + tt, :]
    ext_ref[0:8, :] = qk_in[tt - 8:tt]
    qk = _silu(acc)

    lane = lax.broadcasted_iota(jnp.int32, (tt, LANES), 1)
    gate_lanes = (lane >= GATE_LANE) & (lane < GATE_LANE + HEADS)
    ig = jnp.where(gate_lanes, zs_a + ib_ref[...], 0.0)
    lf = jnp.where(gate_lanes, _log_sigmoid(zs_ref[:, LANES:2 * LANES] + fb_ref[...]), 0.0)
    bcum = _scan_rows(lf, L, jnp.add, 0.0)
    u = ig - bcum
    cmax = _scan_rows(u, L, jnp.maximum, -jnp.inf)

    for ci in range(n_chunks):
        r = slice(ci * L, (ci + 1) * L)
        yg = _gla_chunk(zg_ref[r, 0:HK].astype(F32) * (DK ** -0.5), zg_ref[r, HK:2 * HK].astype(F32),
                        zg_ref[r, 2 * HK:2 * HK + HV], zg_ref[r, 2 * HK + HV:2 * HK + 2 * HV].astype(F32),
                        g[r], c[r], gnw_ref[...], st_ref)
        yg_ref[r, :] = yg.astype(yg_ref.dtype)
        ym = _mlstm_chunk(qk[r, 0:HK], qk[r, HK:2 * HK] * (DK ** -0.5), zm_ref[r, 2 * HK:2 * HK + HV],
                          zm_ref[r, 2 * HK + HV:2 * HK + 2 * HV].astype(F32), u[r], bcum[r], cmax[r],
                          mt_ref, nrm_ref, m_ref)
        ym_ref[r, :] = ym.astype(ym_ref.dtype)


def _mixer_kernel(zg_ref, zm_ref, zs_ref, a2_ref, ab_ref, gnw_ref, cw_ref, cb_ref, ib_ref, fb_ref,
                  yg_ref, ym_ref, st_ref, mt_ref, nrm_ref, m_ref, ext_ref, *, n_chunks):
    @pl.when(pl.program_id(1) == 0)
    def _():
        for ref in (st_ref, mt_ref, nrm_ref, m_ref):
            ref[...] = jnp.zeros_like(ref)
        ext_ref[0:8, :] = jnp.zeros((8, 2 * HK), F32)

    _mixers(zg_ref.at[0], zm_ref.at[0], zs_ref.at[0], a2_ref, ab_ref, gnw_ref, cw_ref, cb_ref, ib_ref, fb_ref,
            yg_ref.at[0], ym_ref.at[0], st_ref, mt_ref, nrm_ref, m_ref, ext_ref, n_chunks)


def _mixer(zg, zm, zs, a2p, ab, gnw, cw, cb, ibp, fbp, *, layer, tt):
    b, t, _ = zg.shape
    tok = lambda n: pl.BlockSpec((1, tt, n), lambda i, j: (i, j, 0))
    return pl.pallas_call(
        functools.partial(_mixer_kernel, n_chunks=tt // CHUNK),
        grid=(b, t // tt),
        in_specs=[tok(N_GLA), tok(N_ML), tok(N_SMALL)] + [_layer_spec(a, layer)
                                                           for a in (a2p, ab, gnw, cw, cb, ibp, fbp)],
        out_specs=[tok(HV), tok(HV)],
        out_shape=[jax.ShapeDtypeStruct((b, t, HV), BF16)] * 2,
        scratch_shapes=[pltpu.VMEM((HK, HV), F32), pltpu.VMEM((HK, HV), F32), pltpu.VMEM((HK, LANES), F32),
                        pltpu.VMEM((1, LANES), F32), pltpu.VMEM((8 + tt, 2 * HK), F32)],
        compiler_params=_params(("parallel", "arbitrary")),
        name="mixer",
    )(zg, zm, zs, a2p, ab, gnw, cw, cb, ibp, fbp)


def _s5_prep_kernel(lr_ref, li_ref, lrc_ref, lic_ref, dt_ref, btr_ref, bti_ref, ctr_ref, cti_ref, d_ref,
                    t_ref, w_ref, v_ref, pa_ref, pb_ref, *, n_pow):
    P = S5_STATE
    nb = S5_BLOCK
    dt = dt_ref[0]
    lr = jnp.minimum(lr_ref[0], -1e-4)
    li = li_ref[0]
    lrc = jnp.minimum(lrc_ref[0], -1e-4)
    lic = lic_ref[0]
    dtr = jnp.exp(dt)

    def power(lr_, li_, n):
        mag = jnp.exp(lr_ * dtr * n)
        ang = li_ * dtr * n
        return mag * jnp.cos(ang), mag * jnp.sin(ang)

    n_rows = jnp.where(lax.broadcasted_iota(jnp.int32, (nb + 8, 1), 0) < nb,
                       nb - 1 - lax.broadcasted_iota(jnp.int32, (nb + 8, 1), 0),
                       jnp.where(lax.broadcasted_iota(jnp.int32, (nb + 8, 1), 0) == nb, 1, nb)).astype(F32)
    row_r, row_i = power(lr, li, n_rows)
    col_r, col_i = power(lrc, lic, lax.broadcasted_iota(jnp.int32, (P, LANES), 1).astype(F32))

    ar1, ai1 = row_r[nb:nb + 1], row_i[nb:nb + 1]
    inv = 1.0 / (lr * lr + li * li)
    zr = ((ar1 - 1.0) * lr + ai1 * li) * inv
    zi = (ai1 * lr - (ar1 - 1.0) * li) * inv
    bbr = zr * btr_ref[0] - zi * bti_ref[0]
    bbi = zr * bti_ref[0] + zi * btr_ref[0]

    ctr, cti = ctr_ref[0], cti_ref[0]
    src = lax.broadcasted_iota(jnp.int32, (LANES, nb * S5_GROUP), 0)
    dst_t = lax.broadcasted_iota(jnp.int32, (LANES, nb * S5_GROUP), 1) // S5_GROUP

    def out_map(shift):
        expand = (src == dst_t + shift).astype(F32)
        ar = jnp.dot(col_r, expand, precision=HIGHEST, preferred_element_type=F32)
        ai = jnp.dot(col_i, expand, precision=HIGHEST, preferred_element_type=F32)
        return jnp.concatenate([ctr * ar - cti * ai, -(ctr * ai + cti * ar)], axis=0)

    v_ref[0] = out_map(1).T.astype(v_ref.dtype)
    base = jnp.dot(jnp.concatenate([bbr, bbi], axis=1), out_map(0), precision=HIGHEST,
                   preferred_element_type=F32)
    lane = lax.broadcasted_iota(jnp.int32, base.shape, 1)
    base = base + jnp.where(lane == lax.broadcasted_iota(jnp.int32, base.shape, 0), d_ref[0], 0.0)
    rows = [base]
    for s in range(1, nb):
        rows.append(jnp.where(lane >= s * S5_GROUP, pltpu.roll(base, s * S5_GROUP, 1), 0.0))
    t_ref[0] = jnp.concatenate(rows, axis=0).T.astype(t_ref.dtype)

    pick = (lax.broadcasted_iota(jnp.int32, (nb * S5_GROUP, nb + 8), 0) // S5_GROUP
            == lax.broadcasted_iota(jnp.int32, (nb * S5_GROUP, nb + 8), 1)).astype(F32)
    rep = lambda a: jnp.dot(pick, a, precision=HIGHEST, preferred_element_type=F32)
    ar, ai = rep(row_r), rep(row_i)
    bbr_t = jnp.concatenate([bbr] * nb, axis=0)
    bbi_t = jnp.concatenate([bbi] * nb, axis=0)
    wt = jnp.concatenate([ar * bbr_t - ai * bbi_t, ar * bbi_t + ai * bbr_t], axis=1).T
    w_ref[0] = jnp.concatenate([wt, wt[P:2 * P], wt[0:P]], axis=0).astype(w_ref.dtype)

    pr, pi = row_r[nb + 1:nb + 2], row_i[nb + 1:nb + 2]
    pa, pb = [], []
    for _ in range(n_pow):
        pa.append(jnp.concatenate([pr, pr], axis=1))
        pb.append(jnp.concatenate([-pi, pi], axis=1))
        pr, pi = pr * pr - pi * pi, 2.0 * pr * pi
    pa_ref[0] = jnp.concatenate(pa, axis=0)
    pb_ref[0] = jnp.concatenate(pb, axis=0)


def _s5_prep(lam_re, lam_im, log_step, b_re, b_im, c_re, c_im, d_skip, *, n_pow):
    nl, g, p = lam_re.shape
    nbg = S5_BLOCK * S5_GROUP
    bt = lambda a: jnp.swapaxes(a, 2, 3)
    ct = lambda a: jnp.tile(jnp.swapaxes(a, 2, 3), (1, 1, 1, S5_BLOCK))
    row = lambda a: a.reshape(nl, g, 1, p)
    colv = lambda a: a.reshape(nl, g, p, 1)
    dtile = jnp.tile(d_skip.reshape(nl, g, 1, S5_GROUP), (1, 1, 1, S5_BLOCK))
    spec = lambda *s: pl.BlockSpec((None, 1) + s, lambda l, i: (l, i) + (0,) * len(s))
    shape = lambda dt, *s: jax.ShapeDtypeStruct((nl, g) + s, dt)
    return pl.pallas_call(
        functools.partial(_s5_prep_kernel, n_pow=n_pow),
        grid=(nl, g),
        in_specs=[spec(1, p), spec(1, p), spec(p, 1), spec(p, 1), spec(1, 1),
                  spec(S5_GROUP, p), spec(S5_GROUP, p), spec(p, nbg), spec(p, nbg), spec(1, nbg)],
        out_specs=[spec(nbg, nbg), spec(4 * p, nbg), spec(nbg, 2 * p), spec(n_pow, 2 * p), spec(n_pow, 2 * p)],
        out_shape=[shape(BF16, nbg, nbg), shape(BF16, 4 * p, nbg), shape(BF16, nbg, 2 * p),
                   shape(F32, n_pow, 2 * p), shape(F32, n_pow, 2 * p)],
        compiler_params=_params(("parallel", "parallel")),
        name="s5_prep",
    )(row(lam_re), row(lam_im), colv(lam_re), colv(lam_im), log_step.reshape(nl, g, 1, 1),
      bt(b_re), bt(b_im), ct(c_re), ct(c_im), dtile)


S5_SCAN_GROUPS = 4


def _s5_kernel(u_ref, t_ref, w_ref, v_ref, pa_ref, pb_ref, y_ref,
               ut_ref, yt_ref, zt_ref, zw_ref, xin_ref, xc_ref, *, groups, n_pow):
    nb = S5_BLOCK
    p2 = 2 * S5_STATE
    slabs, tt = u_ref.shape[1], u_ref.shape[2]
    nblk = tt // nb

    @pl.when(pl.program_id(1) == 0)
    def _():
        xc_ref[...] = jnp.zeros_like(xc_ref)

    gps = S5_SCAN_GROUPS
    n_chunks = groups // gps
    chunks_per_slab = LANES // S5_GROUP // gps

    def relayout_in(s):
        for tl in range(nb):
            ut_ref[tl, s * LANES:(s + 1) * LANES, :] = (
                u_ref[0, s, pl.ds(tl, nblk, stride=nb), :].T.astype(ut_ref.dtype))

    def relayout_out(s):
        for tl in range(nb):
            y_ref[0, s, pl.ds(tl, nblk, stride=nb), :] = yt_ref[tl, s * LANES:(s + 1) * LANES, :].T

    def group_input(g):
        return ut_ref[:, g * S5_GROUP:(g + 1) * S5_GROUP, :].reshape(nb * S5_GROUP, nblk)

    def state_in(ci):
        for g in range(ci * gps, (ci + 1) * gps):
            z2 = jnp.dot(w_ref[g], group_input(g), preferred_element_type=F32)
            zt_ref[g * p2:(g + 1) * p2, :] = z2[0:p2]
            zw_ref[g * p2:(g + 1) * p2, :] = z2[p2:2 * p2]

    width = gps * p2
    row = lax.broadcasted_iota(jnp.int32, (nblk, width), 0)

    def scan(ci):
        l0 = ci * width
        x = zt_ref[pl.ds(l0, width), :].T
        xw = zw_ref[pl.ds(l0, width), :].T
        pa = jnp.concatenate([pa_ref[g] for g in range(ci * gps, (ci + 1) * gps)], axis=1)
        pb = jnp.concatenate([pb_ref[g] for g in range(ci * gps, (ci + 1) * gps)], axis=1)
        xc = xc_ref[0:1, pl.ds(l0, width)]
        xcw = xc_ref[1:2, pl.ds(l0, width)]
        first = row == 0
        x = x + jnp.where(first, xc * pa[0:1] + xcw * pb[0:1], 0.0)
        xw = xw + jnp.where(first, xcw * pa[0:1] - xc * pb[0:1], 0.0)
        for k in range(n_pow):
            sh = 1 << k
            if sh >= nblk:
                break
            xs, xws, m = pltpu.roll(x, sh, 0), pltpu.roll(xw, sh, 0), row >= sh
            x, xw = (x + jnp.where(m, xs * pa[k:k + 1] + xws * pb[k:k + 1], 0.0),
                     xw + jnp.where(m, xws * pa[k:k + 1] - xs * pb[k:k + 1], 0.0))
        x_in = jnp.where(row >= 1, pltpu.roll(x, 1, 0), xc)
        xc_ref[0:1, pl.ds(l0, width)] = x[nblk - 1:nblk]
        xc_ref[1:2, pl.ds(l0, width)] = xw[nblk - 1:nblk]
        xin_ref[pl.ds(l0, width), :] = x_in.T.astype(xin_ref.dtype)

    def block_out(ci):
        for g in range(ci * gps, (ci + 1) * gps):
            yt = jnp.dot(t_ref[g], group_input(g), preferred_element_type=F32)
            yt = yt + jnp.dot(v_ref[g], xin_ref[g * p2:(g + 1) * p2, :], preferred_element_type=F32)
            yt_ref[:, g * S5_GROUP:(g + 1) * S5_GROUP, :] = yt.reshape(nb, S5_GROUP, nblk)

    relayout_in(0)
    for step in range(n_chunks + 2):
        nxt = step // chunks_per_slab + 1
        if step % chunks_per_slab == 0 and nxt < slabs:
            relayout_in(nxt)
        if step < n_chunks:
            state_in(step)
        if 1 <= step <= n_chunks:
            scan(step - 1)
        if step >= 2:
            block_out(step - 2)
            if (step - 2) % chunks_per_slab == chunks_per_slab - 1:
                relayout_out((step - 2) // chunks_per_slab)


def _s5(u, tmat, wmat, vmat, pa, pb, *, layer, tt):
    b, slabs, t, _ = u.shape
    width = slabs * LANES
    groups = width // S5_GROUP
    assert groups % S5_SCAN_GROUPS == 0
    n_pow = pa.shape[2]
    nblk = tt // S5_BLOCK
    states = groups * 2 * S5_STATE
    tok = pl.BlockSpec((1, slabs, tt, LANES), lambda i, j: (i, 0, j, 0))
    return pl.pallas_call(
        functools.partial(_s5_kernel, groups=groups, n_pow=n_pow),
        grid=(b, t // tt),
        in_specs=[tok] + [_layer_spec(a, layer, True) for a in (tmat, wmat, vmat, pa, pb)],
        out_specs=tok,
        out_shape=jax.ShapeDtypeStruct((b, slabs, t, LANES), F32),
        scratch_shapes=[pltpu.VMEM((S5_BLOCK, width, nblk), BF16), pltpu.VMEM((S5_BLOCK, width, nblk), F32),
                        pltpu.VMEM((states, nblk), F32), pltpu.VMEM((states, nblk), F32),
                        pltpu.VMEM((states, nblk), BF16), pltpu.VMEM((8, states), F32)],
        compiler_params=_params(("parallel", "arbitrary")),
        name="s5",
    )(u, tmat, wmat, vmat, pa, pb)


def _merge_kernel(x_ref, mod_ref, n1_ref, wg_ref, yg_ref, ym_ref, ys_ref, gluw_ref, glub_ref, pg_ref, pm_ref,
                  ps_ref, gb_ref, wo_ref, nw_ref, o_ref, *, d):
    mod = mod_ref[0]
    rows = x_ref.shape[1] // SUBTILES
    for sub in range(SUBTILES):
        r = slice(sub * rows, (sub + 1) * rows)
        x = x_ref[0, r, :]
        hb = (_rms(x, n1_ref[...]) * (1.0 + mod[1:2]) + mod[0:1]).astype(BF16)
        s = _gelu_tanh(jnp.concatenate([ys_ref[0, k, r, :] for k in range(ys_ref.shape[1])], axis=1))
        s = s * _sigmoid(jnp.dot(s.astype(BF16), gluw_ref[...], preferred_element_type=F32) + glub_ref[...])
        merged = None
        for idx, (y, p) in enumerate(((yg_ref[0, r, :], pg_ref), (ym_ref[0, r, :], pm_ref), (s.astype(BF16), ps_ref))):
            cs = slice(idx * d, (idx + 1) * d)
            zg = jnp.dot(hb, wg_ref[:, cs], preferred_element_type=F32) + gb_ref[:, cs]
            term = (jnp.tanh(0.5 * zg) + 1.0) * jnp.dot(y, p[...], preferred_element_type=F32)
            merged = term if merged is None else merged + term
        y = jnp.dot((0.5 * merged).astype(BF16), wo_ref[...], preferred_element_type=F32)
        o_ref[0, r, :] = x + (1.0 + mod[2:3]) * _rms(y, nw_ref[...])


def _merge(x, mod, n1, wg, yg, ym, ys, gluw, glub, pg, pm, ps, gb, wo, nw, *, layer, tm):
    b, t, d = x.shape
    tok = lambda n: pl.BlockSpec((1, tm, n), lambda i, j: (i, j, 0))
    small = lambda a: _layer_spec(a, layer)
    big = lambda a: _layer_spec(a, layer, True)
    return pl.pallas_call(
        functools.partial(_merge_kernel, d=d),
        grid=(b, t // tm),
        in_specs=[tok(d), _mod_spec(mod, layer), small(n1), big(wg),
                  tok(HV), tok(HV), pl.BlockSpec((1, ys.shape[1], tm, LANES), lambda i, j: (i, 0, j, 0)),
                  big(gluw), small(glub), big(pg), big(pm), big(ps), small(gb), big(wo), small(nw)],
        out_specs=tok(d),
        out_shape=jax.ShapeDtypeStruct((b, t, d), F32),
        compiler_params=_params(("parallel", "parallel")),
        name="merge_out",
    )(x, mod, n1, wg, yg, ym, ys, gluw, glub, pg, pm, ps, gb, wo, nw)


def _ffn_kernel(x_ref, mod_ref, n1_ref, wi_ref, wo_ref, n2_ref, o_ref, *, hidden):
    mod = mod_ref[0]
    rows = x_ref.shape[1] // SUBTILES
    for s in range(SUBTILES):
        r = slice(s * rows, (s + 1) * rows)
        x = x_ref[0, r, :]
        h = (_rms(x, n1_ref[...]) * (1.0 + mod[4:5]) + mod[3:4]).astype(BF16)
        gate = jnp.dot(h, wi_ref[:, 0:hidden], preferred_element_type=F32)
        up = jnp.dot(h, wi_ref[:, hidden:2 * hidden], preferred_element_type=F32)
        act = (_silu(gate) * up).astype(BF16)
        y = jnp.dot(act, wo_ref[...], preferred_element_type=F32)
        o_ref[0, r, :] = x + (1.0 + mod[5:6]) * _rms(y, n2_ref[...])


def _ffn(x, mod, n1, wi, wo, n2, *, layer, tm):
    b, t, d = x.shape
    hidden = wo.shape[1]
    tok = pl.BlockSpec((1, tm, d), lambda i, j: (i, j, 0))
    return pl.pallas_call(
        functools.partial(_ffn_kernel, hidden=hidden),
        grid=(b, t // tm),
        in_specs=[tok, _mod_spec(mod, layer), _layer_spec(n1, layer), _layer_spec(wi, layer, True),
                  _layer_spec(wo, layer, True), _layer_spec(n2, layer)],
        out_specs=tok,
        out_shape=jax.ShapeDtypeStruct((b, t, d), F32),
        compiler_params=_params(("parallel", "parallel")),
        name="ffn",
    )(x, mod, n1, wi, wo, n2)


def _regroup_w_in(w_in, d, s5w):
    sizes = (HK, HK, HV, RANK, HV, HK, HK, HV, HEADS, HEADS, HV, s5w, 3 * d)
    offs, a = [], 0
    for s in sizes:
        offs.append(a)
        a += s
    col = lambda i: w_in[..., offs[i]:offs[i] + sizes[i]]
    zeros = lambda n: jnp.zeros(w_in.shape[:-1] + (n,), w_in.dtype)
    small_a = jnp.concatenate([col(3), col(8), zeros(LANES - RANK - HEADS)], axis=-1)
    small_b = jnp.concatenate([zeros(GATE_LANE), col(9), zeros(LANES - GATE_LANE - HEADS)], axis=-1)
    mixers = jnp.concatenate([col(0), col(1), col(2), col(4), col(5), col(6), col(7), col(10), col(11),
                              small_a, small_b], axis=-1)
    return mixers.astype(BF16), col(12).astype(BF16)


def _lane_pad(vec, lane0):
    return jnp.zeros((vec.shape[0], 1, LANES), F32).at[:, 0, lane0:lane0 + vec.shape[1]].set(vec)


def kernel(x, c, ada_w, ada_b, pre1_w, post1_w, pre2_w, post2_w, w_in, gla_a2, gla_a_b, gla_norm_w, ml_conv_w, ml_conv_b, ml_i_b, ml_f_b, s5_lam_re, s5_lam_im, s5_log_step, s5_b_re, s5_b_im, s5_c_re, s5_c_im, s5_d, s5_glu_w, s5_glu_b, proj_gla, proj_ml, proj_s5, branch_gate_b, w_out, ffn_w_in, ffn_w_out):
    b, t, d = x.shape
    depth = ada_w.shape[0]
    s5w = s5_d.shape[1]
    assert RANK + HEADS <= GATE_LANE + HEADS <= LANES and GATE_LANE == RANK
    assert DV == LANES and DK % 8 == 0
    assert t % 256 == 0 and d % LANES == 0
    tm = 512 if t % 512 == 0 else 256
    tm_sub = SUBTILES * tm if t % (SUBTILES * tm) == 0 else tm
    tt = 1024 if t % 1024 == 0 else 256
    tt_s5 = 2048 if t % 2048 == 0 else t
    n_pow = max(1, (tt_s5 // S5_BLOCK - 1).bit_length())

    rows = lambda a: a.reshape(depth, 1, -1)
    bf = lambda a: a.astype(BF16)
    mod = _ada_mod(c, ada_w, ada_b).reshape(depth, b, 6, d)
    w_mix, w_gate = _regroup_w_in(w_in, d, s5w)
    a2p = jnp.zeros((depth, LANES, HK), BF16).at[:, 0:RANK].set(bf(gla_a2))
    ibp, fbp = _lane_pad(ml_i_b, GATE_LANE), _lane_pad(ml_f_b, GATE_LANE)
    s5_ops = _s5_prep(s5_lam_re, s5_lam_im, s5_log_step, s5_b_re, s5_b_im, s5_c_re, s5_c_im,
                      s5_d.reshape(depth, -1, S5_GROUP), n_pow=n_pow)
    glu_w, p_gla, p_ml, p_s5, wo, f_in, f_out = map(bf, (s5_glu_w, proj_gla, proj_ml, proj_s5, w_out,
                                                         ffn_w_in, ffn_w_out))
    for l in range(depth):
        zg, zm, zs5, zsm = _inproj(x, mod, rows(pre1_w), w_mix, layer=l, tm=tm_sub, s5w=s5w)
        y_gla, y_ml = _mixer(zg, zm, zsm, a2p, rows(gla_a_b), rows(gla_norm_w), ml_conv_w, rows(ml_conv_b),
                             ibp, fbp, layer=l, tt=tt)
        ys = _s5(zs5, *s5_ops, layer=l, tt=tt_s5)
        x = _merge(x, mod, rows(pre1_w), w_gate, y_gla, y_ml, ys, glu_w, rows(s5_glu_b), p_gla, p_ml, p_s5,
                   rows(branch_gate_b), wo, rows(post1_w), layer=l, tm=tm_sub)
        x = _ffn(x, mod, rows(pre2_w), f_in, f_out, rows(post2_w), layer=l, tm=tm_sub)
    return x
```

```python
import functools
import math

import jax
import jax.numpy as jnp
from jax import lax
from jax.experimental import pallas as pl
from jax.experimental.pallas import tpu as pltpu

HEADS = 4
DK = 64
DV = 128
RANK = 16
GATE_NORM = 16.0
CONV = 4
S5_GROUP = 16
S5_STATE = 64
CHUNK = 64
S5_BLOCK = 16
RMS_EPS = 1e-6
LANES = 128
GATE_LANE = 16
SUBTILES = 2
VMEM_LIMIT = 56 * 1024 * 1024

HK = HEADS * DK
HV = HEADS * DV
F32 = jnp.float32
BF16 = jnp.bfloat16
HIGHEST = lax.Precision.HIGHEST
NT = (((1,), (1,)), ((), ()))
TN = (((0,), (0,)), ((), ()))


def _params(sem):
    return pltpu.CompilerParams(dimension_semantics=sem, vmem_limit_bytes=VMEM_LIMIT)


def _layer_spec(a, layer, single_buffer=False):
    nd = a.ndim - 1
    kw = dict(pipeline_mode=pl.Buffered(1)) if single_buffer else {}
    return pl.BlockSpec((None,) + a.shape[1:], lambda *_: (layer,) + (0,) * nd, **kw)


def _mod_spec(mod, layer):
    return pl.BlockSpec((None, 1) + mod.shape[2:], lambda i, j: (layer, i, 0, 0))


def _sigmoid(x):
    return 0.5 * jnp.tanh(0.5 * x) + 0.5


def _silu(x):
    return x * _sigmoid(x)


def _log_sigmoid(x):
    return jnp.minimum(x, 0.0) - jnp.log(1.0 + jnp.exp(-jnp.abs(x)))


def _gelu_tanh(x):
    c = math.sqrt(2.0 / math.pi)
    return 0.5 * x * (1.0 + jnp.tanh(c * (x + 0.044715 * (x * x * x))))


def _rms(x, w):
    ms = jnp.mean(x * x, axis=-1, keepdims=True)
    return x * lax.rsqrt(ms + RMS_EPS) * w


def _ada_kernel(c_ref, w_ref, b_ref, o_ref):
    ca = _silu(c_ref[...]).astype(BF16)
    o_ref[0] = jnp.dot(ca, w_ref[0].astype(BF16), preferred_element_type=F32) + b_ref[0]


def _ada_mod(c, ada_w, ada_b):
    nl, d, n = ada_w.shape
    b = c.shape[0]
    tn = 1024
    return pl.pallas_call(
        _ada_kernel,
        grid=(nl, n // tn),
        in_specs=[pl.BlockSpec((b, d), lambda l, j: (0, 0)),
                  pl.BlockSpec((1, d, tn), lambda l, j: (l, 0, j)),
                  pl.BlockSpec((1, 1, tn), lambda l, j: (l, 0, j))],
        out_specs=pl.BlockSpec((1, b, tn), lambda l, j: (l, 0, j)),
        out_shape=jax.ShapeDtypeStruct((nl, b, n), F32),
        compiler_params=_params(("arbitrary", "arbitrary")),
        name="ada_mod",
    )(c, ada_w, ada_b.reshape(nl, 1, n))


N_GLA = 2 * HK + 2 * HV
N_ML = 2 * HK + 2 * HV
N_SMALL = 2 * LANES


def _inproj_kernel(x_ref, mod_ref, nw_ref, wg_qkv, wg_r, wm_qkv, wm_o, w_s5, w_small, cw_ref, cb_ref,
                   zg_ref, zm_ref, zs_ref, zsm_ref, ext_ref):
    tm = x_ref.shape[1]

    @pl.when(pl.program_id(1) == 0)
    def _():
        ext_ref[0:8, :] = jnp.zeros((8, 2 * HK), F32)

    mod = mod_ref[0]
    split = 2 * HK + HV
    rows = tm // SUBTILES
    for sub in range(SUBTILES):
        r = slice(sub * rows, (sub + 1) * rows)
        hb = (_rms(x_ref[0, r, :], nw_ref[...]) * (1.0 + mod[1:2]) + mod[0:1]).astype(BF16)
        proj = lambda w: jnp.dot(hb, w[...], preferred_element_type=F32)

        r8 = slice(8 + sub * rows, 8 + (sub + 1) * rows)
        ext_ref[r8, :] = jnp.dot(hb, wm_qkv[:, 0:2 * HK], preferred_element_type=F32)
        acc = jnp.zeros((rows, 2 * HK), F32) + cb_ref[...]
        for j in range(CONV):
            off = r8.start - (CONV - 1) + j
            acc = acc + cw_ref[j:j + 1, :] * ext_ref[off:off + rows, :]
        zm_ref[0, r, 0:2 * HK] = _silu(acc).astype(zm_ref.dtype)

        zg_ref[0, r, 0:split] = proj(wg_qkv).astype(zg_ref.dtype)
        zg_ref[0, r, split:] = _silu(proj(wg_r)).astype(zg_ref.dtype)
        zm_ref[0, r, 2 * HK:split] = jnp.dot(hb, wm_qkv[:, 2 * HK:], preferred_element_type=F32).astype(zm_ref.dtype)
        zm_ref[0, r, split:] = _sigmoid(proj(wm_o)).astype(zm_ref.dtype)

        s5 = proj(w_s5)
        for s in range(zs_ref.shape[1]):
            zs_ref[0, s, r, :] = s5[:, s * LANES:(s + 1) * LANES]
        zsm_ref[0, r, :] = proj(w_small)
    ext_ref[0:8, :] = ext_ref[tm:tm + 8, :]


def _inproj(x, mod, nw, weights, cw, cb, *, layer, tm):
    b, t, d = x.shape
    s5w = weights[4].shape[2]
    tok = lambda n: pl.BlockSpec((1, tm, n), lambda i, j: (i, j, 0))
    slabs = s5w // LANES
    return pl.pallas_call(
        _inproj_kernel,
        grid=(b, t // tm),
        in_specs=([tok(d), _mod_spec(mod, layer), _layer_spec(nw, layer)]
                  + [_layer_spec(w, layer, True) for w in weights] + [_layer_spec(cw, layer), _layer_spec(cb, layer)]),
        out_specs=[tok(N_GLA), tok(N_ML), pl.BlockSpec((1, slabs, tm, LANES), lambda i, j: (i, 0, j, 0)),
                   tok(N_SMALL)],
        out_shape=[jax.ShapeDtypeStruct((b, t, N_GLA), BF16), jax.ShapeDtypeStruct((b, t, N_ML), BF16),
                   jax.ShapeDtypeStruct((b, slabs, t, LANES), F32),
                   jax.ShapeDtypeStruct((b, t, N_SMALL), F32)],
        scratch_shapes=[pltpu.VMEM((8 + tm, 2 * HK), F32)],
        compiler_params=_params(("parallel", "arbitrary")),
        name="in_proj",
    )(x, mod, nw, *weights, cw, cb)


def _block_diag_rows(a, width):
    lane_head = lax.broadcasted_iota(jnp.int32, a.shape, 1) // width
    zero = jnp.zeros_like(a)
    return jnp.concatenate([jnp.where(lane_head == h, a, zero) for h in range(HEADS)], axis=0)


def _scan_rows(x, period, op, fill):
    pos = lax.broadcasted_iota(jnp.int32, x.shape, 0) % period
    sh = 1
    while sh < period:
        x = op(x, jnp.where(pos >= sh, pltpu.roll(x, sh, 0), fill))
        sh *= 2
    return x


def _bcast_heads(tile, per_head):
    rows = tile.shape[0]
    full = [jnp.broadcast_to(tile[:, GATE_LANE + h:GATE_LANE + h + 1], (rows, LANES)) for h in range(HEADS)]
    if per_head == LANES:
        return jnp.concatenate(full, axis=1)
    assert 2 * per_head == LANES and HEADS % 2 == 0
    low = lax.broadcasted_iota(jnp.int32, (rows, LANES), 1) < per_head
    return jnp.concatenate([jnp.where(low, full[h], full[h + 1]) for h in range(0, HEADS, 2)], axis=1)


def _gla_levels(q, k, g, c):
    n = q.shape[0]
    row = lax.broadcasted_iota(jnp.int32, q.shape, 0)
    out = []
    hs = 1
    while 2 * hs <= n:
        blk = 2 * hs
        pos = row % blk
        upper = pos >= hs
        if hs == 1:
            x = jnp.where(upper, g, 0.0)
        elif hs == 2:
            g_prev = pltpu.roll(g, 1, 0)
            g_next = pltpu.roll(g, n - 1, 0)
            x = jnp.where(pos == 2, g, jnp.where(pos == 3, g + g_prev, jnp.where(pos == 0, g_next, 0.0)))
        elif hs % 8 != 0:
            c3 = c.reshape(n // blk, blk, c.shape[1])
            anchor = jnp.broadcast_to(c3[:, hs - 1:hs, :], c3.shape).reshape(c.shape)
            x = -jnp.abs(c - anchor)
        else:
            split = lambda a: a.reshape(n // blk, 2, hs, a.shape[1])
            c4, q4, k4 = split(c), split(q), split(k)
            anchor = c4[:, 0, hs - 1:hs, :]
            mixed = jnp.concatenate([k4[:, 0] * jnp.exp(anchor - c4[:, 0]), q4[:, 1] * jnp.exp(c4[:, 1] - anchor)],
                                    axis=1)
            out.append((mixed.reshape(q.shape), hs))
            hs = blk
            continue
        out.append((jnp.where(upper, q, k) * jnp.exp(x), hs))
        hs = blk
    return out


def _pair_block_diag(a):
    low = lax.broadcasted_iota(jnp.int32, a.shape, 1) < a.shape[1] // 2
    zero = jnp.zeros_like(a)
    return jnp.concatenate([jnp.where(low, a, zero), jnp.where(low, zero, a)], axis=0)


def _own_block(h):
    return slice(h * DK, (h + 1) * DK), slice(h * DV, (h + 1) * DV)


def _state_operand(st_ref):
    zero = jnp.zeros((DK, DV), BF16)
    return jnp.concatenate(
        [jnp.concatenate([st_ref[_own_block(h)].astype(BF16) if j == h else zero for j in range(HEADS)], axis=1)
         for h in range(HEADS)], axis=0)


def _gla_chunk(q, k, v, gr, g, c, nw, st_ref):
    L = q.shape[0]
    col = lax.broadcasted_iota(jnp.int32, (L, LANES), 1) % DK
    rowi = lax.broadcasted_iota(jnp.int32, (L, LANES), 0)
    qb, kb = q.astype(BF16), k.astype(BF16)
    levels = [(t.astype(BF16), hs) for t, hs in _gla_levels(q, k, g, c)]
    pairs = []
    for p in range(HK // LANES):
        ls = slice(p * LANES, (p + 1) * LANES)
        s = lax.dot_general(qb[:, ls], _pair_block_diag(kb[:, ls]), NT, preferred_element_type=F32)
        acc = jnp.where(rowi == col, s, 0.0)
        for tb, hs in levels:
            s = lax.dot_general(tb[:, ls], _pair_block_diag(tb[:, ls]), NT, preferred_element_type=F32)
            blk = 2 * hs
            valid = ((rowi // blk) == (col // blk)) & (rowi % blk >= hs) & (col % blk < hs)
            acc = acc + jnp.where(valid, s, 0.0)
        pairs.append(acc)
    scores = jnp.concatenate(pairs, axis=1)

    o = jnp.dot(scores.astype(BF16), _block_diag_rows(v, DV), preferred_element_type=F32)
    o = o + jnp.dot((q * jnp.exp(c)).astype(BF16), _state_operand(st_ref), preferred_element_type=F32)
    c_last = c[L - 1:L]
    kd = (k * jnp.exp(c_last - c)).astype(BF16)
    upd = lax.dot_general(kd, v, TN, preferred_element_type=F32)
    decay = jnp.broadcast_to(jnp.exp(c_last), (LANES, HK)).T
    for h in range(HEADS):
        blk = _own_block(h)
        st_ref[blk] = decay[blk[0], :] * st_ref[blk] + upd[blk]

    parts = []
    for h in range(HEADS):
        oh = o[:, h * DV:(h + 1) * DV]
        parts.append(oh * lax.rsqrt(jnp.mean(oh * oh, axis=-1, keepdims=True) + RMS_EPS))
    return jnp.concatenate(parts, axis=1) * nw * gr


def _mlstm_chunk(q, k, v, og, u, bcum, cmax, mt_ref, nrm_ref, m_ref):
    L = q.shape[0]
    lane = lax.broadcasted_iota(jnp.int32, (L, LANES), 1)
    gate_lanes = (lane >= GATE_LANE) & (lane < GATE_LANE + HEADS)
    m_prev = m_ref[...]
    big_m = jnp.maximum(m_prev, cmax)
    m_t = bcum + big_m
    m_new = m_t[L - 1:L]
    b_last = bcum[L - 1:L]

    ut = jnp.concatenate([u, u], axis=0).T
    low = lax.broadcasted_iota(jnp.int32, (1, 2 * L), 1) < L
    urow = jnp.concatenate([jnp.where(low, ut[GATE_LANE + h:GATE_LANE + h + 1], ut[GATE_LANE + h + 1:GATE_LANE + h + 2])
                            for h in range(0, HEADS, 2)], axis=1)
    causal = (lax.broadcasted_iota(jnp.int32, (L, HK), 1) % DK) <= lax.broadcasted_iota(jnp.int32, (L, HK), 0)
    dmat = _bcast_heads(-big_m, DK) + urow
    pmat = jnp.where(causal, jnp.exp(jnp.where(causal, dmat, 0.0)), 0.0)

    qb = q.astype(BF16)
    qk_s = lax.dot_general(qb, _block_diag_rows(k.astype(BF16), DK), NT, preferred_element_type=F32)
    sm = (qk_s * pmat).astype(BF16)

    den_lane = (lax.broadcasted_iota(jnp.int32, (HK, LANES), 0) // DK + GATE_LANE
                == lax.broadcasted_iota(jnp.int32, (HK, LANES), 1))
    nrm_cols = jnp.where(den_lane, nrm_ref[...], 0.0).astype(BF16)
    inter = jnp.dot(qb, jnp.concatenate([_state_operand(mt_ref), nrm_cols], axis=1),
                    preferred_element_type=F32)
    v_ext = jnp.concatenate([_block_diag_rows(v, DV), den_lane.astype(BF16)], axis=1)
    intra = jnp.dot(sm, v_ext, preferred_element_type=F32)

    lw = jnp.where(gate_lanes, m_prev - big_m, 0.0)
    den = jnp.exp(lw) * inter[:, HV:] + intra[:, HV:]
    rden = 1.0 / jnp.maximum(jnp.abs(den), jnp.exp(-m_t))
    hout = (jnp.exp(_bcast_heads(lw, DV)) * inter[:, 0:HV] + intra[:, 0:HV]) * _bcast_heads(rden, DV)

    wk = jnp.where(gate_lanes, b_last - m_new + u, 0.0)
    cd = jnp.where(gate_lanes[0:1], b_last + m_prev - m_new, 0.0)
    kw = (k * jnp.exp(_bcast_heads(wk, DK))).astype(BF16)
    upd = lax.dot_general(kw, jnp.concatenate([v, jnp.ones((L, LANES), BF16)], axis=1), TN,
                          preferred_element_type=F32)
    decay = jnp.exp(cd)
    for h in range(HEADS):
        blk = _own_block(h)
        dh = jnp.broadcast_to(decay[:, GATE_LANE + h:GATE_LANE + h + 1], (DK, LANES))
        mt_ref[blk] = dh * mt_ref[blk] + upd[blk]
        nrm_ref[blk[0], :] = dh * nrm_ref[blk[0], :] + upd[blk[0], HV:]
    m_ref[...] = m_new
    return og * hout


def _mixers(zg_ref, zm_ref, zs_ref, a2_ref, ab_ref, gnw_ref, ib_ref, fb_ref,
            yg_ref, ym_ref, st_ref, mt_ref, nrm_ref, m_ref, n_chunks):
    L = CHUNK
    tt = n_chunks * L

    zs_a = zs_ref[:, 0:LANES]
    xg = jnp.dot(zs_a.astype(BF16), a2_ref[...], preferred_element_type=F32) + ab_ref[...]
    g = _log_sigmoid(xg) * (1.0 / GATE_NORM)
    c = _scan_rows(g, L, jnp.add, 0.0)

    lane = lax.broadcasted_iota(jnp.int32, (tt, LANES), 1)
    gate_lanes = (lane >= GATE_LANE) & (lane < GATE_LANE + HEADS)
    ig = jnp.where(gate_lanes, zs_a + ib_ref[...], 0.0)
    lf = jnp.where(gate_lanes, _log_sigmoid(zs_ref[:, LANES:2 * LANES] + fb_ref[...]), 0.0)
    bcum = _scan_rows(lf, L, jnp.add, 0.0)
    u = ig - bcum
    cmax = _scan_rows(u, L, jnp.maximum, -jnp.inf)

    for ci in range(n_chunks):
        r = slice(ci * L, (ci + 1) * L)
        yg = _gla_chunk(zg_ref[r, 0:HK].astype(F32) * (DK ** -0.5), zg_ref[r, HK:2 * HK].astype(F32),
                        zg_ref[r, 2 * HK:2 * HK + HV], zg_ref[r, 2 * HK + HV:2 * HK + 2 * HV].astype(F32),
                        g[r], c[r], gnw_ref[...], st_ref)
        yg_ref[r, :] = yg.astype(yg_ref.dtype)
        ym = _mlstm_chunk(zm_ref[r, 0:HK].astype(F32), zm_ref[r, HK:2 * HK].astype(F32) * (DK ** -0.5),
                          zm_ref[r, 2 * HK:2 * HK + HV], zm_ref[r, 2 * HK + HV:2 * HK + 2 * HV].astype(F32),
                          u[r], bcum[r], cmax[r], mt_ref, nrm_ref, m_ref)
        ym_ref[r, :] = ym.astype(ym_ref.dtype)


def _mixer_kernel(zg_ref, zm_ref, zs_ref, a2_ref, ab_ref, gnw_ref, ib_ref, fb_ref,
                  yg_ref, ym_ref, st_ref, mt_ref, nrm_ref, m_ref, *, n_chunks):
    @pl.when(pl.program_id(1) == 0)
    def _():
        for ref in (st_ref, mt_ref, nrm_ref, m_ref):
            ref[...] = jnp.zeros_like(ref)

    _mixers(zg_ref.at[0], zm_ref.at[0], zs_ref.at[0], a2_ref, ab_ref, gnw_ref, ib_ref, fb_ref,
            yg_ref.at[0], ym_ref.at[0], st_ref, mt_ref, nrm_ref, m_ref, n_chunks)


def _mixer(zg, zm, zs, a2p, ab, gnw, ibp, fbp, *, layer, tt):
    b, t, _ = zg.shape
    tok = lambda n: pl.BlockSpec((1, tt, n), lambda i, j: (i, j, 0))
    return pl.pallas_call(
        functools.partial(_mixer_kernel, n_chunks=tt // CHUNK),
        grid=(b, t // tt),
        in_specs=[tok(N_GLA), tok(N_ML), tok(N_SMALL)] + [_layer_spec(a, layer)
                                                           for a in (a2p, ab, gnw, ibp, fbp)],
        out_specs=[tok(HV), tok(HV)],
        out_shape=[jax.ShapeDtypeStruct((b, t, HV), BF16)] * 2,
        scratch_shapes=[pltpu.VMEM((HK, HV), F32), pltpu.VMEM((HK, HV), F32), pltpu.VMEM((HK, LANES), F32),
                        pltpu.VMEM((1, LANES), F32)],
        compiler_params=_params(("parallel", "arbitrary")),
        name="mixer",
    )(zg, zm, zs, a2p, ab, gnw, ibp, fbp)


def _s5_prep_kernel(lr_ref, li_ref, lrc_ref, lic_ref, dt_ref, btr_ref, bti_ref, ctr_ref, cti_ref, d_ref,
                    t_ref, w_ref, v_ref, pa_ref, pb_ref, *, n_pow):
    P = S5_STATE
    nb = S5_BLOCK
    dt = dt_ref[0]
    lr = jnp.minimum(lr_ref[0], -1e-4)
    li = li_ref[0]
    lrc = jnp.minimum(lrc_ref[0], -1e-4)
    lic = lic_ref[0]
    dtr = jnp.exp(dt)

    def power(lr_, li_, n):
        mag = jnp.exp(lr_ * dtr * n)
        ang = li_ * dtr * n
        return mag * jnp.cos(ang), mag * jnp.sin(ang)

    n_rows = jnp.where(lax.broadcasted_iota(jnp.int32, (nb + 8, 1), 0) < nb,
                       nb - 1 - lax.broadcasted_iota(jnp.int32, (nb + 8, 1), 0),
                       jnp.where(lax.broadcasted_iota(jnp.int32, (nb + 8, 1), 0) == nb, 1, nb)).astype(F32)
    row_r, row_i = power(lr, li, n_rows)
    col_r, col_i = power(lrc, lic, lax.broadcasted_iota(jnp.int32, (P, LANES), 1).astype(F32))

    ar1, ai1 = row_r[nb:nb + 1], row_i[nb:nb + 1]
    inv = 1.0 / (lr * lr + li * li)
    zr = ((ar1 - 1.0) * lr + ai1 * li) * inv
    zi = (ai1 * lr - (ar1 - 1.0) * li) * inv
    bbr = zr * btr_ref[0] - zi * bti_ref[0]
    bbi = zr * bti_ref[0] + zi * btr_ref[0]

    ctr, cti = ctr_ref[0], cti_ref[0]
    src = lax.broadcasted_iota(jnp.int32, (LANES, nb * S5_GROUP), 0)
    dst_t = lax.broadcasted_iota(jnp.int32, (LANES, nb * S5_GROUP), 1) // S5_GROUP

    def out_map(shift):
        expand = (src == dst_t + shift).astype(F32)
        ar = jnp.dot(col_r, expand, precision=HIGHEST, preferred_element_type=F32)
        ai = jnp.dot(col_i, expand, precision=HIGHEST, preferred_element_type=F32)
        return jnp.concatenate([ctr * ar - cti * ai, -(ctr * ai + cti * ar)], axis=0)

    v_ref[0] = out_map(1).T.astype(v_ref.dtype)
    base = jnp.dot(jnp.concatenate([bbr, bbi], axis=1), out_map(0), precision=HIGHEST,
                   preferred_element_type=F32)
    lane = lax.broadcasted_iota(jnp.int32, base.shape, 1)
    base = base + jnp.where(lane == lax.broadcasted_iota(jnp.int32, base.shape, 0), d_ref[0], 0.0)
    rows = [base]
    for s in range(1, nb):
        rows.append(jnp.where(lane >= s * S5_GROUP, pltpu.roll(base, s * S5_GROUP, 1), 0.0))
    t_ref[0] = jnp.concatenate(rows, axis=0).T.astype(t_ref.dtype)

    pick = (lax.broadcasted_iota(jnp.int32, (nb * S5_GROUP, nb + 8), 0) // S5_GROUP
            == lax.broadcasted_iota(jnp.int32, (nb * S5_GROUP, nb + 8), 1)).astype(F32)
    rep = lambda a: jnp.dot(pick, a, precision=HIGHEST, preferred_element_type=F32)
    ar, ai = rep(row_r), rep(row_i)
    bbr_t = jnp.concatenate([bbr] * nb, axis=0)
    bbi_t = jnp.concatenate([bbi] * nb, axis=0)
    wt = jnp.concatenate([ar * bbr_t - ai * bbi_t, ar * bbi_t + ai * bbr_t], axis=1).T
    w_ref[0] = jnp.concatenate([wt, wt[P:2 * P], wt[0:P]], axis=0).astype(w_ref.dtype)

    pr, pi = row_r[nb + 1:nb + 2], row_i[nb + 1:nb + 2]
    pa, pb = [], []
    for _ in range(n_pow):
        pa.append(jnp.concatenate([pr, pr], axis=1))
        pb.append(jnp.concatenate([-pi, pi], axis=1))
        pr, pi = pr * pr - pi * pi, 2.0 * pr * pi
    pa_ref[0] = jnp.concatenate(pa, axis=0)
    pb_ref[0] = jnp.concatenate(pb, axis=0)


def _s5_prep(lam_re, lam_im, log_step, b_re, b_im, c_re, c_im, d_skip, *, n_pow):
    nl, g, p = lam_re.shape
    nbg = S5_BLOCK * S5_GROUP
    bt = lambda a: jnp.swapaxes(a, 2, 3)
    ct = lambda a: jnp.tile(jnp.swapaxes(a, 2, 3), (1, 1, 1, S5_BLOCK))
    row = lambda a: a.reshape(nl, g, 1, p)
    colv = lambda a: a.reshape(nl, g, p, 1)
    dtile = jnp.tile(d_skip.reshape(nl, g, 1, S5_GROUP), (1, 1, 1, S5_BLOCK))
    spec = lambda *s: pl.BlockSpec((None, 1) + s, lambda l, i: (l, i) + (0,) * len(s))
    shape = lambda dt, *s: jax.ShapeDtypeStruct((nl, g) + s, dt)
    return pl.pallas_call(
        functools.partial(_s5_prep_kernel, n_pow=n_pow),
        grid=(nl, g),
        in_specs=[spec(1, p), spec(1, p), spec(p, 1), spec(p, 1), spec(1, 1),
                  spec(S5_GROUP, p), spec(S5_GROUP, p), spec(p, nbg), spec(p, nbg), spec(1, nbg)],
        out_specs=[spec(nbg, nbg), spec(4 * p, nbg), spec(nbg, 2 * p), spec(n_pow, 2 * p), spec(n_pow, 2 * p)],
        out_shape=[shape(BF16, nbg, nbg), shape(BF16, 4 * p, nbg), shape(BF16, nbg, 2 * p),
                   shape(F32, n_pow, 2 * p), shape(F32, n_pow, 2 * p)],
        compiler_params=_params(("parallel", "parallel")),
        name="s5_prep",
    )(row(lam_re), row(lam_im), colv(lam_re), colv(lam_im), log_step.reshape(nl, g, 1, 1),
      bt(b_re), bt(b_im), ct(c_re), ct(c_im), dtile)


S5_SCAN_GROUPS = 4


def _s5_kernel(u_ref, t_ref, w_ref, v_ref, pa_ref, pb_ref, y_ref,
               ut_ref, yt_ref, zt_ref, zw_ref, xin_ref, xc_ref, *, groups, n_pow):
    nb = S5_BLOCK
    p2 = 2 * S5_STATE
    slabs, tt = u_ref.shape[1], u_ref.shape[2]
    nblk = tt // nb

    @pl.when(pl.program_id(1) == 0)
    def _():
        xc_ref[...] = jnp.zeros_like(xc_ref)

    gps = S5_SCAN_GROUPS
    n_chunks = groups // gps
    chunks_per_slab = LANES // S5_GROUP // gps

    def relayout_in(s):
        for tl in range(nb):
            ut_ref[tl, s * LANES:(s + 1) * LANES, :] = (
                u_ref[0, s, pl.ds(tl, nblk, stride=nb), :].T.astype(ut_ref.dtype))

    def relayout_out(s):
        for tl in range(nb):
            y_ref[0, s, pl.ds(tl, nblk, stride=nb), :] = yt_ref[tl, s * LANES:(s + 1) * LANES, :].T

    def group_input(g):
        return ut_ref[:, g * S5_GROUP:(g + 1) * S5_GROUP, :].reshape(nb * S5_GROUP, nblk)

    def state_in(ci):
        for g in range(ci * gps, (ci + 1) * gps):
            z2 = jnp.dot(w_ref[g], group_input(g), preferred_element_type=F32)
            zt_ref[g * p2:(g + 1) * p2, :] = z2[0:p2]
            zw_ref[g * p2:(g + 1) * p2, :] = z2[p2:2 * p2]

    width = gps * p2
    row = lax.broadcasted_iota(jnp.int32, (nblk, width), 0)

    def scan(ci):
        l0 = ci * width
        x = zt_ref[pl.ds(l0, width), :].T
        xw = zw_ref[pl.ds(l0, width), :].T
        pa = jnp.concatenate([pa_ref[g] for g in range(ci * gps, (ci + 1) * gps)], axis=1)
        pb = jnp.concatenate([pb_ref[g] for g in range(ci * gps, (ci + 1) * gps)], axis=1)
        xc = xc_ref[0:1, pl.ds(l0, width)]
        xcw = xc_ref[1:2, pl.ds(l0, width)]
        first = row == 0
        x = x + jnp.where(first, xc * pa[0:1] + xcw * pb[0:1], 0.0)
        xw = xw + jnp.where(first, xcw * pa[0:1] - xc * pb[0:1], 0.0)
        for k in range(n_pow):
            sh = 1 << k
            if sh >= nblk:
                break
            xs, xws, m = pltpu.roll(x, sh, 0), pltpu.roll(xw, sh, 0), row >= sh
            x, xw = (x + jnp.where(m, xs * pa[k:k + 1] + xws * pb[k:k + 1], 0.0),
                     xw + jnp.where(m, xws * pa[k:k + 1] - xs * pb[k:k + 1], 0.0))
        x_in = jnp.where(row >= 1, pltpu.roll(x, 1, 0), xc)
        xc_ref[0:1, pl.ds(l0, width)] = x[nblk - 1:nblk]
        xc_ref[1:2, pl.ds(l0, width)] = xw[nblk - 1:nblk]
        xin_ref[pl.ds(l0, width), :] = x_in.T.astype(xin_ref.dtype)

    def block_out(ci):
        for g in range(ci * gps, (ci + 1) * gps):
            yt = jnp.dot(t_ref[g], group_input(g), preferred_element_type=F32)
            yt = yt + jnp.dot(v_ref[g], xin_ref[g * p2:(g + 1) * p2, :], preferred_element_type=F32)
            yt_ref[:, g * S5_GROUP:(g + 1) * S5_GROUP, :] = yt.reshape(nb, S5_GROUP, nblk)

    relayout_in(0)
    for step in range(n_chunks + 2):
        nxt = step // chunks_per_slab + 1
        if step % chunks_per_slab == 0 and nxt < slabs:
            relayout_in(nxt)
        if step < n_chunks:
            state_in(step)
        if 1 <= step <= n_chunks:
            scan(step - 1)
        if step >= 2:
            block_out(step - 2)
            if (step - 2) % chunks_per_slab == chunks_per_slab - 1:
                relayout_out((step - 2) // chunks_per_slab)


def _s5(u, tmat, wmat, vmat, pa, pb, *, layer, tt):
    b, slabs, t, _ = u.shape
    width = slabs * LANES
    groups = width // S5_GROUP
    assert groups % S5_SCAN_GROUPS == 0
    n_pow = pa.shape[2]
    nblk = tt // S5_BLOCK
    states = groups * 2 * S5_STATE
    tok = pl.BlockSpec((1, slabs, tt, LANES), lambda i, j: (i, 0, j, 0))
    return pl.pallas_call(
        functools.partial(_s5_kernel, groups=groups, n_pow=n_pow),
        grid=(b, t // tt),
        in_specs=[tok] + [_layer_spec(a, layer, True) for a in (tmat, wmat, vmat, pa, pb)],
        out_specs=tok,
        out_shape=jax.ShapeDtypeStruct((b, slabs, t, LANES), F32),
        scratch_shapes=[pltpu.VMEM((S5_BLOCK, width, nblk), BF16), pltpu.VMEM((S5_BLOCK, width, nblk), F32),
                        pltpu.VMEM((states, nblk), F32), pltpu.VMEM((states, nblk), F32),
                        pltpu.VMEM((states, nblk), BF16), pltpu.VMEM((8, states), F32)],
        compiler_params=_params(("parallel", "arbitrary")),
        name="s5",
    )(u, tmat, wmat, vmat, pa, pb)


def _merge_kernel(x_ref, mod_ref, n1_ref, wg_ref, yg_ref, ym_ref, ys_ref, gluw_ref, glub_ref, pg_ref, pm_ref,
                  ps_ref, gb_ref, wo_ref, nw_ref, o_ref, *, d):
    mod = mod_ref[0]
    rows = x_ref.shape[1] // SUBTILES
    for sub in range(SUBTILES):
        r = slice(sub * rows, (sub + 1) * rows)
        x = x_ref[0, r, :]
        hb = (_rms(x, n1_ref[...]) * (1.0 + mod[1:2]) + mod[0:1]).astype(BF16)
        s = _gelu_tanh(jnp.concatenate([ys_ref[0, k, r, :] for k in range(ys_ref.shape[1])], axis=1))
        s = s * _sigmoid(jnp.dot(s.astype(BF16), gluw_ref[...], preferred_element_type=F32) + glub_ref[...])
        merged = None
        for idx, (y, p) in enumerate(((yg_ref[0, r, :], pg_ref), (ym_ref[0, r, :], pm_ref), (s.astype(BF16), ps_ref))):
            cs = slice(idx * d, (idx + 1) * d)
            zg = jnp.dot(hb, wg_ref[:, cs], preferred_element_type=F32) + gb_ref[:, cs]
            term = (jnp.tanh(0.5 * zg) + 1.0) * jnp.dot(y, p[...], preferred_element_type=F32)
            merged = term if merged is None else merged + term
        y = jnp.dot((0.5 * merged).astype(BF16), wo_ref[...], preferred_element_type=F32)
        o_ref[0, r, :] = x + (1.0 + mod[2:3]) * _rms(y, nw_ref[...])


def _merge(x, mod, n1, wg, yg, ym, ys, gluw, glub, pg, pm, ps, gb, wo, nw, *, layer, tm):
    b, t, d = x.shape
    tok = lambda n: pl.BlockSpec((1, tm, n), lambda i, j: (i, j, 0))
    small = lambda a: _layer_spec(a, layer)
    big = lambda a: _layer_spec(a, layer, True)
    return pl.pallas_call(
        functools.partial(_merge_kernel, d=d),
        grid=(b, t // tm),
        in_specs=[tok(d), _mod_spec(mod, layer), small(n1), big(wg),
                  tok(HV), tok(HV), pl.BlockSpec((1, ys.shape[1], tm, LANES), lambda i, j: (i, 0, j, 0)),
                  big(gluw), small(glub), big(pg), big(pm), big(ps), small(gb), big(wo), small(nw)],
        out_specs=tok(d),
        out_shape=jax.ShapeDtypeStruct((b, t, d), F32),
        compiler_params=_params(("parallel", "parallel")),
        name="merge_out",
    )(x, mod, n1, wg, yg, ym, ys, gluw, glub, pg, pm, ps, gb, wo, nw)


def _ffn_kernel(x_ref, mod_ref, n1_ref, wi_ref, wo_ref, n2_ref, o_ref, *, hidden):
    mod = mod_ref[0]
    rows = x_ref.shape[1] // SUBTILES
    for s in range(SUBTILES):
        r = slice(s * rows, (s + 1) * rows)
        x = x_ref[0, r, :]
        h = (_rms(x, n1_ref[...]) * (1.0 + mod[4:5]) + mod[3:4]).astype(BF16)
        gate = jnp.dot(h, wi_ref[:, 0:hidden], preferred_element_type=F32)
        up = jnp.dot(h, wi_ref[:, hidden:2 * hidden], preferred_element_type=F32)
        act = (_silu(gate) * up).astype(BF16)
        y = jnp.dot(act, wo_ref[...], preferred_element_type=F32)
        o_ref[0, r, :] = x + (1.0 + mod[5:6]) * _rms(y, n2_ref[...])


def _ffn(x, mod, n1, wi, wo, n2, *, layer, tm):
    b, t, d = x.shape
    hidden = wo.shape[1]
    tok = pl.BlockSpec((1, tm, d), lambda i, j: (i, j, 0))
    return pl.pallas_call(
        functools.partial(_ffn_kernel, hidden=hidden),
        grid=(b, t // tm),
        in_specs=[tok, _mod_spec(mod, layer), _layer_spec(n1, layer), _layer_spec(wi, layer, True),
                  _layer_spec(wo, layer, True), _layer_spec(n2, layer)],
        out_specs=tok,
        out_shape=jax.ShapeDtypeStruct((b, t, d), F32),
        compiler_params=_params(("parallel", "parallel")),
        name="ffn",
    )(x, mod, n1, wi, wo, n2)


def _split_w_in(w_in, d, s5w):
    sizes = (HK, HK, HV, RANK, HV, HK, HK, HV, HEADS, HEADS, HV, s5w, 3 * d)
    offs, a = [], 0
    for s in sizes:
        offs.append(a)
        a += s
    def cols(i, j=None):
        j = i if j is None else j
        return w_in[..., offs[i]:offs[j] + sizes[j]].astype(BF16)

    zeros = lambda n: jnp.zeros(w_in.shape[:-1] + (n,), BF16)
    small = jnp.concatenate([cols(3), cols(8), zeros(LANES - RANK - HEADS),
                             zeros(GATE_LANE), cols(9), zeros(LANES - GATE_LANE - HEADS)], axis=-1)
    return (cols(0, 2), cols(4), cols(5, 7), cols(10), cols(11), small), cols(12)


def _lane_pad(vec, lane0):
    return jnp.zeros((vec.shape[0], 1, LANES), F32).at[:, 0, lane0:lane0 + vec.shape[1]].set(vec)


def kernel(x, c, ada_w, ada_b, pre1_w, post1_w, pre2_w, post2_w, w_in, gla_a2, gla_a_b, gla_norm_w, ml_conv_w, ml_conv_b, ml_i_b, ml_f_b, s5_lam_re, s5_lam_im, s5_log_step, s5_b_re, s5_b_im, s5_c_re, s5_c_im, s5_d, s5_glu_w, s5_glu_b, proj_gla, proj_ml, proj_s5, branch_gate_b, w_out, ffn_w_in, ffn_w_out):
    b, t, d = x.shape
    depth = ada_w.shape[0]
    s5w = s5_d.shape[1]
    assert RANK + HEADS <= GATE_LANE + HEADS <= LANES and GATE_LANE == RANK
    assert DV == LANES and DK % 8 == 0
    assert t % 256 == 0 and d % LANES == 0
    tm = 512 if t % 512 == 0 else 256
    tm_sub = SUBTILES * tm if t % (SUBTILES * tm) == 0 else tm
    tt = 1024 if t % 1024 == 0 else 256
    tt_s5 = 2048 if t % 2048 == 0 else t
    n_pow = max(1, (tt_s5 // S5_BLOCK - 1).bit_length())

    rows = lambda a: a.reshape(depth, 1, -1)
    bf = lambda a: a.astype(BF16)
    mod = _ada_mod(c, ada_w, ada_b).reshape(depth, b, 6, d)
    w_mix, w_gate = _split_w_in(w_in, d, s5w)
    a2p = jnp.zeros((depth, LANES, HK), BF16).at[:, 0:RANK].set(bf(gla_a2))
    ibp, fbp = _lane_pad(ml_i_b, GATE_LANE), _lane_pad(ml_f_b, GATE_LANE)
    s5_ops = _s5_prep(s5_lam_re, s5_lam_im, s5_log_step, s5_b_re, s5_b_im, s5_c_re, s5_c_im,
                      s5_d.reshape(depth, -1, S5_GROUP), n_pow=n_pow)
    glu_w, p_gla, p_ml, p_s5, wo, f_in, f_out = map(bf, (s5_glu_w, proj_gla, proj_ml, proj_s5, w_out,
                                                         ffn_w_in, ffn_w_out))
    for l in range(depth):
        zg, zm, zs5, zsm = _inproj(x, mod, rows(pre1_w), w_mix, ml_conv_w, rows(ml_conv_b), layer=l, tm=tm_sub)
        y_gla, y_ml = _mixer(zg, zm, zsm, a2p, rows(gla_a_b), rows(gla_norm_w), ibp, fbp, layer=l, tt=tt)
        ys = _s5(zs5, *s5_ops, layer=l, tt=tt_s5)
        x = _merge(x, mod, rows(pre1_w), w_gate, y_gla, y_ml, ys, glu_w, rows(s5_glu_b), p_gla, p_ml, p_s5,
                   rows(branch_gate_b), wo, rows(post1_w), layer=l, tm=tm_sub)
        x = _ffn(x, mod, rows(pre2_w), f_in, f_out, rows(post2_w), layer=l, tm=tm_sub)
    return x
```

```python
import functools
import math

import jax
import jax.numpy as jnp
from jax import lax
from jax.experimental import pallas as pl
from jax.experimental.pallas import tpu as pltpu

HEADS = 4
DK = 64
DV = 128
RANK = 16
GATE_NORM = 16.0
CONV = 4
S5_GROUP = 16
S5_STATE = 64
CHUNK = 64
S5_BLOCK = 16
RMS_EPS = 1e-6
LANES = 128
GATE_LANE = 16
SUBTILES = 2
VMEM_LIMIT = 56 * 1024 * 1024

HK = HEADS * DK
HV = HEADS * DV
F32 = jnp.float32
BF16 = jnp.bfloat16
HIGHEST = lax.Precision.HIGHEST
NT = (((1,), (1,)), ((), ()))
TN = (((0,), (0,)), ((), ()))


def _params(sem):
    return pltpu.CompilerParams(dimension_semantics=sem, vmem_limit_bytes=VMEM_LIMIT)


def _layer_spec(a, layer, single_buffer=False):
    nd = a.ndim - 1
    kw = dict(pipeline_mode=pl.Buffered(1)) if single_buffer else {}
    return pl.BlockSpec((None,) + a.shape[1:], lambda *_: (layer,) + (0,) * nd, **kw)


def _mod_spec(mod, layer):
    return pl.BlockSpec((None, 1) + mod.shape[2:], lambda i, j: (layer, i, 0, 0))


def _sigmoid(x):
    return 0.5 * jnp.tanh(0.5 * x) + 0.5


def _silu(x):
    return x * _sigmoid(x)


def _log_sigmoid(x):
    return jnp.minimum(x, 0.0) - jnp.log(1.0 + jnp.exp(-jnp.abs(x)))


def _gelu_tanh(x):
    c = math.sqrt(2.0 / math.pi)
    return 0.5 * x * (1.0 + jnp.tanh(c * (x + 0.044715 * (x * x * x))))


def _rms(x, w):
    ms = jnp.mean(x * x, axis=-1, keepdims=True)
    return x * lax.rsqrt(ms + RMS_EPS) * w


def _ada_kernel(c_ref, w_ref, b_ref, o_ref):
    ca = _silu(c_ref[...]).astype(BF16)
    o_ref[0] = jnp.dot(ca, w_ref[0].astype(BF16), preferred_element_type=F32) + b_ref[0]


def _ada_mod(c, ada_w, ada_b):
    nl, d, n = ada_w.shape
    b = c.shape[0]
    tn = 1024
    return pl.pallas_call(
        _ada_kernel,
        grid=(nl, n // tn),
        in_specs=[pl.BlockSpec((b, d), lambda l, j: (0, 0)),
                  pl.BlockSpec((1, d, tn), lambda l, j: (l, 0, j)),
                  pl.BlockSpec((1, 1, tn), lambda l, j: (l, 0, j))],
        out_specs=pl.BlockSpec((1, b, tn), lambda l, j: (l, 0, j)),
        out_shape=jax.ShapeDtypeStruct((nl, b, n), F32),
        compiler_params=_params(("arbitrary", "arbitrary")),
        name="ada_mod",
    )(c, ada_w, ada_b.reshape(nl, 1, n))


N_GLA = 2 * HK + 2 * HV
N_ML = 2 * HK + 2 * HV
N_SMALL = 2 * LANES


def _inproj_kernel(x_ref, mod_ref, nw_ref, wg_qkv, wg_r, wm_qkv, wm_o, w_s5, w_small,
                   zg_ref, zm_ref, zs_ref, zsm_ref):
    mod = mod_ref[0]
    split = 2 * HK + HV
    n_sub = SUBTILES
    rows = x_ref.shape[1] // n_sub

    def normed(sub):
        x = x_ref[0, sub * rows:(sub + 1) * rows, :]
        return (_rms(x, nw_ref[...]) * (1.0 + mod[1:2]) + mod[0:1]).astype(BF16)

    hb_next = normed(0)
    for sub in range(n_sub):
        r = slice(sub * rows, (sub + 1) * rows)
        hb = hb_next
        proj = lambda w: jnp.dot(hb, w[...], preferred_element_type=F32)
        zg_ref[0, r, 0:split] = proj(wg_qkv).astype(zg_ref.dtype)
        if sub + 1 < n_sub:
            hb_next = normed(sub + 1)
        zg_ref[0, r, split:] = proj(wg_r).astype(zg_ref.dtype)
        zm_ref[0, r, 0:split] = proj(wm_qkv).astype(zm_ref.dtype)
        zm_ref[0, r, split:] = proj(wm_o).astype(zm_ref.dtype)
        s5 = proj(w_s5)
        for s in range(zs_ref.shape[1]):
            zs_ref[0, s, r, :] = s5[:, s * LANES:(s + 1) * LANES]
        zsm_ref[0, r, :] = proj(w_small)


def _inproj(x, mod, nw, weights, *, layer, tm):
    b, t, d = x.shape
    s5w = weights[4].shape[2]
    tok = lambda n: pl.BlockSpec((1, tm, n), lambda i, j: (i, j, 0))
    slabs = s5w // LANES
    return pl.pallas_call(
        _inproj_kernel,
        grid=(b, t // tm),
        in_specs=[tok(d), _mod_spec(mod, layer), _layer_spec(nw, layer)] + [_layer_spec(w, layer, True)
                                                                           for w in weights],
        out_specs=[tok(N_GLA), tok(N_ML), pl.BlockSpec((1, slabs, tm, LANES), lambda i, j: (i, 0, j, 0)),
                   tok(N_SMALL)],
        out_shape=[jax.ShapeDtypeStruct((b, t, N_GLA), BF16), jax.ShapeDtypeStruct((b, t, N_ML), BF16),
                   jax.ShapeDtypeStruct((b, slabs, t, LANES), F32),
                   jax.ShapeDtypeStruct((b, t, N_SMALL), F32)],
        compiler_params=_params(("parallel", "parallel")),
        name="in_proj",
    )(x, mod, nw, *weights)


def _block_diag_rows(a, width):
    lane_head = lax.broadcasted_iota(jnp.int32, a.shape, 1) // width
    zero = jnp.zeros_like(a)
    return jnp.concatenate([jnp.where(lane_head == h, a, zero) for h in range(HEADS)], axis=0)


def _scan_rows(x, period, op, fill):
    pos = lax.broadcasted_iota(jnp.int32, x.shape, 0) % period
    sh = 1
    while sh < period:
        x = op(x, jnp.where(pos >= sh, pltpu.roll(x, sh, 0), fill))
        sh *= 2
    return x


def _bcast_heads(tile, per_head):
    rows = tile.shape[0]
    full = [jnp.broadcast_to(tile[:, GATE_LANE + h:GATE_LANE + h + 1], (rows, LANES)) for h in range(HEADS)]
    if per_head == LANES:
        return jnp.concatenate(full, axis=1)
    assert 2 * per_head == LANES and HEADS % 2 == 0
    low = lax.broadcasted_iota(jnp.int32, (rows, LANES), 1) < per_head
    return jnp.concatenate([jnp.where(low, full[h], full[h + 1]) for h in range(0, HEADS, 2)], axis=1)


def _gla_levels(q, k, g, c):
    n = q.shape[0]
    row = lax.broadcasted_iota(jnp.int32, q.shape, 0)
    out = []
    hs = 1
    while 2 * hs <= n:
        blk = 2 * hs
        pos = row % blk
        upper = pos >= hs
        if hs == 1:
            x = jnp.where(upper, g, 0.0)
        elif hs == 2:
            g_prev = pltpu.roll(g, 1, 0)
            g_next = pltpu.roll(g, n - 1, 0)
            x = jnp.where(pos == 2, g, jnp.where(pos == 3, g + g_prev, jnp.where(pos == 0, g_next, 0.0)))
        elif hs % 8 != 0:
            c3 = c.reshape(n // blk, blk, c.shape[1])
            anchor = jnp.broadcast_to(c3[:, hs - 1:hs, :], c3.shape).reshape(c.shape)
            x = -jnp.abs(c - anchor)
        else:
            split = lambda a: a.reshape(n // blk, 2, hs, a.shape[1])
            c4, q4, k4 = split(c), split(q), split(k)
            anchor = c4[:, 0, hs - 1:hs, :]
            mixed = jnp.concatenate([k4[:, 0] * jnp.exp(anchor - c4[:, 0]), q4[:, 1] * jnp.exp(c4[:, 1] - anchor)],
                                    axis=1)
            out.append((mixed.reshape(q.shape), hs))
            hs = blk
            continue
        out.append((jnp.where(upper, q, k) * jnp.exp(x), hs))
        hs = blk
    return out


def _pair_block_diag(a):
    low = lax.broadcasted_iota(jnp.int32, a.shape, 1) < a.shape[1] // 2
    zero = jnp.zeros_like(a)
    return jnp.concatenate([jnp.where(low, a, zero), jnp.where(low, zero, a)], axis=0)


def _own_block(h):
    return slice(h * DK, (h + 1) * DK), slice(h * DV, (h + 1) * DV)


def _state_operand(st_ref):
    zero = jnp.zeros((DK, DV), BF16)
    return jnp.concatenate(
        [jnp.concatenate([st_ref[_own_block(h)].astype(BF16) if j == h else zero for j in range(HEADS)], axis=1)
         for h in range(HEADS)], axis=0)


def _gla_chunk(q, k, v, gr, g, c, nw, st_ref):
    L = q.shape[0]
    col = lax.broadcasted_iota(jnp.int32, (L, LANES), 1) % DK
    rowi = lax.broadcasted_iota(jnp.int32, (L, LANES), 0)
    qb, kb = q.astype(BF16), k.astype(BF16)
    levels = [(t.astype(BF16), hs) for t, hs in _gla_levels(q, k, g, c)]
    pairs = []
    for p in range(HK // LANES):
        ls = slice(p * LANES, (p + 1) * LANES)
        s = lax.dot_general(qb[:, ls], _pair_block_diag(kb[:, ls]), NT, preferred_element_type=F32)
        acc = jnp.where(rowi == col, s, 0.0)
        for tb, hs in levels:
            s = lax.dot_general(tb[:, ls], _pair_block_diag(tb[:, ls]), NT, preferred_element_type=F32)
            blk = 2 * hs
            valid = ((rowi // blk) == (col // blk)) & (rowi % blk >= hs) & (col % blk < hs)
            acc = acc + jnp.where(valid, s, 0.0)
        pairs.append(acc)
    scores = jnp.concatenate(pairs, axis=1)

    o = jnp.dot(scores.astype(BF16), _block_diag_rows(v, DV), preferred_element_type=F32)
    o = o + jnp.dot((q * jnp.exp(c)).astype(BF16), _state_operand(st_ref), preferred_element_type=F32)
    c_last = c[L - 1:L]
    kd = (k * jnp.exp(c_last - c)).astype(BF16)
    upd = lax.dot_general(kd, v, TN, preferred_element_type=F32)
    decay = jnp.broadcast_to(jnp.exp(c_last), (LANES, HK)).T
    for h in range(HEADS):
        blk = _own_block(h)
        st_ref[blk] = decay[blk[0], :] * st_ref[blk] + upd[blk]

    parts = []
    for h in range(HEADS):
        oh = o[:, h * DV:(h + 1) * DV]
        parts.append(oh * lax.rsqrt(jnp.mean(oh * oh, axis=-1, keepdims=True) + RMS_EPS))
    return jnp.concatenate(parts, axis=1) * nw * _silu(gr)


def _mlstm_chunk(q, k, v, og, u, bcum, cmax, mt_ref, nrm_ref, m_ref):
    L = q.shape[0]
    lane = lax.broadcasted_iota(jnp.int32, (L, LANES), 1)
    gate_lanes = (lane >= GATE_LANE) & (lane < GATE_LANE + HEADS)
    m_prev = m_ref[...]
    big_m = jnp.maximum(m_prev, cmax)
    m_t = bcum + big_m
    m_new = m_t[L - 1:L]
    b_last = bcum[L - 1:L]

    ut = jnp.concatenate([u, u], axis=0).T
    low = lax.broadcasted_iota(jnp.int32, (1, 2 * L), 1) < L
    urow = jnp.concatenate([jnp.where(low, ut[GATE_LANE + h:GATE_LANE + h + 1], ut[GATE_LANE + h + 1:GATE_LANE + h + 2])
                            for h in range(0, HEADS, 2)], axis=1)
    causal = (lax.broadcasted_iota(jnp.int32, (L, HK), 1) % DK) <= lax.broadcasted_iota(jnp.int32, (L, HK), 0)
    dmat = _bcast_heads(-big_m, DK) + urow
    pmat = jnp.where(causal, jnp.exp(jnp.where(causal, dmat, 0.0)), 0.0)

    qb = q.astype(BF16)
    qk_s = lax.dot_general(qb, _block_diag_rows(k.astype(BF16), DK), NT, preferred_element_type=F32)
    sm = (qk_s * pmat).astype(BF16)

    den_lane = (lax.broadcasted_iota(jnp.int32, (HK, LANES), 0) // DK + GATE_LANE
                == lax.broadcasted_iota(jnp.int32, (HK, LANES), 1))
    nrm_cols = jnp.where(den_lane, nrm_ref[...], 0.0).astype(BF16)
    inter = jnp.dot(qb, jnp.concatenate([_state_operand(mt_ref), nrm_cols], axis=1),
                    preferred_element_type=F32)
    v_ext = jnp.concatenate([_block_diag_rows(v, DV), den_lane.astype(BF16)], axis=1)
    intra = jnp.dot(sm, v_ext, preferred_element_type=F32)

    lw = jnp.where(gate_lanes, m_prev - big_m, 0.0)
    den = jnp.exp(lw) * inter[:, HV:] + intra[:, HV:]
    rden = 1.0 / jnp.maximum(jnp.abs(den), jnp.exp(-m_t))
    hout = (jnp.exp(_bcast_heads(lw, DV)) * inter[:, 0:HV] + intra[:, 0:HV]) * _bcast_heads(rden, DV)

    wk = jnp.where(gate_lanes, b_last - m_new + u, 0.0)
    cd = jnp.where(gate_lanes[0:1], b_last + m_prev - m_new, 0.0)
    kw = (k * jnp.exp(_bcast_heads(wk, DK))).astype(BF16)
    upd = lax.dot_general(kw, jnp.concatenate([v, jnp.ones((L, LANES), BF16)], axis=1), TN,
                          preferred_element_type=F32)
    decay = jnp.exp(cd)
    for h in range(HEADS):
        blk = _own_block(h)
        dh = jnp.broadcast_to(decay[:, GATE_LANE + h:GATE_LANE + h + 1], (DK, LANES))
        mt_ref[blk] = dh * mt_ref[blk] + upd[blk]
        nrm_ref[blk[0], :] = dh * nrm_ref[blk[0], :] + upd[blk[0], HV:]
    m_ref[...] = m_new
    return _sigmoid(og) * hout


def _mixers(zg_ref, zm_ref, zs_ref, a2_ref, ab_ref, gnw_ref, cw_ref, cb_ref, ib_ref, fb_ref,
            yg_ref, ym_ref, st_ref, mt_ref, nrm_ref, m_ref, ext_ref, n_chunks):
    L = CHUNK
    tt = n_chunks * L

    zs_a = zs_ref[:, 0:LANES]
    xg = jnp.dot(zs_a.astype(BF16), a2_ref[...], preferred_element_type=F32) + ab_ref[...]
    g = _log_sigmoid(xg) * (1.0 / GATE_NORM)
    c = _scan_rows(g, L, jnp.add, 0.0)

    qk_in = zm_ref[:, 0:2 * HK].astype(F32)
    ext_ref[8:8 + tt, :] = qk_in
    acc = jnp.zeros((tt, 2 * HK), F32) + cb_ref[...]
    for j in range(CONV):
        off = 8 - (CONV - 1) + j
        acc = acc + cw_ref[j:j + 1, :] * ext_ref[off:off + tt, :]
    ext_ref[0:8, :] = qk_in[tt - 8:tt]
    qk = _silu(acc)

    lane = lax.broadcasted_iota(jnp.int32, (tt, LANES), 1)
    gate_lanes = (lane >= GATE_LANE) & (lane < GATE_LANE + HEADS)
    ig = jnp.where(gate_lanes, zs_a + ib_ref[...], 0.0)
    lf = jnp.where(gate_lanes, _log_sigmoid(zs_ref[:, LANES:2 * LANES] + fb_ref[...]), 0.0)
    bcum = _scan_rows(lf, L, jnp.add, 0.0)
    u = ig - bcum
    cmax = _scan_rows(u, L, jnp.maximum, -jnp.inf)

    for ci in range(n_chunks):
        r = slice(ci * L, (ci + 1) * L)
        yg = _gla_chunk(zg_ref[r, 0:HK].astype(F32) * (DK ** -0.5), zg_ref[r, HK:2 * HK].astype(F32),
                        zg_ref[r, 2 * HK:2 * HK + HV], zg_ref[r, 2 * HK + HV:2 * HK + 2 * HV].astype(F32),
                        g[r], c[r], gnw_ref[...], st_ref)
        yg_ref[r, :] = yg.astype(yg_ref.dtype)
        ym = _mlstm_chunk(qk[r, 0:HK], qk[r, HK:2 * HK] * (DK ** -0.5), zm_ref[r, 2 * HK:2 * HK + HV],
                          zm_ref[r, 2 * HK + HV:2 * HK + 2 * HV].astype(F32), u[r], bcum[r], cmax[r],
                          mt_ref, nrm_ref, m_ref)
        ym_ref[r, :] = ym.astype(ym_ref.dtype)


def _mixer_kernel(zg_ref, zm_ref, zs_ref, a2_ref, ab_ref, gnw_ref, cw_ref, cb_ref, ib_ref, fb_ref,
                  yg_ref, ym_ref, st_ref, mt_ref, nrm_ref, m_ref, ext_ref, *, n_chunks):
    @pl.when(pl.program_id(1) == 0)
    def _():
        for ref in (st_ref, mt_ref, nrm_ref, m_ref):
            ref[...] = jnp.zeros_like(ref)
        ext_ref[0:8, :] = jnp.zeros((8, 2 * HK), F32)

    _mixers(zg_ref.at[0], zm_ref.at[0], zs_ref.at[0], a2_ref, ab_ref, gnw_ref, cw_ref, cb_ref, ib_ref, fb_ref,
            yg_ref.at[0], ym_ref.at[0], st_ref, mt_ref, nrm_ref, m_ref, ext_ref, n_chunks)


def _mixer(zg, zm, zs, a2p, ab, gnw, cw, cb, ibp, fbp, *, layer, tt):
    b, t, _ = zg.shape
    tok = lambda n: pl.BlockSpec((1, tt, n), lambda i, j: (i, j, 0))
    return pl.pallas_call(
        functools.partial(_mixer_kernel, n_chunks=tt // CHUNK),
        grid=(b, t // tt),
        in_specs=[tok(N_GLA), tok(N_ML), tok(N_SMALL)] + [_layer_spec(a, layer)
                                                           for a in (a2p, ab, gnw, cw, cb, ibp, fbp)],
        out_specs=[tok(HV), tok(HV)],
        out_shape=[jax.ShapeDtypeStruct((b, t, HV), BF16)] * 2,
        scratch_shapes=[pltpu.VMEM((HK, HV), F32), pltpu.VMEM((HK, HV), F32), pltpu.VMEM((HK, LANES), F32),
                        pltpu.VMEM((1, LANES), F32), pltpu.VMEM((8 + tt, 2 * HK), F32)],
        compiler_params=_params(("parallel", "arbitrary")),
        name="mixer",
    )(zg, zm, zs, a2p, ab, gnw, cw, cb, ibp, fbp)


def _s5_prep_kernel(lr_ref, li_ref, lrc_ref, lic_ref, dt_ref, btr_ref, bti_ref, ctr_ref, cti_ref, d_ref,
                    t_ref, w_ref, v_ref, pa_ref, pb_ref, *, n_pow):
    P = S5_STATE
    nb = S5_BLOCK
    dt = dt_ref[0]
    lr = jnp.minimum(lr_ref[0], -1e-4)
    li = li_ref[0]
    lrc = jnp.minimum(lrc_ref[0], -1e-4)
    lic = lic_ref[0]
    dtr = jnp.exp(dt)

    def power(lr_, li_, n):
        mag = jnp.exp(lr_ * dtr * n)
        ang = li_ * dtr * n
        return mag * jnp.cos(ang), mag * jnp.sin(ang)

    n_rows = jnp.where(lax.broadcasted_iota(jnp.int32, (nb + 8, 1), 0) < nb,
                       nb - 1 - lax.broadcasted_iota(jnp.int32, (nb + 8, 1), 0),
                       jnp.where(lax.broadcasted_iota(jnp.int32, (nb + 8, 1), 0) == nb, 1, nb)).astype(F32)
    row_r, row_i = power(lr, li, n_rows)
    col_r, col_i = power(lrc, lic, lax.broadcasted_iota(jnp.int32, (P, LANES), 1).astype(F32))

    ar1, ai1 = row_r[nb:nb + 1], row_i[nb:nb + 1]
    inv = 1.0 / (lr * lr + li * li)
    zr = ((ar1 - 1.0) * lr + ai1 * li) * inv
    zi = (ai1 * lr - (ar1 - 1.0) * li) * inv
    bbr = zr * btr_ref[0] - zi * bti_ref[0]
    bbi = zr * bti_ref[0] + zi * btr_ref[0]

    ctr, cti = ctr_ref[0], cti_ref[0]
    src = lax.broadcasted_iota(jnp.int32, (LANES, nb * S5_GROUP), 0)
    dst_t = lax.broadcasted_iota(jnp.int32, (LANES, nb * S5_GROUP), 1) // S5_GROUP

    def out_map(shift):
        expand = (src == dst_t + shift).astype(F32)
        ar = jnp.dot(col_r, expand, precision=HIGHEST, preferred_element_type=F32)
        ai = jnp.dot(col_i, expand, precision=HIGHEST, preferred_element_type=F32)
        return jnp.concatenate([ctr * ar - cti * ai, -(ctr * ai + cti * ar)], axis=0)

    v_ref[0] = out_map(1).T.astype(v_ref.dtype)
    base = jnp.dot(jnp.concatenate([bbr, bbi], axis=1), out_map(0), precision=HIGHEST,
                   preferred_element_type=F32)
    lane = lax.broadcasted_iota(jnp.int32, base.shape, 1)
    base = base + jnp.where(lane == lax.broadcasted_iota(jnp.int32, base.shape, 0), d_ref[0], 0.0)
    rows = [base]
    for s in range(1, nb):
        rows.append(jnp.where(lane >= s * S5_GROUP, pltpu.roll(base, s * S5_GROUP, 1), 0.0))
    t_ref[0] = jnp.concatenate(rows, axis=0).T.astype(t_ref.dtype)

    pick = (lax.broadcasted_iota(jnp.int32, (nb * S5_GROUP, nb + 8), 0) // S5_GROUP
            == lax.broadcasted_iota(jnp.int32, (nb * S5_GROUP, nb + 8), 1)).astype(F32)
    rep = lambda a: jnp.dot(pick, a, precision=HIGHEST, preferred_element_type=F32)
    ar, ai = rep(row_r), rep(row_i)
    bbr_t = jnp.concatenate([bbr] * nb, axis=0)
    bbi_t = jnp.concatenate([bbi] * nb, axis=0)
    wt = jnp.concatenate([ar * bbr_t - ai * bbi_t, ar * bbi_t + ai * bbr_t], axis=1).T
    w_ref[0] = jnp.concatenate([wt, wt[P:2 * P], wt[0:P]], axis=0).astype(w_ref.dtype)

    pr, pi = row_r[nb + 1:nb + 2], row_i[nb + 1:nb + 2]
    pa, pb = [], []
    for _ in range(n_pow):
        pa.append(jnp.concatenate([pr, pr], axis=1))
        pb.append(jnp.concatenate([-pi, pi], axis=1))
        pr, pi = pr * pr - pi * pi, 2.0 * pr * pi
    pa_ref[0] = jnp.concatenate(pa, axis=0)
    pb_ref[0] = jnp.concatenate(pb, axis=0)


def _s5_prep(lam_re, lam_im, log_step, b_re, b_im, c_re, c_im, d_skip, *, n_pow):
    nl, g, p = lam_re.shape
    nbg = S5_BLOCK * S5_GROUP
    bt = lambda a: jnp.swapaxes(a, 2, 3)
    ct = lambda a: jnp.tile(jnp.swapaxes(a, 2, 3), (1, 1, 1, S5_BLOCK))
    row = lambda a: a.reshape(nl, g, 1, p)
    colv = lambda a: a.reshape(nl, g, p, 1)
    dtile = jnp.tile(d_skip.reshape(nl, g, 1, S5_GROUP), (1, 1, 1, S5_BLOCK))
    spec = lambda *s: pl.BlockSpec((None, 1) + s, lambda l, i: (l, i) + (0,) * len(s))
    shape = lambda dt, *s: jax.ShapeDtypeStruct((nl, g) + s, dt)
    return pl.pallas_call(
        functools.partial(_s5_prep_kernel, n_pow=n_pow),
        grid=(nl, g),
        in_specs=[spec(1, p), spec(1, p), spec(p, 1), spec(p, 1), spec(1, 1),
                  spec(S5_GROUP, p), spec(S5_GROUP, p), spec(p, nbg), spec(p, nbg), spec(1, nbg)],
        out_specs=[spec(nbg, nbg), spec(4 * p, nbg), spec(nbg, 2 * p), spec(n_pow, 2 * p), spec(n_pow, 2 * p)],
        out_shape=[shape(BF16, nbg, nbg), shape(BF16, 4 * p, nbg), shape(BF16, nbg, 2 * p),
                   shape(F32, n_pow, 2 * p), shape(F32, n_pow, 2 * p)],
        compiler_params=_params(("parallel", "parallel")),
        name="s5_prep",
    )(row(lam_re), row(lam_im), colv(lam_re), colv(lam_im), log_step.reshape(nl, g, 1, 1),
      bt(b_re), bt(b_im), ct(c_re), ct(c_im), dtile)


S5_SCAN_GROUPS = 4


def _s5_kernel(u_ref, t_ref, w_ref, v_ref, pa_ref, pb_ref, y_ref,
               ut_ref, yt_ref, zt_ref, zw_ref, xin_ref, xc_ref, *, groups, n_pow):
    nb = S5_BLOCK
    p2 = 2 * S5_STATE
    slabs, tt = u_ref.shape[1], u_ref.shape[2]
    nblk = tt // nb

    @pl.when(pl.program_id(1) == 0)
    def _():
        xc_ref[...] = jnp.zeros_like(xc_ref)

    gps = S5_SCAN_GROUPS
    n_chunks = groups // gps
    chunks_per_slab = LANES // S5_GROUP // gps

    def relayout_in(s):
        for tl in range(nb):
            ut_ref[tl, s * LANES:(s + 1) * LANES, :] = (
                u_ref[0, s, pl.ds(tl, nblk, stride=nb), :].T.astype(ut_ref.dtype))

    def relayout_out(s):
        for tl in range(nb):
            y_ref[0, s, pl.ds(tl, nblk, stride=nb), :] = yt_ref[tl, s * LANES:(s + 1) * LANES, :].T

    def group_input(g):
        return ut_ref[:, g * S5_GROUP:(g + 1) * S5_GROUP, :].reshape(nb * S5_GROUP, nblk)

    def state_in(ci):
        for g in range(ci * gps, (ci + 1) * gps):
            z2 = jnp.dot(w_ref[g], group_input(g), preferred_element_type=F32)
            zt_ref[g * p2:(g + 1) * p2, :] = z2[0:p2]
            zw_ref[g * p2:(g + 1) * p2, :] = z2[p2:2 * p2]

    width = gps * p2
    row = lax.broadcasted_iota(jnp.int32, (nblk, width), 0)

    def scan(ci):
        l0 = ci * width
        x = zt_ref[pl.ds(l0, width), :].T
        xw = zw_ref[pl.ds(l0, width), :].T
        pa = jnp.concatenate([pa_ref[g] for g in range(ci * gps, (ci + 1) * gps)], axis=1)
        pb = jnp.concatenate([pb_ref[g] for g in range(ci * gps, (ci + 1) * gps)], axis=1)
        xc = xc_ref[0:1, pl.ds(l0, width)]
        xcw = xc_ref[1:2, pl.ds(l0, width)]
        first = row == 0
        x = x + jnp.where(first, xc * pa[0:1] + xcw * pb[0:1], 0.0)
        xw = xw + jnp.where(first, xcw * pa[0:1] - xc * pb[0:1], 0.0)
        for k in range(n_pow):
            sh = 1 << k
            if sh >= nblk:
                break
            xs, xws, m = pltpu.roll(x, sh, 0), pltpu.roll(xw, sh, 0), row >= sh
            x, xw = (x + jnp.where(m, xs * pa[k:k + 1] + xws * pb[k:k + 1], 0.0),
                     xw + jnp.where(m, xws * pa[k:k + 1] - xs * pb[k:k + 1], 0.0))
        x_in = jnp.where(row >= 1, pltpu.roll(x, 1, 0), xc)
        xc_ref[0:1, pl.ds(l0, width)] = x[nblk - 1:nblk]
        xc_ref[1:2, pl.ds(l0, width)] = xw[nblk - 1:nblk]
        xin_ref[pl.ds(l0, width), :] = x_in.T.astype(xin_ref.dtype)

    def block_out(ci):
        for g in range(ci * gps, (ci + 1) * gps):
            yt = jnp.dot(t_ref[g], group_input(g), preferred_element_type=F32)
            yt = yt + jnp.dot(v_ref[g], xin_ref[g * p2:(g + 1) * p2, :], preferred_element_type=F32)
            yt_ref[:, g * S5_GROUP:(g + 1) * S5_GROUP, :] = yt.reshape(nb, S5_GROUP, nblk)

    relayout_in(0)
    for step in range(n_chunks + 2):
        nxt = step // chunks_per_slab + 1
        if step % chunks_per_slab == 0 and nxt < slabs:
            relayout_in(nxt)
        if step < n_chunks:
            state_in(step)
        if 1 <= step <= n_chunks:
            scan(step - 1)
        if step >= 2:
            block_out(step - 2)
            if (step - 2) % chunks_per_slab == chunks_per_slab - 1:
                relayout_out((step - 2) // chunks_per_slab)


def _s5(u, tmat, wmat, vmat, pa, pb, *, layer, tt):
    b, slabs, t, _ = u.shape
    width = slabs * LANES
    groups = width // S5_GROUP
    assert groups % S5_SCAN_GROUPS == 0
    n_pow = pa.shape[2]
    nblk = tt // S5_BLOCK
    states = groups * 2 * S5_STATE
    tok = pl.BlockSpec((1, slabs, tt, LANES), lambda i, j: (i, 0, j, 0))
    return pl.pallas_call(
        functools.partial(_s5_kernel, groups=groups, n_pow=n_pow),
        grid=(b, t // tt),
        in_specs=[tok] + [_layer_spec(a, layer, True) for a in (tmat, wmat, vmat, pa, pb)],
        out_specs=tok,
        out_shape=jax.ShapeDtypeStruct((b, slabs, t, LANES), F32),
        scratch_shapes=[pltpu.VMEM((S5_BLOCK, width, nblk), BF16), pltpu.VMEM((S5_BLOCK, width, nblk), F32),
                        pltpu.VMEM((states, nblk), F32), pltpu.VMEM((states, nblk), F32),
                        pltpu.VMEM((states, nblk), BF16), pltpu.VMEM((8, states), F32)],
        compiler_params=_params(("parallel", "arbitrary")),
        name="s5",
    )(u, tmat, wmat, vmat, pa, pb)


def _merge_kernel(x_ref, mod_ref, n1_ref, wg_ref, yg_ref, ym_ref, ys_ref, gluw_ref, glub_ref, pg_ref, pm_ref,
                  ps_ref, gb_ref, wo_ref, nw_ref, o_ref, *, d):
    mod = mod_ref[0]
    rows = x_ref.shape[1] // SUBTILES
    for sub in range(SUBTILES):
        r = slice(sub * rows, (sub + 1) * rows)
        x = x_ref[0, r, :]
        hb = (_rms(x, n1_ref[...]) * (1.0 + mod[1:2]) + mod[0:1]).astype(BF16)
        s = _gelu_tanh(jnp.concatenate([ys_ref[0, k, r, :] for k in range(ys_ref.shape[1])], axis=1))
        s = s * _sigmoid(jnp.dot(s.astype(BF16), gluw_ref[...], preferred_element_type=F32) + glub_ref[...])
        merged = None
        for idx, (y, p) in enumerate(((yg_ref[0, r, :], pg_ref), (ym_ref[0, r, :], pm_ref), (s.astype(BF16), ps_ref))):
            cs = slice(idx * d, (idx + 1) * d)
            zg = jnp.dot(hb, wg_ref[:, cs], preferred_element_type=F32) + gb_ref[:, cs]
            term = (jnp.tanh(0.5 * zg) + 1.0) * jnp.dot(y, p[...], preferred_element_type=F32)
            merged = term if merged is None else merged + term
        y = jnp.dot((0.5 * merged).astype(BF16), wo_ref[...], preferred_element_type=F32)
        o_ref[0, r, :] = x + (1.0 + mod[2:3]) * _rms(y, nw_ref[...])


def _merge(x, mod, n1, wg, yg, ym, ys, gluw, glub, pg, pm, ps, gb, wo, nw, *, layer, tm):
    b, t, d = x.shape
    tok = lambda n: pl.BlockSpec((1, tm, n), lambda i, j: (i, j, 0))
    small = lambda a: _layer_spec(a, layer)
    big = lambda a: _layer_spec(a, layer, True)
    return pl.pallas_call(
        functools.partial(_merge_kernel, d=d),
        grid=(b, t // tm),
        in_specs=[tok(d), _mod_spec(mod, layer), small(n1), big(wg),
                  tok(HV), tok(HV), pl.BlockSpec((1, ys.shape[1], tm, LANES), lambda i, j: (i, 0, j, 0)),
                  big(gluw), small(glub), big(pg), big(pm), big(ps), small(gb), big(wo), small(nw)],
        out_specs=tok(d),
        out_shape=jax.ShapeDtypeStruct((b, t, d), F32),
        compiler_params=_params(("parallel", "parallel")),
        name="merge_out",
    )(x, mod, n1, wg, yg, ym, ys, gluw, glub, pg, pm, ps, gb, wo, nw)


def _ffn_kernel(x_ref, mod_ref, n1_ref, wi_ref, wo_ref, n2_ref, o_ref, *, hidden):
    mod = mod_ref[0]
    rows = x_ref.shape[1] // SUBTILES
    tile = lambda s: slice(s * rows, (s + 1) * rows)
    normed = lambda s: (_rms(x_ref[0, tile(s), :], n1_ref[...]) * (1.0 + mod[4:5]) + mod[3:4]).astype(BF16)

    def finish(s, y):
        o_ref[0, tile(s), :] = x_ref[0, tile(s), :] + (1.0 + mod[5:6]) * _rms(y, n2_ref[...])

    h_next, y_prev = normed(0), None
    for s in range(SUBTILES):
        h = h_next
        gate = jnp.dot(h, wi_ref[:, 0:hidden], preferred_element_type=F32)
        if s + 1 < SUBTILES:
            h_next = normed(s + 1)
        if y_prev is not None:
            finish(s - 1, y_prev)
        up = jnp.dot(h, wi_ref[:, hidden:2 * hidden], preferred_element_type=F32)
        act = (_silu(gate) * up).astype(BF16)
        y_prev = jnp.dot(act, wo_ref[...], preferred_element_type=F32)
    finish(SUBTILES - 1, y_prev)


def _ffn(x, mod, n1, wi, wo, n2, *, layer, tm):
    b, t, d = x.shape
    hidden = wo.shape[1]
    tok = pl.BlockSpec((1, tm, d), lambda i, j: (i, j, 0))
    return pl.pallas_call(
        functools.partial(_ffn_kernel, hidden=hidden),
        grid=(b, t // tm),
        in_specs=[tok, _mod_spec(mod, layer), _layer_spec(n1, layer), _layer_spec(wi, layer, True),
                  _layer_spec(wo, layer, True), _layer_spec(n2, layer)],
        out_specs=tok,
        out_shape=jax.ShapeDtypeStruct((b, t, d), F32),
        compiler_params=_params(("parallel", "parallel")),
        name="ffn",
    )(x, mod, n1, wi, wo, n2)


def _split_w_in(w_in, d, s5w):
    sizes = (HK, HK, HV, RANK, HV, HK, HK, HV, HEADS, HEADS, HV, s5w, 3 * d)
    offs, a = [], 0
    for s in sizes:
        offs.append(a)
        a += s
    def cols(i, j=None):
        j = i if j is None else j
        return w_in[..., offs[i]:offs[j] + sizes[j]].astype(BF16)

    zeros = lambda n: jnp.zeros(w_in.shape[:-1] + (n,), BF16)
    small = jnp.concatenate([cols(3), cols(8), zeros(LANES - RANK - HEADS),
                             zeros(GATE_LANE), cols(9), zeros(LANES - GATE_LANE - HEADS)], axis=-1)
    return (cols(0, 2), cols(4), cols(5, 7), cols(10), cols(11), small), cols(12)


def _lane_pad(vec, lane0):
    return jnp.zeros((vec.shape[0], 1, LANES), F32).at[:, 0, lane0:lane0 + vec.shape[1]].set(vec)


def kernel(x, c, ada_w, ada_b, pre1_w, post1_w, pre2_w, post2_w, w_in, gla_a2, gla_a_b, gla_norm_w, ml_conv_w, ml_conv_b, ml_i_b, ml_f_b, s5_lam_re, s5_lam_im, s5_log_step, s5_b_re, s5_b_im, s5_c_re, s5_c_im, s5_d, s5_glu_w, s5_glu_b, proj_gla, proj_ml, proj_s5, branch_gate_b, w_out, ffn_w_in, ffn_w_out):
    b, t, d = x.shape
    depth = ada_w.shape[0]
    s5w = s5_d.shape[1]
    assert RANK + HEADS <= GATE_LANE + HEADS <= LANES and GATE_LANE == RANK
    assert DV == LANES and DK % 8 == 0
    assert t % 256 == 0 and d % LANES == 0
    tm = 512 if t % 512 == 0 else 256
    tm_sub = SUBTILES * tm if t % (SUBTILES * tm) == 0 else tm
    tt = 1024 if t % 1024 == 0 else 256
    tt_s5 = 2048 if t % 2048 == 0 else t
    n_pow = max(1, (tt_s5 // S5_BLOCK - 1).bit_length())

    rows = lambda a: a.reshape(depth, 1, -1)
    bf = lambda a: a.astype(BF16)
    mod = _ada_mod(c, ada_w, ada_b).reshape(depth, b, 6, d)
    w_mix, w_gate = _split_w_in(w_in, d, s5w)
    a2p = jnp.zeros((depth, LANES, HK), BF16).at[:, 0:RANK].set(bf(gla_a2))
    ibp, fbp = _lane_pad(ml_i_b, GATE_LANE), _lane_pad(ml_f_b, GATE_LANE)
    s5_ops = _s5_prep(s5_lam_re, s5_lam_im, s5_log_step, s5_b_re, s5_b_im, s5_c_re, s5_c_im,
                      s5_d.reshape(depth, -1, S5_GROUP), n_pow=n_pow)
    glu_w, p_gla, p_ml, p_s5, wo, f_in, f_out = map(bf, (s5_glu_w, proj_gla, proj_ml, proj_s5, w_out,
                                                         ffn_w_in, ffn_w_out))
    for l in range(depth):
        zg, zm, zs5, zsm = _inproj(x, mod, rows(pre1_w), w_mix, layer=l, tm=tm_sub)
        y_gla, y_ml = _mixer(zg, zm, zsm, a2p, rows(gla_a_b), rows(gla_norm_w), ml_conv_w, rows(ml_conv_b),
                             ibp, fbp, layer=l, tt=tt)
        ys = _s5(zs5, *s5_ops, layer=l, tt=tt_s5)
        x = _merge(x, mod, rows(pre1_w), w_gate, y_gla, y_ml, ys, glu_w, rows(s5_glu_b), p_gla, p_ml, p_s5,
                   rows(branch_gate_b), wo, rows(post1_w), layer=l, tm=tm_sub)
        x = _ffn(x, mod, rows(pre2_w), f_in, f_out, rows(post2_w), layer=l, tm=tm_sub)
    return x
```

```python
import functools
import math

import jax
import jax.numpy as jnp
from jax import lax
from jax.experimental import pallas as pl
from jax.experimental.pallas import tpu as pltpu

HEADS = 4
DK = 64
DV = 128
RANK = 16
GATE_NORM = 16.0
CONV = 4
S5_GROUP = 16
S5_STATE = 64
CHUNK = 64
S5_BLOCK = 16
RMS_EPS = 1e-6
LANES = 128
GATE_LANE = 16
SUBTILES = 2
VMEM_LIMIT = 56 * 1024 * 1024

HK = HEADS * DK
HV = HEADS * DV
F32 = jnp.float32
BF16 = jnp.bfloat16
HIGHEST = lax.Precision.HIGHEST
NT = (((1,), (1,)), ((), ()))
TN = (((0,), (0,)), ((), ()))


def _params(sem):
    return pltpu.CompilerParams(dimension_semantics=sem, vmem_limit_bytes=VMEM_LIMIT)


def _layer_spec(a, layer, single_buffer=False):
    nd = a.ndim - 1
    kw = dict(pipeline_mode=pl.Buffered(1)) if single_buffer else {}
    return pl.BlockSpec((None,) + a.shape[1:], lambda *_: (layer,) + (0,) * nd, **kw)


def _mod_spec(mod, layer):
    return pl.BlockSpec((None, 1) + mod.shape[2:], lambda i, j: (layer, i, 0, 0))


def _sigmoid(x):
    return 0.5 * jnp.tanh(0.5 * x) + 0.5


def _silu(x):
    return x * _sigmoid(x)


def _log_sigmoid(x):
    return jnp.minimum(x, 0.0) - jnp.log(1.0 + jnp.exp(-jnp.abs(x)))


def _gelu_tanh(x):
    c = math.sqrt(2.0 / math.pi)
    return 0.5 * x * (1.0 + jnp.tanh(c * (x + 0.044715 * (x * x * x))))


def _rms(x, w):
    ms = jnp.mean(x * x, axis=-1, keepdims=True)
    return x * lax.rsqrt(ms + RMS_EPS) * w


def _ada_kernel(c_ref, w_ref, b_ref, o_ref):
    ca = _silu(c_ref[...]).astype(BF16)
    o_ref[0] = jnp.dot(ca, w_ref[0].astype(BF16), preferred_element_type=F32) + b_ref[0]


def _ada_mod(c, ada_w, ada_b):
    nl, d, n = ada_w.shape
    b = c.shape[0]
    tn = 1024
    return pl.pallas_call(
        _ada_kernel,
        grid=(nl, n // tn),
        in_specs=[pl.BlockSpec((b, d), lambda l, j: (0, 0)),
                  pl.BlockSpec((1, d, tn), lambda l, j: (l, 0, j)),
                  pl.BlockSpec((1, 1, tn), lambda l, j: (l, 0, j))],
        out_specs=pl.BlockSpec((1, b, tn), lambda l, j: (l, 0, j)),
        out_shape=jax.ShapeDtypeStruct((nl, b, n), F32),
        compiler_params=_params(("arbitrary", "arbitrary")),
        name="ada_mod",
    )(c, ada_w, ada_b.reshape(nl, 1, n))


N_GLA = 2 * HK + 2 * HV
N_ML = 2 * HK + 2 * HV
N_SMALL = 2 * LANES


def _inproj_kernel(x_ref, mod_ref, nw_ref, wg_qkv, wg_r, wm_qkv, wm_o, w_s5, w_small,
                   zg_ref, zm_ref, zs_ref, zsm_ref):
    mod = mod_ref[0]
    split = 2 * HK + HV
    n_sub = SUBTILES
    rows = x_ref.shape[1] // n_sub

    def normed(sub):
        x = x_ref[0, sub * rows:(sub + 1) * rows, :]
        return (_rms(x, nw_ref[...]) * (1.0 + mod[1:2]) + mod[0:1]).astype(BF16)

    hb_next = normed(0)
    for sub in range(n_sub):
        r = slice(sub * rows, (sub + 1) * rows)
        hb = hb_next
        proj = lambda w: jnp.dot(hb, w[...], preferred_element_type=F32)
        zg_ref[0, r, 0:split] = proj(wg_qkv).astype(zg_ref.dtype)
        if sub + 1 < n_sub:
            hb_next = normed(sub + 1)
        zg_ref[0, r, split:] = proj(wg_r).astype(zg_ref.dtype)
        zm_ref[0, r, 0:split] = proj(wm_qkv).astype(zm_ref.dtype)
        zm_ref[0, r, split:] = proj(wm_o).astype(zm_ref.dtype)
        s5 = proj(w_s5)
        for s in range(zs_ref.shape[1]):
            zs_ref[0, s, r, :] = s5[:, s * LANES:(s + 1) * LANES]
        zsm_ref[0, r, :] = proj(w_small)


def _inproj(x, mod, nw, weights, *, layer, tm):
    b, t, d = x.shape
    s5w = weights[4].shape[2]
    tok = lambda n: pl.BlockSpec((1, tm, n), lambda i, j: (i, j, 0))
    slabs = s5w // LANES
    return pl.pallas_call(
        _inproj_kernel,
        grid=(b, t // tm),
        in_specs=[tok(d), _mod_spec(mod, layer), _layer_spec(nw, layer)] + [_layer_spec(w, layer, True)
                                                                           for w in weights],
        out_specs=[tok(N_GLA), tok(N_ML), pl.BlockSpec((1, slabs, tm, LANES), lambda i, j: (i, 0, j, 0)),
                   tok(N_SMALL)],
        out_shape=[jax.ShapeDtypeStruct((b, t, N_GLA), BF16), jax.ShapeDtypeStruct((b, t, N_ML), BF16),
                   jax.ShapeDtypeStruct((b, slabs, t, LANES), F32),
                   jax.ShapeDtypeStruct((b, t, N_SMALL), F32)],
        compiler_params=_params(("parallel", "parallel")),
        name="in_proj",
    )(x, mod, nw, *weights)


def _block_diag_rows(a, width):
    lane_head = lax.broadcasted_iota(jnp.int32, a.shape, 1) // width
    zero = jnp.zeros_like(a)
    return jnp.concatenate([jnp.where(lane_head == h, a, zero) for h in range(HEADS)], axis=0)


def _scan_rows(x, period, op, fill):
    pos = lax.broadcasted_iota(jnp.int32, x.shape, 0) % period
    sh = 1
    while sh < period:
        x = op(x, jnp.where(pos >= sh, pltpu.roll(x, sh, 0), fill))
        sh *= 2
    return x


def _bcast_heads(tile, per_head):
    rows = tile.shape[0]
    full = [jnp.broadcast_to(tile[:, GATE_LANE + h:GATE_LANE + h + 1], (rows, LANES)) for h in range(HEADS)]
    if per_head == LANES:
        return jnp.concatenate(full, axis=1)
    assert 2 * per_head == LANES and HEADS % 2 == 0
    low = lax.broadcasted_iota(jnp.int32, (rows, LANES), 1) < per_head
    return jnp.concatenate([jnp.where(low, full[h], full[h + 1]) for h in range(0, HEADS, 2)], axis=1)


def _gla_levels(q, k, g, c):
    n = q.shape[0]
    row = lax.broadcasted_iota(jnp.int32, q.shape, 0)
    out = []
    hs = 1
    while 2 * hs <= n:
        blk = 2 * hs
        pos = row % blk
        upper = pos >= hs
        if hs == 1:
            x = jnp.where(upper, g, 0.0)
        elif hs == 2:
            g_prev = pltpu.roll(g, 1, 0)
            g_next = pltpu.roll(g, n - 1, 0)
            x = jnp.where(pos == 2, g, jnp.where(pos == 3, g + g_prev, jnp.where(pos == 0, g_next, 0.0)))
        elif hs % 8 != 0:
            c3 = c.reshape(n // blk, blk, c.shape[1])
            anchor = jnp.broadcast_to(c3[:, hs - 1:hs, :], c3.shape).reshape(c.shape)
            x = -jnp.abs(c - anchor)
        else:
            split = lambda a: a.reshape(n // blk, 2, hs, a.shape[1])
            c4, q4, k4 = split(c), split(q), split(k)
            anchor = c4[:, 0, hs - 1:hs, :]
            mixed = jnp.concatenate([k4[:, 0] * jnp.exp(anchor - c4[:, 0]), q4[:, 1] * jnp.exp(c4[:, 1] - anchor)],
                                    axis=1)
            out.append((mixed.reshape(q.shape), hs))
            hs = blk
            continue
        out.append((jnp.where(upper, q, k) * jnp.exp(x), hs))
        hs = blk
    return out


def _pair_block_diag(a):
    low = lax.broadcasted_iota(jnp.int32, a.shape, 1) < a.shape[1] // 2
    zero = jnp.zeros_like(a)
    return jnp.concatenate([jnp.where(low, a, zero), jnp.where(low, zero, a)], axis=0)


def _own_block(h):
    return slice(h * DK, (h + 1) * DK), slice(h * DV, (h + 1) * DV)


def _state_operand(st_ref):
    zero = jnp.zeros((DK, DV), BF16)
    return jnp.concatenate(
        [jnp.concatenate([st_ref[_own_block(h)].astype(BF16) if j == h else zero for j in range(HEADS)], axis=1)
         for h in range(HEADS)], axis=0)


def _gla_chunk(q, k, v, gr, g, c, nw, st_ref):
    L = q.shape[0]
    col = lax.broadcasted_iota(jnp.int32, (L, LANES), 1) % DK
    rowi = lax.broadcasted_iota(jnp.int32, (L, LANES), 0)
    qb, kb = q.astype(BF16), k.astype(BF16)
    levels = [(t.astype(BF16), hs) for t, hs in _gla_levels(q, k, g, c)]
    pairs = []
    for p in range(HK // LANES):
        ls = slice(p * LANES, (p + 1) * LANES)
        s = lax.dot_general(qb[:, ls], _pair_block_diag(kb[:, ls]), NT, preferred_element_type=F32)
        acc = jnp.where(rowi == col, s, 0.0)
        for tb, hs in levels:
            s = lax.dot_general(tb[:, ls], _pair_block_diag(tb[:, ls]), NT, preferred_element_type=F32)
            blk = 2 * hs
            valid = ((rowi // blk) == (col // blk)) & (rowi % blk >= hs) & (col % blk < hs)
            acc = acc + jnp.where(valid, s, 0.0)
        pairs.append(acc)
    scores = jnp.concatenate(pairs, axis=1)

    o = jnp.dot(scores.astype(BF16), _block_diag_rows(v, DV), preferred_element_type=F32)
    o = o + jnp.dot((q * jnp.exp(c)).astype(BF16), _state_operand(st_ref), preferred_element_type=F32)
    c_last = c[L - 1:L]
    kd = (k * jnp.exp(c_last - c)).astype(BF16)
    upd = lax.dot_general(kd, v, TN, preferred_element_type=F32)
    decay = jnp.broadcast_to(jnp.exp(c_last), (LANES, HK)).T
    for h in range(HEADS):
        blk = _own_block(h)
        st_ref[blk] = decay[blk[0], :] * st_ref[blk] + upd[blk]

    parts = []
    for h in range(HEADS):
        oh = o[:, h * DV:(h + 1) * DV]
        parts.append(oh * lax.rsqrt(jnp.mean(oh * oh, axis=-1, keepdims=True) + RMS_EPS))
    return jnp.concatenate(parts, axis=1) * nw * _silu(gr)


def _mlstm_chunk(q, k, v, og, u, bcum, cmax, mt_ref, nrm_ref, m_ref):
    L = q.shape[0]
    lane = lax.broadcasted_iota(jnp.int32, (L, LANES), 1)
    gate_lanes = (lane >= GATE_LANE) & (lane < GATE_LANE + HEADS)
    m_prev = m_ref[...]
    big_m = jnp.maximum(m_prev, cmax)
    m_t = bcum + big_m
    m_new = m_t[L - 1:L]
    b_last = bcum[L - 1:L]

    ut = jnp.concatenate([u, u], axis=0).T
    low = lax.broadcasted_iota(jnp.int32, (1, 2 * L), 1) < L
    urow = jnp.concatenate([jnp.where(low, ut[GATE_LANE + h:GATE_LANE + h + 1], ut[GATE_LANE + h + 1:GATE_LANE + h + 2])
                            for h in range(0, HEADS, 2)], axis=1)
    causal = (lax.broadcasted_iota(jnp.int32, (L, HK), 1) % DK) <= lax.broadcasted_iota(jnp.int32, (L, HK), 0)
    dmat = _bcast_heads(-big_m, DK) + urow
    pmat = jnp.where(causal, jnp.exp(jnp.where(causal, dmat, 0.0)), 0.0)

    qb = q.astype(BF16)
    qk_s = lax.dot_general(qb, _block_diag_rows(k.astype(BF16), DK), NT, preferred_element_type=F32)
    sm = (qk_s * pmat).astype(BF16)

    den_lane = (lax.broadcasted_iota(jnp.int32, (HK, LANES), 0) // DK + GATE_LANE
                == lax.broadcasted_iota(jnp.int32, (HK, LANES), 1))
    nrm_cols = jnp.where(den_lane, nrm_ref[...], 0.0).astype(BF16)
    inter = jnp.dot(qb, jnp.concatenate([_state_operand(mt_ref), nrm_cols], axis=1),
                    preferred_element_type=F32)
    v_ext = jnp.concatenate([_block_diag_rows(v, DV), den_lane.astype(BF16)], axis=1)
    intra = jnp.dot(sm, v_ext, preferred_element_type=F32)

    lw = jnp.where(gate_lanes, m_prev - big_m, 0.0)
    den = jnp.exp(lw) * inter[:, HV:] + intra[:, HV:]
    rden = 1.0 / jnp.maximum(jnp.abs(den), jnp.exp(-m_t))
    hout = (jnp.exp(_bcast_heads(lw, DV)) * inter[:, 0:HV] + intra[:, 0:HV]) * _bcast_heads(rden, DV)

    wk = jnp.where(gate_lanes, b_last - m_new + u, 0.0)
    cd = jnp.where(gate_lanes[0:1], b_last + m_prev - m_new, 0.0)
    kw = (k * jnp.exp(_bcast_heads(wk, DK))).astype(BF16)
    upd = lax.dot_general(kw, jnp.concatenate([v, jnp.ones((L, LANES), BF16)], axis=1), TN,
                          preferred_element_type=F32)
    decay = jnp.exp(cd)
    for h in range(HEADS):
        blk = _own_block(h)
        dh = jnp.broadcast_to(decay[:, GATE_LANE + h:GATE_LANE + h + 1], (DK, LANES))
        mt_ref[blk] = dh * mt_ref[blk] + upd[blk]
        nrm_ref[blk[0], :] = dh * nrm_ref[blk[0], :] + upd[blk[0], HV:]
    m_ref[...] = m_new
    return _sigmoid(og) * hout


def _mixers(zg_ref, zm_ref, zs_ref, a2_ref, ab_ref, gnw_ref, cw_ref, cb_ref, ib_ref, fb_ref,
            yg_ref, ym_ref, st_ref, mt_ref, nrm_ref, m_ref, ext_ref, n_chunks):
    L = CHUNK
    tt = n_chunks * L

    zs_a = zs_ref[:, 0:LANES]
    xg = jnp.dot(zs_a.astype(BF16), a2_ref[...], preferred_element_type=F32) + ab_ref[...]
    g = _log_sigmoid(xg) * (1.0 / GATE_NORM)
    c = _scan_rows(g, L, jnp.add, 0.0)

    qk_in = zm_ref[:, 0:2 * HK].astype(F32)
    ext_ref[8:8 + tt, :] = qk_in
    acc = jnp.zeros((tt, 2 * HK), F32) + cb_ref[...]
    for j in range(CONV):
        off = 8 - (CONV - 1) + j
        acc = acc + cw_ref[j:j + 1, :] * ext_ref[off:off + tt, :]
    ext_ref[0:8, :] = qk_in[tt - 8:tt]
    qk = _silu(acc)

    lane = lax.broadcasted_iota(jnp.int32, (tt, LANES), 1)
    gate_lanes = (lane >= GATE_LANE) & (lane < GATE_LANE + HEADS)
    ig = jnp.where(gate_lanes, zs_a + ib_ref[...], 0.0)
    lf = jnp.where(gate_lanes, _log_sigmoid(zs_ref[:, LANES:2 * LANES] + fb_ref[...]), 0.0)
    bcum = _scan_rows(lf, L, jnp.add, 0.0)
    u = ig - bcum
    cmax = _scan_rows(u, L, jnp.maximum, -jnp.inf)

    for ci in range(n_chunks):
        r = slice(ci * L, (ci + 1) * L)
        yg = _gla_chunk(zg_ref[r, 0:HK].astype(F32) * (DK ** -0.5), zg_ref[r, HK:2 * HK].astype(F32),
                        zg_ref[r, 2 * HK:2 * HK + HV], zg_ref[r, 2 * HK + HV:2 * HK + 2 * HV].astype(F32),
                        g[r], c[r], gnw_ref[...], st_ref)
        yg_ref[r, :] = yg.astype(yg_ref.dtype)
        ym = _mlstm_chunk(qk[r, 0:HK], qk[r, HK:2 * HK] * (DK ** -0.5), zm_ref[r, 2 * HK:2 * HK + HV],
                          zm_ref[r, 2 * HK + HV:2 * HK + 2 * HV].astype(F32), u[r], bcum[r], cmax[r],
                          mt_ref, nrm_ref, m_ref)
        ym_ref[r, :] = ym.astype(ym_ref.dtype)


def _mixer_kernel(zg_ref, zm_ref, zs_ref, a2_ref, ab_ref, gnw_ref, cw_ref, cb_ref, ib_ref, fb_ref,
                  yg_ref, ym_ref, st_ref, mt_ref, nrm_ref, m_ref, ext_ref, *, n_chunks):
    @pl.when(pl.program_id(1) == 0)
    def _():
        for ref in (st_ref, mt_ref, nrm_ref, m_ref):
            ref[...] = jnp.zeros_like(ref)
        ext_ref[0:8, :] = jnp.zeros((8, 2 * HK), F32)

    _mixers(zg_ref.at[0], zm_ref.at[0], zs_ref.at[0], a2_ref, ab_ref, gnw_ref, cw_ref, cb_ref, ib_ref, fb_ref,
            yg_ref.at[0], ym_ref.at[0], st_ref, mt_ref, nrm_ref, m_ref, ext_ref, n_chunks)


def _mixer(zg, zm, zs, a2p, ab, gnw, cw, cb, ibp, fbp, *, layer, tt):
    b, t, _ = zg.shape
    tok = lambda n: pl.BlockSpec((1, tt, n), lambda i, j: (i, j, 0))
    return pl.pallas_call(
        functools.partial(_mixer_kernel, n_chunks=tt // CHUNK),
        grid=(b, t // tt),
        in_specs=[tok(N_GLA), tok(N_ML), tok(N_SMALL)] + [_layer_spec(a, layer)
                                                           for a in (a2p, ab, gnw, cw, cb, ibp, fbp)],
        out_specs=[tok(HV), tok(HV)],
        out_shape=[jax.ShapeDtypeStruct((b, t, HV), BF16)] * 2,
        scratch_shapes=[pltpu.VMEM((HK, HV), F32), pltpu.VMEM((HK, HV), F32), pltpu.VMEM((HK, LANES), F32),
                        pltpu.VMEM((1, LANES), F32), pltpu.VMEM((8 + tt, 2 * HK), F32)],
        compiler_params=_params(("parallel", "arbitrary")),
        name="mixer",
    )(zg, zm, zs, a2p, ab, gnw, cw, cb, ibp, fbp)


def _s5_prep_kernel(lr_ref, li_ref, lrc_ref, lic_ref, dt_ref, btr_ref, bti_ref, ctr_ref, cti_ref, d_ref,
                    t_ref, w_ref, v_ref, pa_ref, pb_ref, *, n_pow):
    P = S5_STATE
    nb = S5_BLOCK
    dt = dt_ref[0]
    lr = jnp.minimum(lr_ref[0], -1e-4)
    li = li_ref[0]
    lrc = jnp.minimum(lrc_ref[0], -1e-4)
    lic = lic_ref[0]
    dtr = jnp.exp(dt)

    def power(lr_, li_, n):
        mag = jnp.exp(lr_ * dtr * n)
        ang = li_ * dtr * n
        return mag * jnp.cos(ang), mag * jnp.sin(ang)

    n_rows = jnp.where(lax.broadcasted_iota(jnp.int32, (nb + 8, 1), 0) < nb,
                       nb - 1 - lax.broadcasted_iota(jnp.int32, (nb + 8, 1), 0),
                       jnp.where(lax.broadcasted_iota(jnp.int32, (nb + 8, 1), 0) == nb, 1, nb)).astype(F32)
    row_r, row_i = power(lr, li, n_rows)
    col_r, col_i = power(lrc, lic, lax.broadcasted_iota(jnp.int32, (P, LANES), 1).astype(F32))

    ar1, ai1 = row_r[nb:nb + 1], row_i[nb:nb + 1]
    inv = 1.0 / (lr * lr + li * li)
    zr = ((ar1 - 1.0) * lr + ai1 * li) * inv
    zi = (ai1 * lr - (ar1 - 1.0) * li) * inv
    bbr = zr * btr_ref[0] - zi * bti_ref[0]
    bbi = zr * bti_ref[0] + zi * btr_ref[0]

    ctr, cti = ctr_ref[0], cti_ref[0]
    src = lax.broadcasted_iota(jnp.int32, (LANES, nb * S5_GROUP), 0)
    dst_t = lax.broadcasted_iota(jnp.int32, (LANES, nb * S5_GROUP), 1) // S5_GROUP

    def out_map(shift):
        expand = (src == dst_t + shift).astype(F32)
        ar = jnp.dot(col_r, expand, precision=HIGHEST, preferred_element_type=F32)
        ai = jnp.dot(col_i, expand, precision=HIGHEST, preferred_element_type=F32)
        return jnp.concatenate([ctr * ar - cti * ai, -(ctr * ai + cti * ar)], axis=0)

    v_ref[0] = out_map(1).T.astype(v_ref.dtype)
    base = jnp.dot(jnp.concatenate([bbr, bbi], axis=1), out_map(0), precision=HIGHEST,
                   preferred_element_type=F32)
    lane = lax.broadcasted_iota(jnp.int32, base.shape, 1)
    base = base + jnp.where(lane == lax.broadcasted_iota(jnp.int32, base.shape, 0), d_ref[0], 0.0)
    rows = [base]
    for s in range(1, nb):
        rows.append(jnp.where(lane >= s * S5_GROUP, pltpu.roll(base, s * S5_GROUP, 1), 0.0))
    t_ref[0] = jnp.concatenate(rows, axis=0).T.astype(t_ref.dtype)

    pick = (lax.broadcasted_iota(jnp.int32, (nb * S5_GROUP, nb + 8), 0) // S5_GROUP
            == lax.broadcasted_iota(jnp.int32, (nb * S5_GROUP, nb + 8), 1)).astype(F32)
    rep = lambda a: jnp.dot(pick, a, precision=HIGHEST, preferred_element_type=F32)
    ar, ai = rep(row_r), rep(row_i)
    bbr_t = jnp.concatenate([bbr] * nb, axis=0)
    bbi_t = jnp.concatenate([bbi] * nb, axis=0)
    wt = jnp.concatenate([ar * bbr_t - ai * bbi_t, ar * bbi_t + ai * bbr_t], axis=1).T
    w_ref[0] = jnp.concatenate([wt, wt[P:2 * P], wt[0:P]], axis=0).astype(w_ref.dtype)

    pr, pi = row_r[nb + 1:nb + 2], row_i[nb + 1:nb + 2]
    pa, pb = [], []
    for _ in range(n_pow):
        pa.append(jnp.concatenate([pr, pr], axis=1))
        pb.append(jnp.concatenate([-pi, pi], axis=1))
        pr, pi = pr * pr - pi * pi, 2.0 * pr * pi
    pa_ref[0] = jnp.concatenate(pa, axis=0)
    pb_ref[0] = jnp.concatenate(pb, axis=0)


def _s5_prep(lam_re, lam_im, log_step, b_re, b_im, c_re, c_im, d_skip, *, n_pow):
    nl, g, p = lam_re.shape
    nbg = S5_BLOCK * S5_GROUP
    bt = lambda a: jnp.swapaxes(a, 2, 3)
    ct = lambda a: jnp.tile(jnp.swapaxes(a, 2, 3), (1, 1, 1, S5_BLOCK))
    row = lambda a: a.reshape(nl, g, 1, p)
    colv = lambda a: a.reshape(nl, g, p, 1)
    dtile = jnp.tile(d_skip.reshape(nl, g, 1, S5_GROUP), (1, 1, 1, S5_BLOCK))
    spec = lambda *s: pl.BlockSpec((None, 1) + s, lambda l, i: (l, i) + (0,) * len(s))
    shape = lambda dt, *s: jax.ShapeDtypeStruct((nl, g) + s, dt)
    return pl.pallas_call(
        functools.partial(_s5_prep_kernel, n_pow=n_pow),
        grid=(nl, g),
        in_specs=[spec(1, p), spec(1, p), spec(p, 1), spec(p, 1), spec(1, 1),
                  spec(S5_GROUP, p), spec(S5_GROUP, p), spec(p, nbg), spec(p, nbg), spec(1, nbg)],
        out_specs=[spec(nbg, nbg), spec(4 * p, nbg), spec(nbg, 2 * p), spec(n_pow, 2 * p), spec(n_pow, 2 * p)],
        out_shape=[shape(BF16, nbg, nbg), shape(BF16, 4 * p, nbg), shape(BF16, nbg, 2 * p),
                   shape(F32, n_pow, 2 * p), shape(F32, n_pow, 2 * p)],
        compiler_params=_params(("parallel", "parallel")),
        name="s5_prep",
    )(row(lam_re), row(lam_im), colv(lam_re), colv(lam_im), log_step.reshape(nl, g, 1, 1),
      bt(b_re), bt(b_im), ct(c_re), ct(c_im), dtile)


S5_SCAN_GROUPS = 4


def _s5_kernel(u_ref, t_ref, w_ref, v_ref, pa_ref, pb_ref, y_ref,
               ut_ref, yt_ref, zt_ref, zw_ref, xin_ref, xc_ref, *, groups, n_pow):
    nb = S5_BLOCK
    p2 = 2 * S5_STATE
    slabs, tt = u_ref.shape[1], u_ref.shape[2]
    nblk = tt // nb

    @pl.when(pl.program_id(1) == 0)
    def _():
        xc_ref[...] = jnp.zeros_like(xc_ref)

    gps = S5_SCAN_GROUPS
    n_chunks = groups // gps
    chunks_per_slab = LANES // S5_GROUP // gps

    def relayout_in(s):
        for tl in range(nb):
            ut_ref[tl, s * LANES:(s + 1) * LANES, :] = (
                u_ref[0, s, pl.ds(tl, nblk, stride=nb), :].T.astype(ut_ref.dtype))

    def relayout_out(s):
        for tl in range(nb):
            y_ref[0, s, pl.ds(tl, nblk, stride=nb), :] = yt_ref[tl, s * LANES:(s + 1) * LANES, :].T

    def group_input(g):
        return ut_ref[:, g * S5_GROUP:(g + 1) * S5_GROUP, :].reshape(nb * S5_GROUP, nblk)

    def state_in(ci):
        for g in range(ci * gps, (ci + 1) * gps):
            z2 = jnp.dot(w_ref[g], group_input(g), preferred_element_type=F32)
            zt_ref[g * p2:(g + 1) * p2, :] = z2[0:p2]
            zw_ref[g * p2:(g + 1) * p2, :] = z2[p2:2 * p2]

    width = gps * p2
    row = lax.broadcasted_iota(jnp.int32, (nblk, width), 0)

    def scan(ci):
        l0 = ci * width
        x = zt_ref[pl.ds(l0, width), :].T
        xw = zw_ref[pl.ds(l0, width), :].T
        pa = jnp.concatenate([pa_ref[g] for g in range(ci * gps, (ci + 1) * gps)], axis=1)
        pb = jnp.concatenate([pb_ref[g] for g in range(ci * gps, (ci + 1) * gps)], axis=1)
        xc = xc_ref[0:1, pl.ds(l0, width)]
        xcw = xc_ref[1:2, pl.ds(l0, width)]
        first = row == 0
        x = x + jnp.where(first, xc * pa[0:1] + xcw * pb[0:1], 0.0)
        xw = xw + jnp.where(first, xcw * pa[0:1] - xc * pb[0:1], 0.0)
        for k in range(n_pow):
            sh = 1 << k
            if sh >= nblk:
                break
            xs, xws, m = pltpu.roll(x, sh, 0), pltpu.roll(xw, sh, 0), row >= sh
            x, xw = (x + jnp.where(m, xs * pa[k:k + 1] + xws * pb[k:k + 1], 0.0),
                     xw + jnp.where(m, xws * pa[k:k + 1] - xs * pb[k:k + 1], 0.0))
        x_in = jnp.where(row >= 1, pltpu.roll(x, 1, 0), xc)
        xc_ref[0:1, pl.ds(l0, width)] = x[nblk - 1:nblk]
        xc_ref[1:2, pl.ds(l0, width)] = xw[nblk - 1:nblk]
        xin_ref[pl.ds(l0, width), :] = x_in.T.astype(xin_ref.dtype)

    def block_out(ci):
        for g in range(ci * gps, (ci + 1) * gps):
            yt = jnp.dot(t_ref[g], group_input(g), preferred_element_type=F32)
            yt = yt + jnp.dot(v_ref[g], xin_ref[g * p2:(g + 1) * p2, :], preferred_element_type=F32)
            yt_ref[:, g * S5_GROUP:(g + 1) * S5_GROUP, :] = yt.reshape(nb, S5_GROUP, nblk)

    relayout_in(0)
    for step in range(n_chunks + 2):
        nxt = step // chunks_per_slab + 1
        if step % chunks_per_slab == 0 and nxt < slabs:
            relayout_in(nxt)
        if step < n_chunks:
            state_in(step)
        if 1 <= step <= n_chunks:
            scan(step - 1)
        if step >= 2:
            block_out(step - 2)
            if (step - 2) % chunks_per_slab == chunks_per_slab - 1:
                relayout_out((step - 2) // chunks_per_slab)


def _s5(u, tmat, wmat, vmat, pa, pb, *, layer, tt):
    b, slabs, t, _ = u.shape
    width = slabs * LANES
    groups = width // S5_GROUP
    assert groups % S5_SCAN_GROUPS == 0
    n_pow = pa.shape[2]
    nblk = tt // S5_BLOCK
    states = groups * 2 * S5_STATE
    tok = pl.BlockSpec((1, slabs, tt, LANES), lambda i, j: (i, 0, j, 0))
    return pl.pallas_call(
        functools.partial(_s5_kernel, groups=groups, n_pow=n_pow),
        grid=(b, t // tt),
        in_specs=[tok] + [_layer_spec(a, layer, True) for a in (tmat, wmat, vmat, pa, pb)],
        out_specs=tok,
        out_shape=jax.ShapeDtypeStruct((b, slabs, t, LANES), F32),
        scratch_shapes=[pltpu.VMEM((S5_BLOCK, width, nblk), BF16), pltpu.VMEM((S5_BLOCK, width, nblk), F32),
                        pltpu.VMEM((states, nblk), F32), pltpu.VMEM((states, nblk), F32),
                        pltpu.VMEM((states, nblk), BF16), pltpu.VMEM((8, states), F32)],
        compiler_params=_params(("parallel", "arbitrary")),
        name="s5",
    )(u, tmat, wmat, vmat, pa, pb)


def _merge_kernel(x_ref, mod_ref, n1_ref, wg_ref, yg_ref, ym_ref, ys_ref, gluw_ref, glub_ref, pg_ref, pm_ref,
                  ps_ref, gb_ref, wo_ref, nw_ref, o_ref, *, d):
    mod = mod_ref[0]
    rows = x_ref.shape[1] // SUBTILES
    for sub in range(SUBTILES):
        r = slice(sub * rows, (sub + 1) * rows)
        x = x_ref[0, r, :]
        hb = (_rms(x, n1_ref[...]) * (1.0 + mod[1:2]) + mod[0:1]).astype(BF16)
        s = _gelu_tanh(jnp.concatenate([ys_ref[0, k, r, :] for k in range(ys_ref.shape[1])], axis=1))
        s = s * _sigmoid(jnp.dot(s.astype(BF16), gluw_ref[...], preferred_element_type=F32) + glub_ref[...])
        merged = None
        for idx, (y, p) in enumerate(((yg_ref[0, r, :], pg_ref), (ym_ref[0, r, :], pm_ref), (s.astype(BF16), ps_ref))):
            cs = slice(idx * d, (idx + 1) * d)
            zg = jnp.dot(hb, wg_ref[:, cs], preferred_element_type=F32) + gb_ref[:, cs]
            term = (jnp.tanh(0.5 * zg) + 1.0) * jnp.dot(y, p[...], preferred_element_type=F32)
            merged = term if merged is None else merged + term
        y = jnp.dot((0.5 * merged).astype(BF16), wo_ref[...], preferred_element_type=F32)
        o_ref[0, r, :] = x + (1.0 + mod[2:3]) * _rms(y, nw_ref[...])


def _merge(x, mod, n1, wg, yg, ym, ys, gluw, glub, pg, pm, ps, gb, wo, nw, *, layer, tm):
    b, t, d = x.shape
    tok = lambda n: pl.BlockSpec((1, tm, n), lambda i, j: (i, j, 0))
    small = lambda a: _layer_spec(a, layer)
    big = lambda a: _layer_spec(a, layer, True)
    return pl.pallas_call(
        functools.partial(_merge_kernel, d=d),
        grid=(b, t // tm),
        in_specs=[tok(d), _mod_spec(mod, layer), small(n1), big(wg),
                  tok(HV), tok(HV), pl.BlockSpec((1, ys.shape[1], tm, LANES), lambda i, j: (i, 0, j, 0)),
                  big(gluw), small(glub), big(pg), big(pm), big(ps), small(gb), big(wo), small(nw)],
        out_specs=tok(d),
        out_shape=jax.ShapeDtypeStruct((b, t, d), F32),
        compiler_params=_params(("parallel", "parallel")),
        name="merge_out",
    )(x, mod, n1, wg, yg, ym, ys, gluw, glub, pg, pm, ps, gb, wo, nw)


def _ffn_kernel(x_ref, mod_ref, n1_ref, wi_ref, wo_ref, n2_ref, o_ref, *, hidden):
    mod = mod_ref[0]
    rows = x_ref.shape[1] // SUBTILES
    tile = lambda s: slice(s * rows, (s + 1) * rows)
    normed = lambda s: (_rms(x_ref[0, tile(s), :], n1_ref[...]) * (1.0 + mod[4:5]) + mod[3:4]).astype(BF16)

    def finish(s, y):
        o_ref[0, tile(s), :] = x_ref[0, tile(s), :] + (1.0 + mod[5:6]) * _rms(y, n2_ref[...])

    h_next, y_prev = normed(0), None
    for s in range(SUBTILES):
        h = h_next
        gate = jnp.dot(h, wi_ref[:, 0:hidden], preferred_element_type=F32)
        if s + 1 < SUBTILES:
            h_next = normed(s + 1)
        if y_prev is not None:
            finish(s - 1, y_prev)
        up = jnp.dot(h, wi_ref[:, hidden:2 * hidden], preferred_element_type=F32)
        act = (_silu(gate) * up).astype(BF16)
        y_prev = jnp.dot(act, wo_ref[...], preferred_element_type=F32)
    finish(SUBTILES - 1, y_prev)


def _ffn(x, mod, n1, wi, wo, n2, *, layer, tm):
    b, t, d = x.shape
    hidden = wo.shape[1]
    tok = pl.BlockSpec((1, tm, d), lambda i, j: (i, j, 0))
    return pl.pallas_call(
        functools.partial(_ffn_kernel, hidden=hidden),
        grid=(b, t // tm),
        in_specs=[tok, _mod_spec(mod, layer), _layer_spec(n1, layer), _layer_spec(wi, layer, True),
                  _layer_spec(wo, layer, True), _layer_spec(n2, layer)],
        out_specs=tok,
        out_shape=jax.ShapeDtypeStruct((b, t, d), F32),
        compiler_params=_params(("parallel", "parallel")),
        name="ffn",
    )(x, mod, n1, wi, wo, n2)


def _split_kernel(w_ref, *out_refs, spans):
    w = w_ref[0]
    for ref, (a, b) in zip(out_refs, spans):
        ref[0] = w[:, a:b].astype(ref.dtype)


def _split_w_in(w_in, d, s5w):
    sizes = (HK, HK, HV, RANK, HV, HK, HK, HV, HEADS, HEADS, HV, s5w, 3 * d)
    offs, a = [], 0
    for s in sizes:
        offs.append(a)
        a += s
    span = lambda i, j: (offs[i], offs[j] + sizes[j])
    spans = (span(0, 2), span(4, 4), span(5, 7), span(10, 10), span(11, 11), span(12, 12))
    nl, rows_total, width = w_in.shape
    rows = 128
    big = pl.pallas_call(
        functools.partial(_split_kernel, spans=spans),
        grid=(nl, rows_total // rows),
        in_specs=[pl.BlockSpec((1, rows, width), lambda l, i: (l, i, 0))],
        out_specs=[pl.BlockSpec((1, rows, b - a), lambda l, i: (l, i, 0)) for a, b in spans],
        out_shape=[jax.ShapeDtypeStruct((nl, rows_total, b - a), BF16) for a, b in spans],
        compiler_params=_params(("parallel", "parallel")),
        name="split_w_in",
    )(w_in)
    tiny = lambda i: w_in[..., offs[i]:offs[i] + sizes[i]].astype(BF16)
    zeros = lambda n: jnp.zeros(w_in.shape[:-1] + (n,), BF16)
    small = jnp.concatenate([tiny(3), tiny(8), zeros(LANES - RANK - HEADS),
                             zeros(GATE_LANE), tiny(9), zeros(LANES - GATE_LANE - HEADS)], axis=-1)
    return tuple(big[0:5]) + (small,), big[5]


def _lane_pad(vec, lane0):
    return jnp.zeros((vec.shape[0], 1, LANES), F32).at[:, 0, lane0:lane0 + vec.shape[1]].set(vec)


def kernel(x, c, ada_w, ada_b, pre1_w, post1_w, pre2_w, post2_w, w_in, gla_a2, gla_a_b, gla_norm_w, ml_conv_w, ml_conv_b, ml_i_b, ml_f_b, s5_lam_re, s5_lam_im, s5_log_step, s5_b_re, s5_b_im, s5_c_re, s5_c_im, s5_d, s5_glu_w, s5_glu_b, proj_gla, proj_ml, proj_s5, branch_gate_b, w_out, ffn_w_in, ffn_w_out):
    b, t, d = x.shape
    depth = ada_w.shape[0]
    s5w = s5_d.shape[1]
    assert RANK + HEADS <= GATE_LANE + HEADS <= LANES and GATE_LANE == RANK
    assert DV == LANES and DK % 8 == 0
    assert t % 256 == 0 and d % LANES == 0
    tm = 512 if t % 512 == 0 else 256
    tm_sub = SUBTILES * tm if t % (SUBTILES * tm) == 0 else tm
    tt = 1024 if t % 1024 == 0 else 256
    tt_s5 = 2048 if t % 2048 == 0 else t
    n_pow = max(1, (tt_s5 // S5_BLOCK - 1).bit_length())

    rows = lambda a: a.reshape(depth, 1, -1)
    bf = lambda a: a.astype(BF16)
    mod = _ada_mod(c, ada_w, ada_b).reshape(depth, b, 6, d)
    w_mix, w_gate = _split_w_in(w_in, d, s5w)
    a2p = jnp.zeros((depth, LANES, HK), BF16).at[:, 0:RANK].set(bf(gla_a2))
    ibp, fbp = _lane_pad(ml_i_b, GATE_LANE), _lane_pad(ml_f_b, GATE_LANE)
    s5_ops = _s5_prep(s5_lam_re, s5_lam_im, s5_log_step, s5_b_re, s5_b_im, s5_c_re, s5_c_im,
                      s5_d.reshape(depth, -1, S5_GROUP), n_pow=n_pow)
    glu_w, p_gla, p_ml, p_s5, wo, f_in, f_out = map(bf, (s5_glu_w, proj_gla, proj_ml, proj_s5, w_out,
                                                         ffn_w_in, ffn_w_out))
    for l in range(depth):
        zg, zm, zs5, zsm = _inproj(x, mod, rows(pre1_w), w_mix, layer=l, tm=tm_sub)
        y_gla, y_ml = _mixer(zg, zm, zsm, a2p, rows(gla_a_b), rows(gla_norm_w), ml_conv_w, rows(ml_conv_b),
                             ibp, fbp, layer=l, tt=tt)
        ys = _s5(zs5, *s5_ops, layer=l, tt=tt_s5)
        x = _merge(x, mod, rows(pre1_w), w_gate, y_gla, y_ml, ys, glu_w, rows(s5_glu_b), p_gla, p_ml, p_s5,
                   rows(branch_gate_b), wo, rows(post1_w), layer=l, tm=tm_sub)
        x = _ffn(x, mod, rows(pre2_w), f_in, f_out, rows(post2_w), layer=l, tm=tm_sub)
    return x
```

```python
import functools
import math

import jax
import jax.numpy as jnp
from jax import lax
from jax.experimental import pallas as pl
from jax.experimental.pallas import tpu as pltpu

HEADS = 4
DK = 64
DV = 128
RANK = 16
GATE_NORM = 16.0
CONV = 4
S5_GROUP = 16
S5_STATE = 64
CHUNK = 64
S5_BLOCK = 16
RMS_EPS = 1e-6
LANES = 128
GATE_LANE = 16
SUBTILES = 2
VMEM_LIMIT = 56 * 1024 * 1024

HK = HEADS * DK
HV = HEADS * DV
F32 = jnp.float32
BF16 = jnp.bfloat16
HIGHEST = lax.Precision.HIGHEST
NT = (((1,), (1,)), ((), ()))
TN = (((0,), (0,)), ((), ()))


def _params(sem):
    return pltpu.CompilerParams(dimension_semantics=sem, vmem_limit_bytes=VMEM_LIMIT)


def _layer_spec(a, layer, single_buffer=False):
    nd = a.ndim - 1
    kw = dict(pipeline_mode=pl.Buffered(1)) if single_buffer else {}
    return pl.BlockSpec((None,) + a.shape[1:], lambda *_: (layer,) + (0,) * nd, **kw)


def _mod_spec(mod, layer):
    return pl.BlockSpec((None, 1) + mod.shape[2:], lambda i, j: (layer, i, 0, 0))


def _sigmoid(x):
    return 0.5 * jnp.tanh(0.5 * x) + 0.5


def _silu(x):
    return x * _sigmoid(x)


def _log_sigmoid(x):
    return jnp.minimum(x, 0.0) - jnp.log(1.0 + jnp.exp(-jnp.abs(x)))


def _gelu_tanh(x):
    c = math.sqrt(2.0 / math.pi)
    return 0.5 * x * (1.0 + jnp.tanh(c * (x + 0.044715 * (x * x * x))))


def _rms(x, w):
    ms = jnp.mean(x * x, axis=-1, keepdims=True)
    return x * lax.rsqrt(ms + RMS_EPS) * w


def _ada_kernel(c_ref, w_ref, b_ref, o_ref):
    ca = _silu(c_ref[...]).astype(BF16)
    o_ref[0] = jnp.dot(ca, w_ref[0].astype(BF16), preferred_element_type=F32) + b_ref[0]


def _ada_mod(c, ada_w, ada_b):
    nl, d, n = ada_w.shape
    b = c.shape[0]
    tn = 1024
    return pl.pallas_call(
        _ada_kernel,
        grid=(nl, n // tn),
        in_specs=[pl.BlockSpec((b, d), lambda l, j: (0, 0)),
                  pl.BlockSpec((1, d, tn), lambda l, j: (l, 0, j)),
                  pl.BlockSpec((1, 1, tn), lambda l, j: (l, 0, j))],
        out_specs=pl.BlockSpec((1, b, tn), lambda l, j: (l, 0, j)),
        out_shape=jax.ShapeDtypeStruct((nl, b, n), F32),
        compiler_params=_params(("arbitrary", "arbitrary")),
        name="ada_mod",
    )(c, ada_w, ada_b.reshape(nl, 1, n))


N_GLA = 2 * HK + 2 * HV
N_ML = 2 * HK + 2 * HV
N_SMALL = 2 * LANES


def _inproj_kernel(x_ref, mod_ref, nw_ref, wg_qkv, wg_r, wm_qkv, wm_o, w_s5, w_small,
                   zg_ref, zm_ref, zs_ref, zsm_ref):
    mod = mod_ref[0]
    split = 2 * HK + HV
    n_sub = SUBTILES
    rows = x_ref.shape[1] // n_sub

    def normed(sub):
        x = x_ref[0, sub * rows:(sub + 1) * rows, :]
        return (_rms(x, nw_ref[...]) * (1.0 + mod[1:2]) + mod[0:1]).astype(BF16)

    hb_next = normed(0)
    for sub in range(n_sub):
        r = slice(sub * rows, (sub + 1) * rows)
        hb = hb_next
        proj = lambda w: jnp.dot(hb, w[...], preferred_element_type=F32)
        zg_ref[0, r, 0:split] = proj(wg_qkv).astype(zg_ref.dtype)
        if sub + 1 < n_sub:
            hb_next = normed(sub + 1)
        zg_ref[0, r, split:] = proj(wg_r).astype(zg_ref.dtype)
        zm_ref[0, r, 0:split] = proj(wm_qkv).astype(zm_ref.dtype)
        zm_ref[0, r, split:] = proj(wm_o).astype(zm_ref.dtype)
        s5 = proj(w_s5)
        for s in range(zs_ref.shape[1]):
            zs_ref[0, s, r, :] = s5[:, s * LANES:(s + 1) * LANES]
        zsm_ref[0, r, :] = proj(w_small)


def _inproj(x, mod, nw, weights, *, layer, tm):
    b, t, d = x.shape
    s5w = weights[4].shape[2]
    tok = lambda n: pl.BlockSpec((1, tm, n), lambda i, j: (i, j, 0))
    slabs = s5w // LANES
    return pl.pallas_call(
        _inproj_kernel,
        grid=(b, t // tm),
        in_specs=[tok(d), _mod_spec(mod, layer), _layer_spec(nw, layer)] + [_layer_spec(w, layer, True)
                                                                           for w in weights],
        out_specs=[tok(N_GLA), tok(N_ML), pl.BlockSpec((1, slabs, tm, LANES), lambda i, j: (i, 0, j, 0)),
                   tok(N_SMALL)],
        out_shape=[jax.ShapeDtypeStruct((b, t, N_GLA), BF16), jax.ShapeDtypeStruct((b, t, N_ML), BF16),
                   jax.ShapeDtypeStruct((b, slabs, t, LANES), F32),
                   jax.ShapeDtypeStruct((b, t, N_SMALL), F32)],
        compiler_params=_params(("parallel", "parallel")),
        name="in_proj",
    )(x, mod, nw, *weights)


def _block_diag_rows(a, width):
    lane_head = lax.broadcasted_iota(jnp.int32, a.shape, 1) // width
    zero = jnp.zeros_like(a)
    return jnp.concatenate([jnp.where(lane_head == h, a, zero) for h in range(HEADS)], axis=0)


def _scan_rows(x, period, op, fill):
    pos = lax.broadcasted_iota(jnp.int32, x.shape, 0) % period
    sh = 1
    while sh < period:
        x = op(x, jnp.where(pos >= sh, pltpu.roll(x, sh, 0), fill))
        sh *= 2
    return x


def _bcast_heads(tile, per_head):
    rows = tile.shape[0]
    full = [jnp.broadcast_to(tile[:, GATE_LANE + h:GATE_LANE + h + 1], (rows, LANES)) for h in range(HEADS)]
    if per_head == LANES:
        return jnp.concatenate(full, axis=1)
    assert 2 * per_head == LANES and HEADS % 2 == 0
    low = lax.broadcasted_iota(jnp.int32, (rows, LANES), 1) < per_head
    return jnp.concatenate([jnp.where(low, full[h], full[h + 1]) for h in range(0, HEADS, 2)], axis=1)


def _gla_levels(q, k, g, c):
    n = q.shape[0]
    row = lax.broadcasted_iota(jnp.int32, q.shape, 0)
    out = []
    hs = 1
    while 2 * hs <= n:
        blk = 2 * hs
        pos = row % blk
        upper = pos >= hs
        if hs == 1:
            x = jnp.where(upper, g, 0.0)
        elif hs == 2:
            g_prev = pltpu.roll(g, 1, 0)
            g_next = pltpu.roll(g, n - 1, 0)
            x = jnp.where(pos == 2, g, jnp.where(pos == 3, g + g_prev, jnp.where(pos == 0, g_next, 0.0)))
        elif hs % 8 != 0:
            c3 = c.reshape(n // blk, blk, c.shape[1])
            anchor = jnp.broadcast_to(c3[:, hs - 1:hs, :], c3.shape).reshape(c.shape)
            x = -jnp.abs(c - anchor)
        else:
            split = lambda a: a.reshape(n // blk, 2, hs, a.shape[1])
            c4, q4, k4 = split(c), split(q), split(k)
            anchor = c4[:, 0, hs - 1:hs, :]
            mixed = jnp.concatenate([k4[:, 0] * jnp.exp(anchor - c4[:, 0]), q4[:, 1] * jnp.exp(c4[:, 1] - anchor)],
                                    axis=1)
            out.append((mixed.reshape(q.shape), hs))
            hs = blk
            continue
        out.append((jnp.where(upper, q, k) * jnp.exp(x), hs))
        hs = blk
    return out


def _pair_block_diag(a):
    low = lax.broadcasted_iota(jnp.int32, a.shape, 1) < a.shape[1] // 2
    zero = jnp.zeros_like(a)
    return jnp.concatenate([jnp.where(low, a, zero), jnp.where(low, zero, a)], axis=0)


def _own_block(h):
    return slice(h * DK, (h + 1) * DK), slice(h * DV, (h + 1) * DV)


def _state_operand(st_ref):
    zero = jnp.zeros((DK, DV), BF16)
    return jnp.concatenate(
        [jnp.concatenate([st_ref[_own_block(h)].astype(BF16) if j == h else zero for j in range(HEADS)], axis=1)
         for h in range(HEADS)], axis=0)


def _gla_chunk(q, k, v, gr, g, c, nw, st_ref):
    L = q.shape[0]
    col = lax.broadcasted_iota(jnp.int32, (L, LANES), 1) % DK
    rowi = lax.broadcasted_iota(jnp.int32, (L, LANES), 0)
    qb, kb = q.astype(BF16), k.astype(BF16)
    levels = [(t.astype(BF16), hs) for t, hs in _gla_levels(q, k, g, c)]
    pairs = []
    for p in range(HK // LANES):
        ls = slice(p * LANES, (p + 1) * LANES)
        s = lax.dot_general(qb[:, ls], _pair_block_diag(kb[:, ls]), NT, preferred_element_type=F32)
        acc = jnp.where(rowi == col, s, 0.0)
        for tb, hs in levels:
            s = lax.dot_general(tb[:, ls], _pair_block_diag(tb[:, ls]), NT, preferred_element_type=F32)
            blk = 2 * hs
            valid = ((rowi // blk) == (col // blk)) & (rowi % blk >= hs) & (col % blk < hs)
            acc = acc + jnp.where(valid, s, 0.0)
        pairs.append(acc)
    scores = jnp.concatenate(pairs, axis=1)

    o = jnp.dot(scores.astype(BF16), _block_diag_rows(v, DV), preferred_element_type=F32)
    o = o + jnp.dot((q * jnp.exp(c)).astype(BF16), _state_operand(st_ref), preferred_element_type=F32)
    c_last = c[L - 1:L]
    kd = (k * jnp.exp(c_last - c)).astype(BF16)
    upd = lax.dot_general(kd, v, TN, preferred_element_type=F32)
    decay = jnp.broadcast_to(jnp.exp(c_last), (LANES, HK)).T
    for h in range(HEADS):
        blk = _own_block(h)
        st_ref[blk] = decay[blk[0], :] * st_ref[blk] + upd[blk]

    parts = []
    for h in range(HEADS):
        oh = o[:, h * DV:(h + 1) * DV]
        parts.append(oh * lax.rsqrt(jnp.mean(oh * oh, axis=-1, keepdims=True) + RMS_EPS))
    return jnp.concatenate(parts, axis=1) * nw * _silu(gr)


def _mlstm_chunk(q, k, v, og, u, bcum, cmax, mt_ref, nrm_ref, m_ref):
    L = q.shape[0]
    lane = lax.broadcasted_iota(jnp.int32, (L, LANES), 1)
    gate_lanes = (lane >= GATE_LANE) & (lane < GATE_LANE + HEADS)
    m_prev = m_ref[...]
    big_m = jnp.maximum(m_prev, cmax)
    m_t = bcum + big_m
    m_new = m_t[L - 1:L]
    b_last = bcum[L - 1:L]

    ut = jnp.concatenate([u, u], axis=0).T
    low = lax.broadcasted_iota(jnp.int32, (1, 2 * L), 1) < L
    urow = jnp.concatenate([jnp.where(low, ut[GATE_LANE + h:GATE_LANE + h + 1], ut[GATE_LANE + h + 1:GATE_LANE + h + 2])
                            for h in range(0, HEADS, 2)], axis=1)
    causal = (lax.broadcasted_iota(jnp.int32, (L, HK), 1) % DK) <= lax.broadcasted_iota(jnp.int32, (L, HK), 0)
    dmat = _bcast_heads(-big_m, DK) + urow
    pmat = jnp.where(causal, jnp.exp(jnp.where(causal, dmat, 0.0)), 0.0)

    qb = q.astype(BF16)
    qk_s = lax.dot_general(qb, _block_diag_rows(k.astype(BF16), DK), NT, preferred_element_type=F32)
    sm = (qk_s * pmat).astype(BF16)

    den_lane = (lax.broadcasted_iota(jnp.int32, (HK, LANES), 0) // DK + GATE_LANE
                == lax.broadcasted_iota(jnp.int32, (HK, LANES), 1))
    nrm_cols = jnp.where(den_lane, nrm_ref[...], 0.0).astype(BF16)
    inter = jnp.dot(qb, jnp.concatenate([_state_operand(mt_ref), nrm_cols], axis=1),
                    preferred_element_type=F32)
    v_ext = jnp.concatenate([_block_diag_rows(v, DV), den_lane.astype(BF16)], axis=1)
    intra = jnp.dot(sm, v_ext, preferred_element_type=F32)

    lw = jnp.where(gate_lanes, m_prev - big_m, 0.0)
    den = jnp.exp(lw) * inter[:, HV:] + intra[:, HV:]
    rden = 1.0 / jnp.maximum(jnp.abs(den), jnp.exp(-m_t))
    hout = (jnp.exp(_bcast_heads(lw, DV)) * inter[:, 0:HV] + intra[:, 0:HV]) * _bcast_heads(rden, DV)

    wk = jnp.where(gate_lanes, b_last - m_new + u, 0.0)
    cd = jnp.where(gate_lanes[0:1], b_last + m_prev - m_new, 0.0)
    kw = (k * jnp.exp(_bcast_heads(wk, DK))).astype(BF16)
    upd = lax.dot_general(kw, jnp.concatenate([v, jnp.ones((L, LANES), BF16)], axis=1), TN,
                          preferred_element_type=F32)
    decay = jnp.exp(cd)
    for h in range(HEADS):
        blk = _own_block(h)
        dh = jnp.broadcast_to(decay[:, GATE_LANE + h:GATE_LANE + h + 1], (DK, LANES))
        mt_ref[blk] = dh * mt_ref[blk] + upd[blk]
        nrm_ref[blk[0], :] = dh * nrm_ref[blk[0], :] + upd[blk[0], HV:]
    m_ref[...] = m_new
    return _sigmoid(og) * hout


def _mixers(zg_ref, zm_ref, zs_ref, a2_ref, ab_ref, gnw_ref, cw_ref, cb_ref, ib_ref, fb_ref,
            yg_ref, ym_ref, st_ref, mt_ref, nrm_ref, m_ref, ext_ref, n_chunks):
    L = CHUNK
    tt = n_chunks * L

    zs_a = zs_ref[:, 0:LANES]
    xg = jnp.dot(zs_a.astype(BF16), a2_ref[...], preferred_element_type=F32) + ab_ref[...]
    g = _log_sigmoid(xg) * (1.0 / GATE_NORM)
    c = _scan_rows(g, L, jnp.add, 0.0)

    qk_in = zm_ref[:, 0:2 * HK].astype(F32)
    ext_ref[8:8 + tt, :] = qk_in
    acc = jnp.zeros((tt, 2 * HK), F32) + cb_ref[...]
    for j in range(CONV):
        off = 8 - (CONV - 1) + j
        acc = acc + cw_ref[j:j + 1, :] * ext_ref[off:off + tt, :]
    ext_ref[0:8, :] = qk_in[tt - 8:tt]
    qk = _silu(acc)

    lane = lax.broadcasted_iota(jnp.int32, (tt, LANES), 1)
    gate_lanes = (lane >= GATE_LANE) & (lane < GATE_LANE + HEADS)
    ig = jnp.where(gate_lanes, zs_a + ib_ref[...], 0.0)
    lf = jnp.where(gate_lanes, _log_sigmoid(zs_ref[:, LANES:2 * LANES] + fb_ref[...]), 0.0)
    bcum = _scan_rows(lf, L, jnp.add, 0.0)
    u = ig - bcum
    cmax = _scan_rows(u, L, jnp.maximum, -jnp.inf)

    for ci in range(n_chunks):
        r = slice(ci * L, (ci + 1) * L)
        yg = _gla_chunk(zg_ref[r, 0:HK].astype(F32) * (DK ** -0.5), zg_ref[r, HK:2 * HK].astype(F32),
                        zg_ref[r, 2 * HK:2 * HK + HV], zg_ref[r, 2 * HK + HV:2 * HK + 2 * HV].astype(F32),
                        g[r], c[r], gnw_ref[...], st_ref)
        yg_ref[r, :] = yg.astype(yg_ref.dtype)
        ym = _mlstm_chunk(qk[r, 0:HK], qk[r, HK:2 * HK] * (DK ** -0.5), zm_ref[r, 2 * HK:2 * HK + HV],
                          zm_ref[r, 2 * HK + HV:2 * HK + 2 * HV].astype(F32), u[r], bcum[r], cmax[r],
                          mt_ref, nrm_ref, m_ref)
        ym_ref[r, :] = ym.astype(ym_ref.dtype)


def _mixer_kernel(zg_ref, zm_ref, zs_ref, a2_ref, ab_ref, gnw_ref, cw_ref, cb_ref, ib_ref, fb_ref,
                  yg_ref, ym_ref, st_ref, mt_ref, nrm_ref, m_ref, ext_ref, *, n_chunks):
    @pl.when(pl.program_id(1) == 0)
    def _():
        for ref in (st_ref, mt_ref, nrm_ref, m_ref):
            ref[...] = jnp.zeros_like(ref)
        ext_ref[0:8, :] = jnp.zeros((8, 2 * HK), F32)

    _mixers(zg_ref.at[0], zm_ref.at[0], zs_ref.at[0], a2_ref, ab_ref, gnw_ref, cw_ref, cb_ref, ib_ref, fb_ref,
            yg_ref.at[0], ym_ref.at[0], st_ref, mt_ref, nrm_ref, m_ref, ext_ref, n_chunks)


def _mixer(zg, zm, zs, a2p, ab, gnw, cw, cb, ibp, fbp, *, layer, tt):
    b, t, _ = zg.shape
    tok = lambda n: pl.BlockSpec((1, tt, n), lambda i, j: (i, j, 0))
    return pl.pallas_call(
        functools.partial(_mixer_kernel, n_chunks=tt // CHUNK),
        grid=(b, t // tt),
        in_specs=[tok(N_GLA), tok(N_ML), tok(N_SMALL)] + [_layer_spec(a, layer)
                                                           for a in (a2p, ab, gnw, cw, cb, ibp, fbp)],
        out_specs=[tok(HV), tok(HV)],
        out_shape=[jax.ShapeDtypeStruct((b, t, HV), BF16)] * 2,
        scratch_shapes=[pltpu.VMEM((HK, HV), F32), pltpu.VMEM((HK, HV), F32), pltpu.VMEM((HK, LANES), F32),
                        pltpu.VMEM((1, LANES), F32), pltpu.VMEM((8 + tt, 2 * HK), F32)],
        compiler_params=_params(("parallel", "arbitrary")),
        name="mixer",
    )(zg, zm, zs, a2p, ab, gnw, cw, cb, ibp, fbp)


def _s5_prep_kernel(lr_ref, li_ref, lrc_ref, lic_ref, dt_ref, btr_ref, bti_ref, ctr_ref, cti_ref, d_ref,
                    t_ref, w_ref, v_ref, pa_ref, pb_ref, *, n_pow):
    P = S5_STATE
    nb = S5_BLOCK
    dt = dt_ref[0]
    lr = jnp.minimum(lr_ref[0], -1e-4)
    li = li_ref[0]
    lrc = jnp.minimum(lrc_ref[0], -1e-4)
    lic = lic_ref[0]
    dtr = jnp.exp(dt)

    def power(lr_, li_, n):
        mag = jnp.exp(lr_ * dtr * n)
        ang = li_ * dtr * n
        return mag * jnp.cos(ang), mag * jnp.sin(ang)

    n_rows = jnp.where(lax.broadcasted_iota(jnp.int32, (nb + 8, 1), 0) < nb,
                       nb - 1 - lax.broadcasted_iota(jnp.int32, (nb + 8, 1), 0),
                       jnp.where(lax.broadcasted_iota(jnp.int32, (nb + 8, 1), 0) == nb, 1, nb)).astype(F32)
    row_r, row_i = power(lr, li, n_rows)
    col_r, col_i = power(lrc, lic, lax.broadcasted_iota(jnp.int32, (P, LANES), 1).astype(F32))

    ar1, ai1 = row_r[nb:nb + 1], row_i[nb:nb + 1]
    inv = 1.0 / (lr * lr + li * li)
    zr = ((ar1 - 1.0) * lr + ai1 * li) * inv
    zi = (ai1 * lr - (ar1 - 1.0) * li) * inv
    bbr = zr * btr_ref[0] - zi * bti_ref[0]
    bbi = zr * bti_ref[0] + zi * btr_ref[0]

    ctr, cti = ctr_ref[0], cti_ref[0]
    src = lax.broadcasted_iota(jnp.int32, (LANES, nb * S5_GROUP), 0)
    dst_t = lax.broadcasted_iota(jnp.int32, (LANES, nb * S5_GROUP), 1) // S5_GROUP

    def out_map(shift):
        expand = (src == dst_t + shift).astype(F32)
        ar = jnp.dot(col_r, expand, precision=HIGHEST, preferred_element_type=F32)
        ai = jnp.dot(col_i, expand, precision=HIGHEST, preferred_element_type=F32)
        return jnp.concatenate([ctr * ar - cti * ai, -(ctr * ai + cti * ar)], axis=0)

    v_ref[0] = out_map(1).T.astype(v_ref.dtype)
    base = jnp.dot(jnp.concatenate([bbr, bbi], axis=1), out_map(0), precision=HIGHEST,
                   preferred_element_type=F32)
    lane = lax.broadcasted_iota(jnp.int32, base.shape, 1)
    base = base + jnp.where(lane == lax.broadcasted_iota(jnp.int32, base.shape, 0), d_ref[0], 0.0)
    rows = [base]
    for s in range(1, nb):
        rows.append(jnp.where(lane >= s * S5_GROUP, pltpu.roll(base, s * S5_GROUP, 1), 0.0))
    t_ref[0] = jnp.concatenate(rows, axis=0).T.astype(t_ref.dtype)

    pick = (lax.broadcasted_iota(jnp.int32, (nb * S5_GROUP, nb + 8), 0) // S5_GROUP
            == lax.broadcasted_iota(jnp.int32, (nb * S5_GROUP, nb + 8), 1)).astype(F32)
    rep = lambda a: jnp.dot(pick, a, precision=HIGHEST, preferred_element_type=F32)
    ar, ai = rep(row_r), rep(row_i)
    bbr_t = jnp.concatenate([bbr] * nb, axis=0)
    bbi_t = jnp.concatenate([bbi] * nb, axis=0)
    wt = jnp.concatenate([ar * bbr_t - ai * bbi_t, ar * bbi_t + ai * bbr_t], axis=1).T
    w_ref[0] = jnp.concatenate([wt, wt[P:2 * P], wt[0:P]], axis=0).astype(w_ref.dtype)

    pr, pi = row_r[nb + 1:nb + 2], row_i[nb + 1:nb + 2]
    pa, pb = [], []
    for _ in range(n_pow):
        pa.append(jnp.concatenate([pr, pr], axis=1))
        pb.append(jnp.concatenate([-pi, pi], axis=1))
        pr, pi = pr * pr - pi * pi, 2.0 * pr * pi
    pa_ref[0] = jnp.concatenate(pa, axis=0)
    pb_ref[0] = jnp.concatenate(pb, axis=0)


def _s5_prep(lam_re, lam_im, log_step, b_re, b_im, c_re, c_im, d_skip, *, n_pow):
    nl, g, p = lam_re.shape
    nbg = S5_BLOCK * S5_GROUP
    bt = lambda a: jnp.swapaxes(a, 2, 3)
    ct = lambda a: jnp.tile(jnp.swapaxes(a, 2, 3), (1, 1, 1, S5_BLOCK))
    row = lambda a: a.reshape(nl, g, 1, p)
    colv = lambda a: a.reshape(nl, g, p, 1)
    dtile = jnp.tile(d_skip.reshape(nl, g, 1, S5_GROUP), (1, 1, 1, S5_BLOCK))
    spec = lambda *s: pl.BlockSpec((None, 1) + s, lambda l, i: (l, i) + (0,) * len(s))
    shape = lambda dt, *s: jax.ShapeDtypeStruct((nl, g) + s, dt)
    return pl.pallas_call(
        functools.partial(_s5_prep_kernel, n_pow=n_pow),
        grid=(nl, g),
        in_specs=[spec(1, p), spec(1, p), spec(p, 1), spec(p, 1), spec(1, 1),
                  spec(S5_GROUP, p), spec(S5_GROUP, p), spec(p, nbg), spec(p, nbg), spec(1, nbg)],
        out_specs=[spec(nbg, nbg), spec(4 * p, nbg), spec(nbg, 2 * p), spec(n_pow, 2 * p), spec(n_pow, 2 * p)],
        out_shape=[shape(BF16, nbg, nbg), shape(BF16, 4 * p, nbg), shape(BF16, nbg, 2 * p),
                   shape(F32, n_pow, 2 * p), shape(F32, n_pow, 2 * p)],
        compiler_params=_params(("parallel", "parallel")),
        name="s5_prep",
    )(row(lam_re), row(lam_im), colv(lam_re), colv(lam_im), log_step.reshape(nl, g, 1, 1),
      bt(b_re), bt(b_im), ct(c_re), ct(c_im), dtile)


S5_SCAN_GROUPS = 4


def _s5_kernel(u_ref, t_ref, w_ref, v_ref, pa_ref, pb_ref, y_ref,
               ut_ref, yt_ref, zt_ref, zw_ref, xin_ref, xc_ref, *, groups, n_pow):
    nb = S5_BLOCK
    p2 = 2 * S5_STATE
    slabs, tt = u_ref.shape[1], u_ref.shape[2]
    nblk = tt // nb

    @pl.when(pl.program_id(1) == 0)
    def _():
        xc_ref[...] = jnp.zeros_like(xc_ref)

    gps = S5_SCAN_GROUPS
    n_chunks = groups // gps
    chunks_per_slab = LANES // S5_GROUP // gps

    def relayout_in(s):
        for tl in range(nb):
            ut_ref[tl, s * LANES:(s + 1) * LANES, :] = (
                u_ref[0, s, pl.ds(tl, nblk, stride=nb), :].T.astype(ut_ref.dtype))

    def relayout_out(s):
        for tl in range(nb):
            y_ref[0, s, pl.ds(tl, nblk, stride=nb), :] = yt_ref[tl, s * LANES:(s + 1) * LANES, :].T

    def group_input(g):
        return ut_ref[:, g * S5_GROUP:(g + 1) * S5_GROUP, :].reshape(nb * S5_GROUP, nblk)

    def state_in(ci):
        for g in range(ci * gps, (ci + 1) * gps):
            z2 = jnp.dot(w_ref[g], group_input(g), preferred_element_type=F32)
            zt_ref[g * p2:(g + 1) * p2, :] = z2[0:p2]
            zw_ref[g * p2:(g + 1) * p2, :] = z2[p2:2 * p2]

    width = gps * p2
    row = lax.broadcasted_iota(jnp.int32, (nblk, width), 0)

    def scan(ci):
        l0 = ci * width
        x = zt_ref[pl.ds(l0, width), :].T
        xw = zw_ref[pl.ds(l0, width), :].T
        pa = jnp.concatenate([pa_ref[g] for g in range(ci * gps, (ci + 1) * gps)], axis=1)
        pb = jnp.concatenate([pb_ref[g] for g in range(ci * gps, (ci + 1) * gps)], axis=1)
        xc = xc_ref[0:1, pl.ds(l0, width)]
        xcw = xc_ref[1:2, pl.ds(l0, width)]
        first = row == 0
        x = x + jnp.where(first, xc * pa[0:1] + xcw * pb[0:1], 0.0)
        xw = xw + jnp.where(first, xcw * pa[0:1] - xc * pb[0:1], 0.0)
        for k in range(n_pow):
            sh = 1 << k
            if sh >= nblk:
                break
            xs, xws, m = pltpu.roll(x, sh, 0), pltpu.roll(xw, sh, 0), row >= sh
            x, xw = (x + jnp.where(m, xs * pa[k:k + 1] + xws * pb[k:k + 1], 0.0),
                     xw + jnp.where(m, xws * pa[k:k + 1] - xs * pb[k:k + 1], 0.0))
        x_in = jnp.where(row >= 1, pltpu.roll(x, 1, 0), xc)
        xc_ref[0:1, pl.ds(l0, width)] = x[nblk - 1:nblk]
        xc_ref[1:2, pl.ds(l0, width)] = xw[nblk - 1:nblk]
        xin_ref[pl.ds(l0, width), :] = x_in.T.astype(xin_ref.dtype)

    def block_out(ci):
        for g in range(ci * gps, (ci + 1) * gps):
            yt = jnp.dot(t_ref[g], group_input(g), preferred_element_type=F32)
            yt = yt + jnp.dot(v_ref[g], xin_ref[g * p2:(g + 1) * p2, :], preferred_element_type=F32)
            yt_ref[:, g * S5_GROUP:(g + 1) * S5_GROUP, :] = yt.reshape(nb, S5_GROUP, nblk)

    relayout_in(0)
    for step in range(n_chunks + 2):
        nxt = step // chunks_per_slab + 1
        if step % chunks_per_slab == 0 and nxt < slabs:
            relayout_in(nxt)
        if step < n_chunks:
            state_in(step)
        if 1 <= step <= n_chunks:
            scan(step - 1)
        if step >= 2:
            block_out(step - 2)
            if (step - 2) % chunks_per_slab == chunks_per_slab - 1:
                relayout_out((step - 2) // chunks_per_slab)


def _s5(u, tmat, wmat, vmat, pa, pb, *, layer, tt):
    b, slabs, t, _ = u.shape
    width = slabs * LANES
    groups = width // S5_GROUP
    assert groups % S5_SCAN_GROUPS == 0
    n_pow = pa.shape[2]
    nblk = tt // S5_BLOCK
    states = groups * 2 * S5_STATE
    tok = pl.BlockSpec((1, slabs, tt, LANES), lambda i, j: (i, 0, j, 0))
    return pl.pallas_call(
        functools.partial(_s5_kernel, groups=groups, n_pow=n_pow),
        grid=(b, t // tt),
        in_specs=[tok] + [_layer_spec(a, layer, True) for a in (tmat, wmat, vmat, pa, pb)],
        out_specs=tok,
        out_shape=jax.ShapeDtypeStruct((b, slabs, t, LANES), F32),
        scratch_shapes=[pltpu.VMEM((S5_BLOCK, width, nblk), BF16), pltpu.VMEM((S5_BLOCK, width, nblk), F32),
                        pltpu.VMEM((states, nblk), F32), pltpu.VMEM((states, nblk), F32),
                        pltpu.VMEM((states, nblk), BF16), pltpu.VMEM((8, states), F32)],
        compiler_params=_params(("parallel", "arbitrary")),
        name="s5",
    )(u, tmat, wmat, vmat, pa, pb)


def _merge_kernel(x_ref, mod_ref, n1_ref, wg_ref, yg_ref, ym_ref, ys_ref, gluw_ref, glub_ref, pg_ref, pm_ref,
                  ps_ref, gb_ref, wo_ref, nw_ref, o_ref, *, d):
    mod = mod_ref[0]
    rows = x_ref.shape[1] // SUBTILES
    for sub in range(SUBTILES):
        r = slice(sub * rows, (sub + 1) * rows)
        x = x_ref[0, r, :]
        hb = (_rms(x, n1_ref[...]) * (1.0 + mod[1:2]) + mod[0:1]).astype(BF16)
        s = _gelu_tanh(jnp.concatenate([ys_ref[0, k, r, :] for k in range(ys_ref.shape[1])], axis=1))
        s = s * _sigmoid(jnp.dot(s.astype(BF16), gluw_ref[...], preferred_element_type=F32) + glub_ref[...])
        merged = None
        for idx, (y, p) in enumerate(((yg_ref[0, r, :], pg_ref), (ym_ref[0, r, :], pm_ref), (s.astype(BF16), ps_ref))):
            cs = slice(idx * d, (idx + 1) * d)
            zg = jnp.dot(hb, wg_ref[:, cs], preferred_element_type=F32) + gb_ref[:, cs]
            term = (jnp.tanh(0.5 * zg) + 1.0) * jnp.dot(y, p[...], preferred_element_type=F32)
            merged = term if merged is None else merged + term
        y = jnp.dot((0.5 * merged).astype(BF16), wo_ref[...], preferred_element_type=F32)
        o_ref[0, r, :] = x + (1.0 + mod[2:3]) * _rms(y, nw_ref[...])


def _merge(x, mod, n1, wg, yg, ym, ys, gluw, glub, pg, pm, ps, gb, wo, nw, *, layer, tm):
    b, t, d = x.shape
    tok = lambda n: pl.BlockSpec((1, tm, n), lambda i, j: (i, j, 0))
    small = lambda a: _layer_spec(a, layer)
    big = lambda a: _layer_spec(a, layer, True)
    return pl.pallas_call(
        functools.partial(_merge_kernel, d=d),
        grid=(b, t // tm),
        in_specs=[tok(d), _mod_spec(mod, layer), small(n1), big(wg),
                  tok(HV), tok(HV), pl.BlockSpec((1, ys.shape[1], tm, LANES), lambda i, j: (i, 0, j, 0)),
                  big(gluw), small(glub), big(pg), big(pm), big(ps), small(gb), big(wo), small(nw)],
        out_specs=tok(d),
        out_shape=jax.ShapeDtypeStruct((b, t, d), F32),
        compiler_params=_params(("parallel", "parallel")),
        name="merge_out",
    )(x, mod, n1, wg, yg, ym, ys, gluw, glub, pg, pm, ps, gb, wo, nw)


def _ffn_kernel(x_ref, mod_ref, n1_ref, wi_ref, wo_ref, n2_ref, o_ref, *, hidden):
    mod = mod_ref[0]
    rows = x_ref.shape[1] // SUBTILES
    tile = lambda s: slice(s * rows, (s + 1) * rows)
    normed = lambda s: (_rms(x_ref[0, tile(s), :], n1_ref[...]) * (1.0 + mod[4:5]) + mod[3:4]).astype(BF16)

    def finish(s, y):
        o_ref[0, tile(s), :] = x_ref[0, tile(s), :] + (1.0 + mod[5:6]) * _rms(y, n2_ref[...])

    h_next, y_prev = normed(0), None
    for s in range(SUBTILES):
        h = h_next
        gate = jnp.dot(h, wi_ref[:, 0:hidden], preferred_element_type=F32)
        if s + 1 < SUBTILES:
            h_next = normed(s + 1)
        if y_prev is not None:
            finish(s - 1, y_prev)
        up = jnp.dot(h, wi_ref[:, hidden:2 * hidden], preferred_element_type=F32)
        act = (_silu(gate) * up).astype(BF16)
        y_prev = jnp.dot(act, wo_ref[...], preferred_element_type=F32)
    finish(SUBTILES - 1, y_prev)


def _ffn(x, mod, n1, wi, wo, n2, *, layer, tm):
    b, t, d = x.shape
    hidden = wo.shape[1]
    tok = pl.BlockSpec((1, tm, d), lambda i, j: (i, j, 0))
    return pl.pallas_call(
        functools.partial(_ffn_kernel, hidden=hidden),
        grid=(b, t // tm),
        in_specs=[tok, _mod_spec(mod, layer), _layer_spec(n1, layer), _layer_spec(wi, layer, True),
                  _layer_spec(wo, layer, True), _layer_spec(n2, layer)],
        out_specs=tok,
        out_shape=jax.ShapeDtypeStruct((b, t, d), F32),
        compiler_params=_params(("parallel", "parallel")),
        name="ffn",
    )(x, mod, n1, wi, wo, n2)


def _split_w_in(w_in, d, s5w):
    sizes = (HK, HK, HV, RANK, HV, HK, HK, HV, HEADS, HEADS, HV, s5w, 3 * d)
    offs, a = [], 0
    for s in sizes:
        offs.append(a)
        a += s
    def cols(i, j=None):
        j = i if j is None else j
        return w_in[..., offs[i]:offs[j] + sizes[j]].astype(BF16)

    zeros = lambda n: jnp.zeros(w_in.shape[:-1] + (n,), BF16)
    small = jnp.concatenate([cols(3), cols(8), zeros(LANES - RANK - HEADS),
                             zeros(GATE_LANE), cols(9), zeros(LANES - GATE_LANE - HEADS)], axis=-1)
    return (cols(0, 2), cols(4), cols(5, 7), cols(10), cols(11), small), cols(12)


def _lane_pad(vec, lane0):
    return jnp.zeros((vec.shape[0], 1, LANES), F32).at[:, 0, lane0:lane0 + vec.shape[1]].set(vec)


def kernel(x, c, ada_w, ada_b, pre1_w, post1_w, pre2_w, post2_w, w_in, gla_a2, gla_a_b, gla_norm_w, ml_conv_w, ml_conv_b, ml_i_b, ml_f_b, s5_lam_re, s5_lam_im, s5_log_step, s5_b_re, s5_b_im, s5_c_re, s5_c_im, s5_d, s5_glu_w, s5_glu_b, proj_gla, proj_ml, proj_s5, branch_gate_b, w_out, ffn_w_in, ffn_w_out):
    b, t, d = x.shape
    depth = ada_w.shape[0]
    s5w = s5_d.shape[1]
    assert RANK + HEADS <= GATE_LANE + HEADS <= LANES and GATE_LANE == RANK
    assert DV == LANES and DK % 8 == 0
    assert t % 256 == 0 and d % LANES == 0
    tm = 512 if t % 512 == 0 else 256
    tm_sub = SUBTILES * tm if t % (SUBTILES * tm) == 0 else tm
    tt = 1024 if t % 1024 == 0 else 256
    tt_s5 = 2048 if t % 2048 == 0 else t
    n_pow = max(1, (tt_s5 // S5_BLOCK - 1).bit_length())

    rows = lambda a: a.reshape(depth, 1, -1)
    bf = lambda a: a.astype(BF16)
    mod = _ada_mod(c, ada_w, ada_b).reshape(depth, b, 6, d)
    w_mix, w_gate = _split_w_in(w_in, d, s5w)
    a2p = jnp.zeros((depth, LANES, HK), BF16).at[:, 0:RANK].set(bf(gla_a2))
    ibp, fbp = _lane_pad(ml_i_b, GATE_LANE), _lane_pad(ml_f_b, GATE_LANE)
    s5_ops = _s5_prep(s5_lam_re, s5_lam_im, s5_log_step, s5_b_re, s5_b_im, s5_c_re, s5_c_im,
                      s5_d.reshape(depth, -1, S5_GROUP), n_pow=n_pow)
    glu_w, p_gla, p_ml, p_s5, wo, f_in, f_out = map(bf, (s5_glu_w, proj_gla, proj_ml, proj_s5, w_out,
                                                         ffn_w_in, ffn_w_out))
    for l in range(depth):
        zg, zm, zs5, zsm = _inproj(x, mod, rows(pre1_w), w_mix, layer=l, tm=tm_sub)
        y_gla, y_ml = _mixer(zg, zm, zsm, a2p, rows(gla_a_b), rows(gla_norm_w), ml_conv_w, rows(ml_conv_b),
                             ibp, fbp, layer=l, tt=tt)
        ys = _s5(zs5, *s5_ops, layer=l, tt=tt_s5)
        x = _merge(x, mod, rows(pre1_w), w_gate, y_gla, y_ml, ys, glu_w, rows(s5_glu_b), p_gla, p_ml, p_s5,
                   rows(branch_gate_b), wo, rows(post1_w), layer=l, tm=tm_sub)
        x = _ffn(x, mod, rows(pre2_w), f_in, f_out, rows(post2_w), layer=l, tm=tm_sub)
    return x
```
